```python
import jax, jax.numpy as jnp
from jax import lax
import numpy as np

D_MODEL = 1024
BATCH = 8
SEQ = 2048
DEPTH = 4
DEC_BATCH = 32
DEC_SEQ = 8
PAST_LEN = 16384
PAGE_SIZE = 128

GLA_HEADS = 4
GLA_DK = 32
GLA_DV = 64
GLA_WIDTH = GLA_HEADS * GLA_DV
GLA_LOWRANK = 16
GLA_GATE_TAU = 16.0
GLA_CHUNK = 32
LRU_WIDTH = 256
LRU_BLOCKS = 4
LRU_BLOCK_W = LRU_WIDTH // LRU_BLOCKS
CONV_WIDTH = 4
LRU_C = 8.0
MLA_HEADS = 8
MLA_NOPE = 64
MLA_ROPE = 32
MLA_V = 64
MLA_WIDTH = MLA_HEADS * MLA_V
MLA_Q_LORA = 256
MLA_KV_LORA = 128
ROPE_THETA = 10000.0
ATTN_Q_BLOCK = 128
MIX_WIDTH = GLA_WIDTH + LRU_WIDTH + MLA_WIDTH
D_FF = -(-8 * D_MODEL // (3 * 256)) * 256
DEEPNORM_ALPHA = (2.0 * DEPTH) ** 0.25
DEEPNORM_BETA = (8.0 * DEPTH) ** -0.25
IN_SIZES = (GLA_HEADS * GLA_DK, GLA_HEADS * GLA_DK, GLA_WIDTH, GLA_WIDTH, GLA_LOWRANK,
            LRU_WIDTH, LRU_WIDTH, MLA_Q_LORA, MLA_KV_LORA, MLA_ROPE)
IN_COLS = sum(IN_SIZES)

kernel_name = 'hybrid_gla_rglru_mla_deepnorm_adaln_step'


def _layer_norm(x, g, b, eps=1e-5):
    xf = x.astype(jnp.float32)
    mu = jnp.mean(xf, -1, keepdims=True)
    var = jnp.mean(jnp.square(xf - mu), -1, keepdims=True)
    return ((xf - mu) * lax.rsqrt(var + eps) * g + b).astype(x.dtype)


def _rms_norm(x, g, eps=1e-6):
    xf = x.astype(jnp.float32)
    return (xf * lax.rsqrt(jnp.mean(xf * xf, -1, keepdims=True) + eps) * g).astype(x.dtype)


def _rope(x, pos):
    half = MLA_ROPE // 2
    inv_freq = ROPE_THETA ** (-jnp.arange(half, dtype=jnp.float32) / half)
    ang = pos.astype(jnp.float32)[:, None] * inv_freq[None, :]
    ang = ang.reshape(ang.shape[0], *([1] * (x.ndim - 3)), half)
    cos, sin = jnp.cos(ang), jnp.sin(ang)
    xf = x.astype(jnp.float32)
    x1, x2 = xf[..., :half], xf[..., half:]
    return jnp.concatenate([x1 * cos - x2 * sin, x2 * cos + x1 * sin], -1).astype(x.dtype)


def _gla(q, k, v, logf, s0):
    B, T = q.shape[:2]
    C = GLA_CHUNK if T % GLA_CHUNK == 0 else T
    n = T // C

    def to_chunks(a):
        return a.astype(jnp.float32).reshape(B, n, C, *a.shape[2:]).transpose(1, 0, 3, 2, 4)

    causal = jnp.tril(jnp.ones((C, C), dtype=bool))[:, :, None]

    def step(s, inp):
        qb, kb, vb, gb = inp
        bcum = jnp.cumsum(gb, axis=2)
        diff = bcum[:, :, :, None, :] - bcum[:, :, None, :, :]
        decay = jnp.exp(jnp.where(causal, diff, -jnp.inf))
        attn = jnp.einsum('bhtk,bhsk,bhtsk->bhts', qb, kb, decay)
        o = jnp.einsum('bhts,bhsv->bhtv', attn, vb) + jnp.einsum('bhtk,bhkv->bhtv', qb * jnp.exp(bcum), s)
        b_last = bcum[:, :, -1:, :]
        s_new = jnp.exp(b_last[:, :, 0, :])[..., None] * s + jnp.einsum('bhsk,bhsv->bhkv', kb * jnp.exp(b_last - bcum), vb)
        return s_new, o

    s_fin, o = lax.scan(step, s0.astype(jnp.float32), (to_chunks(q), to_chunks(k), to_chunks(v), to_chunks(logf)))
    o = o.transpose(1, 0, 3, 2, 4).reshape(B, T, GLA_HEADS, GLA_DV)
    return o.astype(q.dtype), s_fin.astype(s0.dtype)


def _causal_conv(x, buf, w, b):
    T = x.shape[1]
    xp = jnp.concatenate([buf.astype(x.dtype), x], axis=1)
    y = b
    for kk in range(CONV_WIDTH):
        y = y + xp[:, kk:kk + T] * w[kk]
    return y, xp[:, T:]


def _block_diag(x, w):
    B, T, _ = x.shape
    return jnp.einsum('btnj,njk->btnk', x.reshape(B, T, LRU_BLOCKS, LRU_BLOCK_W), w).reshape(B, T, LRU_WIDTH)


def _rg_lru(x, h0, w_a, b_a, w_x, b_x, lam):
    xf = x.astype(jnp.float32)
    r = jax.nn.sigmoid((_block_diag(x, w_a) + b_a).astype(jnp.float32))
    i = jax.nn.sigmoid((_block_diag(x, w_x) + b_x).astype(jnp.float32))
    log_a = -LRU_C * r * jax.nn.softplus(-lam.astype(jnp.float32))
    a = jnp.exp(log_a)
    u = jnp.sqrt(-jnp.expm1(2.0 * log_a)) * (i * xf)
    u = u.at[:, 0].add(a[:, 0] * h0.astype(jnp.float32))

    def combine(lhs, rhs):
        a1, b1 = lhs
        a2, b2 = rhs
        return a1 * a2, a2 * b1 + b2

    _, h = lax.associative_scan(combine, (a, u), axis=1)
    return h.astype(x.dtype), h[:, -1].astype(h0.dtype)


def _mla_attend(q_lat, q_rope, ckv, kr, q_pos):
    B, T, H, L = q_lat.shape
    S = ckv.shape[1]
    qb = ATTN_Q_BLOCK if T % ATTN_Q_BLOCK == 0 else T
    nb = T // qb
    k_pos = jnp.arange(S, dtype=jnp.int32)
    scale = (MLA_NOPE + MLA_ROPE) ** -0.5

    def blocks(a):
        return a.reshape(B, nb, qb, *a.shape[2:]).swapaxes(0, 1)

    def one_block(args):
        ql, qr, qp = args
        s = (jnp.einsum('bqhl,bsl->bhqs', ql, ckv) + jnp.einsum('bqhr,bsr->bhqs', qr, kr)).astype(jnp.float32) * scale
        s = jnp.where((k_pos[None, :] <= qp[:, None])[None, None], s, -jnp.inf)
        p = jax.nn.softmax(s, axis=-1).astype(ckv.dtype)
        return jnp.einsum('bhqs,bsl->bqhl', p, ckv)

    o = lax.map(one_block, (blocks(q_lat), blocks(q_rope), q_pos.reshape(nb, qb)))
    return o.swapaxes(0, 1).reshape(B, T, H, L)


def _layer(x, ada, pos, s_gla, h_lru, conv_buf, past_ckv, past_kr, lw):
    (w_in, gla_w_gate, gla_b_gate, gla_norm_g, conv_w, conv_b, lru_w_a, lru_b_a, lru_w_x, lru_b_x, lru_lambda,
     mla_q_norm_g, mla_w_uq, mla_kv_norm_g, mla_w_uk, mla_w_uv, w_out, ln1_g, ln1_b,
     ffn_w_gu, ffn_w_down, ln2_g, ln2_b) = lw
    B, T, _ = x.shape
    shift1, scale1, gate1, shift2, scale2, gate2 = [a[:, None, :] for a in jnp.split(ada, 6, axis=-1)]

    h = x * (1 + scale1) + shift1
    split_at = [int(v) for v in np.cumsum(IN_SIZES)[:-1]]
    gq, gk, gv, gg, glr, lx, lgate, dq, dkv, kr_raw = jnp.split(h @ w_in, split_at, axis=-1)

    q = gq.reshape(B, T, GLA_HEADS, GLA_DK) * (GLA_DK ** -0.5)
    k = gk.reshape(B, T, GLA_HEADS, GLA_DK)
    v = gv.reshape(B, T, GLA_HEADS, GLA_DV)
    logf = jax.nn.log_sigmoid((glr @ gla_w_gate + gla_b_gate).astype(jnp.float32)) / GLA_GATE_TAU
    o, s_gla_new = _gla(q, k, v, logf.reshape(B, T, GLA_HEADS, GLA_DK), s_gla)
    o_gla = (_rms_norm(o, gla_norm_g) * jax.nn.silu(gg.reshape(B, T, GLA_HEADS, GLA_DV))).reshape(B, T, GLA_WIDTH)

    xc, conv_new = _causal_conv(lx, conv_buf, conv_w, conv_b)
    hseq, h_new = _rg_lru(xc, h_lru, lru_w_a, lru_b_a, lru_w_x, lru_b_x, lru_lambda)
    o_lru = jax.nn.gelu(lgate) * hseq

    q_all = jnp.einsum('btr,rhd->bthd', _rms_norm(dq, mla_q_norm_g), mla_w_uq)
    q_nope = q_all[..., :MLA_NOPE]
    q_rope = _rope(q_all[..., MLA_NOPE:], pos)
    ckv_new = _rms_norm(dkv, mla_kv_norm_g)
    kr_new = _rope(kr_raw, pos)
    if past_ckv is None:
        keys_c, keys_r = ckv_new, kr_new
    else:
        keys_c = jnp.concatenate([past_ckv.astype(x.dtype), ckv_new], axis=1)
        keys_r = jnp.concatenate([past_kr.astype(x.dtype), kr_new], axis=1)
    q_lat = jnp.einsum('bthn,lhn->bthl', q_nope, mla_w_uk)
    o_lat = _mla_attend(q_lat, q_rope, keys_c, keys_r, pos)
    o_mla = jnp.einsum('bthl,lhv->bthv', o_lat, mla_w_uv).reshape(B, T, MLA_WIDTH)

    mix = jnp.concatenate([o_gla, o_lru, o_mla], axis=-1) @ w_out
    x = _layer_norm(DEEPNORM_ALPHA * x + gate1 * mix, ln1_g, ln1_b)

    h2 = x * (1 + scale2) + shift2
    g_ff, u_ff = jnp.split(h2 @ ffn_w_gu, 2, axis=-1)
    f = (jax.nn.silu(g_ff) * u_ff) @ ffn_w_down
    x = _layer_norm(DEEPNORM_ALPHA * x + gate2 * f, ln2_g, ln2_b)
    return x, s_gla_new, h_new, conv_new, ckv_new, kr_new


def setup_inputs(seed: int = 0) -> dict:
    key = jax.random.key(seed)
    ks = iter(jax.random.split(key, 64))
    f32 = jnp.float32

    def nrm(shape, scale=1.0):
        return scale * jax.random.normal(next(ks), shape, f32)

    def gain(shape):
        return 1.0 + nrm(shape, 0.02)

    n_pages = PAST_LEN // PAGE_SIZE
    n_used = DEC_BATCH * n_pages
    n_phys = n_used + max(1, n_used // 4)
    inp = {}
    inp['x_prompt'] = nrm((BATCH, SEQ, D_MODEL))
    inp['x_sample'] = nrm((DEC_BATCH, DEC_SEQ, D_MODEL))
    inp['c_prompt'] = nrm((BATCH, D_MODEL))
    inp['c_sample'] = nrm((DEC_BATCH, D_MODEL))
    inp['state_gla'] = nrm((DEPTH, DEC_BATCH, GLA_HEADS, GLA_DK, GLA_DV), 0.5)
    inp['state_lru'] = nrm((DEPTH, DEC_BATCH, LRU_WIDTH), 0.5)
    inp['state_conv'] = nrm((DEPTH, DEC_BATCH, CONV_WIDTH - 1, LRU_WIDTH))
    inp['cache_ckv'] = nrm((DEPTH, n_phys, PAGE_SIZE, MLA_KV_LORA))
    inp['cache_krope'] = nrm((DEPTH, n_phys, PAGE_SIZE, MLA_ROPE))
    inp['page_table'] = jax.random.permutation(next(ks), n_phys)[:n_used].reshape(DEC_BATCH, n_pages).astype(jnp.int32)
    inp['ln_in_g'] = gain((D_MODEL,))
    inp['ln_in_b'] = nrm((D_MODEL,), 0.02)
    inp['w_ada'] = nrm((DEPTH, D_MODEL, 6 * D_MODEL), 0.5 * D_MODEL ** -0.5)
    inp['b_ada'] = nrm((DEPTH, 6 * D_MODEL), 0.02)
    inp['w_in'] = nrm((DEPTH, D_MODEL, IN_COLS), D_MODEL ** -0.5)
    inp['gla_w_gate'] = nrm((DEPTH, GLA_LOWRANK, GLA_HEADS * GLA_DK), GLA_LOWRANK ** -0.5)
    inp['gla_b_gate'] = nrm((DEPTH, GLA_HEADS * GLA_DK), 0.1)
    inp['gla_norm_g'] = gain((DEPTH, GLA_DV))
    inp['lru_conv_w'] = nrm((DEPTH, CONV_WIDTH, LRU_WIDTH), CONV_WIDTH ** -0.5)
    inp['lru_conv_b'] = nrm((DEPTH, LRU_WIDTH), 0.02)
    inp['lru_w_a'] = nrm((DEPTH, LRU_BLOCKS, LRU_BLOCK_W, LRU_BLOCK_W), LRU_BLOCK_W ** -0.5)
    inp['lru_b_a'] = nrm((DEPTH, LRU_WIDTH), 0.02)
    inp['lru_w_x'] = nrm((DEPTH, LRU_BLOCKS, LRU_BLOCK_W, LRU_BLOCK_W), LRU_BLOCK_W ** -0.5)
    inp['lru_b_x'] = nrm((DEPTH, LRU_WIDTH), 0.02)
    a_c = jax.random.uniform(next(ks), (DEPTH, LRU_WIDTH), f32, 0.9, 0.999)
    sig = a_c ** (1.0 / LRU_C)
    inp['lru_lambda'] = jnp.log(sig) - jnp.log1p(-sig)
    inp['mla_q_norm_g'] = gain((DEPTH, MLA_Q_LORA))
    inp['mla_w_uq'] = nrm((DEPTH, MLA_Q_LORA, MLA_HEADS, MLA_NOPE + MLA_ROPE), MLA_Q_LORA ** -0.5)
    inp['mla_kv_norm_g'] = gain((DEPTH, MLA_KV_LORA))
    inp['mla_w_uk'] = nrm((DEPTH, MLA_KV_LORA, MLA_HEADS, MLA_NOPE), MLA_KV_LORA ** -0.5)
    inp['mla_w_uv'] = nrm((DEPTH, MLA_KV_LORA, MLA_HEADS, MLA_V), MLA_KV_LORA ** -0.5)
    inp['w_out'] = nrm((DEPTH, MIX_WIDTH, D_MODEL), DEEPNORM_BETA * MIX_WIDTH ** -0.5)
    inp['ln1_g'] = gain((DEPTH, D_MODEL))
    inp['ln1_b'] = nrm((DEPTH, D_MODEL), 0.02)
    inp['ffn_w_gu'] = nrm((DEPTH, D_MODEL, 2 * D_FF), D_MODEL ** -0.5)
    inp['ffn_w_down'] = nrm((DEPTH, D_FF, D_MODEL), DEEPNORM_BETA * D_FF ** -0.5)
    inp['ln2_g'] = gain((DEPTH, D_MODEL))
    inp['ln2_b'] = nrm((DEPTH, D_MODEL), 0.02)
    return inp


def reference(x_prompt, x_sample, c_prompt, c_sample, state_gla, state_lru, state_conv, cache_ckv, cache_krope,
              page_table, ln_in_g, ln_in_b, w_ada, b_ada, w_in, gla_w_gate, gla_b_gate, gla_norm_g,
              lru_conv_w, lru_conv_b, lru_w_a, lru_b_a, lru_w_x, lru_b_x, lru_lambda,
              mla_q_norm_g, mla_w_uq, mla_kv_norm_g, mla_w_uk, mla_w_uv, w_out, ln1_g, ln1_b,
              ffn_w_gu, ffn_w_down, ln2_g, ln2_b):
    bp, tp, _ = x_prompt.shape
    bs, ts, _ = x_sample.shape
    n_pages = PAST_LEN // PAGE_SIZE
    dt = x_prompt.dtype
    pos_p = jnp.arange(tp, dtype=jnp.int32)
    pos_s = PAST_LEN + jnp.arange(ts, dtype=jnp.int32)
    zero_gla = jnp.zeros((bp, GLA_HEADS, GLA_DK, GLA_DV), dt)
    zero_h = jnp.zeros((bp, LRU_WIDTH), dt)
    zero_conv = jnp.zeros((bp, CONV_WIDTH - 1, LRU_WIDTH), dt)
    xp = _layer_norm(x_prompt, ln_in_g, ln_in_b)
    xs = _layer_norm(x_sample, ln_in_g, ln_in_b)
    cp = jax.nn.silu(c_prompt)
    cs = jax.nn.silu(c_sample)
    st_p, st_s = [], []
    for l in range(DEPTH):
        lw = (w_in[l], gla_w_gate[l], gla_b_gate[l], gla_norm_g[l], lru_conv_w[l], lru_conv_b[l],
              lru_w_a[l], lru_b_a[l], lru_w_x[l], lru_b_x[l], lru_lambda[l],
              mla_q_norm_g[l], mla_w_uq[l], mla_kv_norm_g[l], mla_w_uk[l], mla_w_uv[l], w_out[l],
              ln1_g[l], ln1_b[l], ffn_w_gu[l], ffn_w_down[l], ln2_g[l], ln2_b[l])
        ada_p = cp @ w_ada[l] + b_ada[l]
        ada_s = cs @ w_ada[l] + b_ada[l]
        xp, *sp = _layer(xp, ada_p, pos_p, zero_gla, zero_h, zero_conv, None, None, lw)
        past_ckv = cache_ckv[l][page_table].reshape(bs, n_pages * PAGE_SIZE, MLA_KV_LORA)
        past_kr = cache_krope[l][page_table].reshape(bs, n_pages * PAGE_SIZE, MLA_ROPE)
        xs, *ss = _layer(xs, ada_s, pos_s, state_gla[l], state_lru[l], state_conv[l], past_ckv, past_kr, lw)
        st_p.append(sp)
        st_s.append(ss)

    def stk(outs, j):
        return jnp.stack([o[j] for o in outs])

    return (xp, xs, stk(st_p, 0), stk(st_s, 0), stk(st_p, 1), stk(st_s, 1), stk(st_p, 2), stk(st_s, 2),
            stk(st_p, 3), stk(st_s, 3), stk(st_p, 4), stk(st_s, 4))
```

```python
import functools
import math

import numpy as np
import jax
import jax.numpy as jnp
from jax import lax
from jax.experimental import pallas as pl
from jax.experimental.pallas import tpu as pltpu

F32 = jnp.float32
BF16 = jnp.bfloat16
HIGHEST = lax.Precision.HIGHEST

D_MODEL = 1024
PAGE_SIZE = 128
GLA_HEADS = 4
GLA_DK = 32
GLA_DV = 64
GLA_WIDTH = GLA_HEADS * GLA_DV
GLA_LOWRANK = 16
GLA_GATE_TAU = 16.0
GLA_CHUNK = 32
LRU_WIDTH = 256
LRU_BLOCKS = 4
LRU_BLOCK_W = LRU_WIDTH // LRU_BLOCKS
CONV_WIDTH = 4
LRU_C = 8.0
MLA_HEADS = 8
MLA_NOPE = 64
MLA_ROPE = 32
MLA_V = 64
MLA_Q_LORA = 256
MLA_KV_LORA = 128
ROPE_THETA = 10000.0
D_FF = 2816
LN_EPS = 1e-5
RMS_EPS = 1e-6
ATTN_SCALE = (MLA_NOPE + MLA_ROPE) ** -0.5

LANES = 128
SUBLANES = 8
VMEM_LIMIT_BYTES = 56 * 1024 * 1024

C_Q, C_K, C_V, C_GG, C_LX, C_LG = 0, 128, 256, 512, 768, 1024
C_DQ, C_DKV, C_TAIL = 1280, 1536, 1664
W_IN_COLS = 1792
C_LOGF = 1280
PROJ_COLS = 1408
QK_WIDTH = 256


def _cparams(sem):
    return pltpu.CompilerParams(dimension_semantics=sem, vmem_limit_bytes=VMEM_LIMIT_BYTES)


def _layer_norm_rows(y, g, b):
    mu = jnp.mean(y, axis=-1, keepdims=True)
    yc = y - mu
    var = jnp.mean(yc * yc, axis=-1, keepdims=True)
    return yc * lax.rsqrt(var + LN_EPS) * g + b


def _rms_rows(y, g):
    return y * lax.rsqrt(jnp.mean(y * y, axis=-1, keepdims=True) + RMS_EPS) * g


def _ada_kernel(c_ref, w_ref, b_ref, o_ref):
    c = c_ref[...]
    s = (c * jax.nn.sigmoid(c)).astype(BF16)
    o_ref[0] = jnp.dot(s, w_ref[0].astype(BF16), preferred_element_type=F32) + b_ref[0]


def _ada_all(c_all, w_ada, b_ada):
    depth, d, n = w_ada.shape
    bt = c_all.shape[0]
    tn = 1536
    return pl.pallas_call(
        _ada_kernel,
        grid=(depth, n // tn),
        in_specs=[pl.BlockSpec((bt, d), lambda l, j: (0, 0)),
                  pl.BlockSpec((1, d, tn), lambda l, j: (l, 0, j)),
                  pl.BlockSpec((1, 1, tn), lambda l, j: (l, 0, j))],
        out_specs=pl.BlockSpec((1, bt, tn), lambda l, j: (l, 0, j)),
        out_shape=jax.ShapeDtypeStruct((depth, bt, n), F32),
        compiler_params=_cparams(("parallel", "parallel")),
        name="ada_mod",
    )(c_all, w_ada, b_ada.reshape(depth, 1, n))


def _ln_kernel(x_ref, g_ref, b_ref, o_ref):
    o_ref[...] = _layer_norm_rows(x_ref[...], g_ref[...], b_ref[...])


def _ln_in(x2d, g, b):
    n, d = x2d.shape
    tm = min(1024, n)
    return pl.pallas_call(
        _ln_kernel,
        grid=(n // tm,),
        in_specs=[pl.BlockSpec((tm, d), lambda i: (i, 0)),
                  pl.BlockSpec((1, d), lambda i: (0, 0)),
                  pl.BlockSpec((1, d), lambda i: (0, 0))],
        out_specs=pl.BlockSpec((tm, d), lambda i: (i, 0)),
        out_shape=jax.ShapeDtypeStruct((n, d), F32),
        compiler_params=_cparams(("parallel",)),
        name="ln_in",
    )(x2d, g.reshape(1, d), b.reshape(1, d))


def _fold_kernel(uq_ref, uk_ref, uv_ref, wo_ref, qlat_ref, fold_ref):
    a = uq_ref[0, 0]
    b = uk_ref[0, 0]
    ql = lax.dot_general(a, b, (((1,), (1,)), ((), ())), precision=HIGHEST,
                         preferred_element_type=F32)
    qlat_ref[0] = (ql * ATTN_SCALE).astype(BF16)
    fd = jnp.dot(uv_ref[0, 0], wo_ref[0], precision=HIGHEST, preferred_element_type=F32)
    fold_ref[0] = fd.astype(BF16)


def _fold_weights(mla_w_uq, mla_w_uk, mla_w_uv, w_out):
    depth = mla_w_uq.shape[0]
    uq_n = jnp.transpose(mla_w_uq[..., :MLA_NOPE], (0, 2, 1, 3))
    uk_t = jnp.transpose(mla_w_uk, (0, 2, 1, 3))
    uv_t = jnp.transpose(mla_w_uv, (0, 2, 1, 3))
    mla_row0 = (GLA_WIDTH + LRU_WIDTH) // MLA_V
    return pl.pallas_call(
        _fold_kernel,
        grid=(depth, MLA_HEADS),
        in_specs=[pl.BlockSpec((1, 1, MLA_Q_LORA, MLA_NOPE), lambda l, h: (l, h, 0, 0)),
                  pl.BlockSpec((1, 1, MLA_KV_LORA, MLA_NOPE), lambda l, h: (l, h, 0, 0)),
                  pl.BlockSpec((1, 1, MLA_KV_LORA, MLA_V), lambda l, h: (l, h, 0, 0)),
                  pl.BlockSpec((1, MLA_V, D_MODEL), lambda l, h: (l, mla_row0 + h, 0))],
        out_specs=[pl.BlockSpec((1, MLA_Q_LORA, MLA_KV_LORA), lambda l, h: (l, 0, h)),
                   pl.BlockSpec((1, MLA_KV_LORA, D_MODEL), lambda l, h: (l, h, 0))],
        out_shape=[jax.ShapeDtypeStruct((depth, MLA_Q_LORA, MLA_HEADS * MLA_KV_LORA), BF16),
                   jax.ShapeDtypeStruct((depth, MLA_HEADS * MLA_KV_LORA, D_MODEL), BF16)],
        compiler_params=_cparams(("parallel", "parallel")),
        name="fold_weights",
    )(uq_n, uk_t, uv_t, w_out)


def _inproj_kernel(x_ref, ada_ref, w_ref, wg_ref, bg_ref, qn_ref, kvn_ref, wq2_ref, cs_ref,
                   proj_ref, qcat_ref, ckv_ref, kr_ref, kcat_ref):
    bb, tt, d = x_ref.shape
    m = bb * tt
    x = x_ref[...]
    shift = ada_ref[:, 0:1, :]
    scale = ada_ref[:, 1:2, :]
    h = (x * (1.0 + scale) + shift).reshape(m, d).astype(BF16)
    p = jnp.dot(h, w_ref[...], preferred_element_type=F32)
    proj_ref[:, 0:C_DQ] = p[:, 0:C_DQ]

    tail = p[:, C_TAIL:C_TAIL + LANES]
    z = jnp.dot(tail.astype(BF16), wg_ref[...], preferred_element_type=F32) + bg_ref[...]
    proj_ref[:, C_LOGF:C_LOGF + LANES] = jax.nn.log_sigmoid(z) / GLA_GATE_TAU

    cs = cs_ref[...]
    lane = lax.broadcasted_iota(jnp.int32, (m, LANES), 1)
    rope_lanes = lane < MLA_ROPE

    def rope(block):
        r = block * cs
        r = r + pltpu.roll(r, LANES - MLA_ROPE, 1)
        return jnp.where(rope_lanes, r, 0.0)

    kr = rope(tail)
    kr_ref[...] = kr[:, 0:MLA_ROPE]
    ckv = _rms_rows(p[:, C_DKV:C_DKV + MLA_KV_LORA], kvn_ref[...])
    ckv_ref[...] = ckv
    kcat_ref[...] = jnp.concatenate([ckv.astype(BF16), kr.astype(BF16)], axis=1)

    dqn = _rms_rows(p[:, C_DQ:C_DQ + MLA_Q_LORA], qn_ref[...]).astype(BF16)
    q2 = jnp.dot(dqn, wq2_ref[...], preferred_element_type=F32)
    for hh in range(MLA_HEADS):
        lat = q2[:, hh * LANES:(hh + 1) * LANES]
        rp = rope(q2[:, (MLA_HEADS + hh) * LANES:(MLA_HEADS + hh + 1) * LANES])
        qc = jnp.concatenate([lat, rp], axis=1)
        qcat_ref[:, hh] = qc.reshape(bb, tt, QK_WIDTH).astype(qcat_ref.dtype)


def _inproj(x, ada, w_in_p, wg_p, bg, qn, kvn, wq2, cs, bb, tt):
    b, t, d = x.shape
    n = b * t
    m = bb * tt
    grid = (b // bb, t // tt)
    nt = t // tt
    const = lambda i, j: (0, 0)
    row = lambda i, j: (i * nt + j, 0)
    return pl.pallas_call(
        _inproj_kernel,
        grid=grid,
        in_specs=[pl.BlockSpec((bb, tt, d), lambda i, j: (i, j, 0)),
                  pl.BlockSpec((bb, 6, d), lambda i, j: (i, 0, 0)),
                  pl.BlockSpec((d, W_IN_COLS), const),
                  pl.BlockSpec((LANES, LANES), const),
                  pl.BlockSpec((1, LANES), const),
                  pl.BlockSpec((1, MLA_Q_LORA), const),
                  pl.BlockSpec((1, MLA_KV_LORA), const),
                  pl.BlockSpec((MLA_Q_LORA, 2 * MLA_HEADS * LANES), const),
                  pl.BlockSpec((m, LANES), lambda i, j: (j, 0))],
        out_specs=[pl.BlockSpec((m, PROJ_COLS), row),
                   pl.BlockSpec((bb, MLA_HEADS, tt, QK_WIDTH), lambda i, j: (i, 0, j, 0)),
                   pl.BlockSpec((m, MLA_KV_LORA), row),
                   pl.BlockSpec((m, MLA_ROPE), row),
                   pl.BlockSpec((m, QK_WIDTH), row)],
        out_shape=[jax.ShapeDtypeStruct((n, PROJ_COLS), F32),
                   jax.ShapeDtypeStruct((b, MLA_HEADS, t, QK_WIDTH), BF16 if tt % 16 == 0 else F32),
                   jax.ShapeDtypeStruct((n, MLA_KV_LORA), F32),
                   jax.ShapeDtypeStruct((n, MLA_ROPE), F32),
                   jax.ShapeDtypeStruct((n, QK_WIDTH), BF16)],
        compiler_params=_cparams(("parallel", "parallel")),
        name="in_proj",
    )(x, ada, w_in_p, wg_p, bg, qn, kvn, wq2, cs)


def _gla_kernel(q_ref, k_ref, v_ref, gg_ref, g_ref, s0_ref, ng_ref, ltri_ref, ind_ref,
                msk_ref, seg_ref, o_ref, sT_ref, st_sc, b_sc, o_sc, *, nb, t, c):
    n_chunks = t // c
    groups = c // SUBLANES
    ltri = ltri_ref[...]
    ind = ind_ref[...]
    msk = msk_ref[...]
    row_iota = lax.broadcasted_iota(jnp.int32, (c, LANES), 0)
    for j in range(nb):
        st_sc[j] = s0_ref[j]

    def chunk(ci, carry):
        for j in range(nb):
            r0 = pl.multiple_of(j * t + ci * c, c)
            g = g_ref[pl.ds(r0, c), :]
            b = jnp.dot(ltri, g, precision=HIGHEST, preferred_element_type=F32)
            b_sc[j] = b
            q = q_ref[pl.ds(r0, c), :]
            k = k_ref[pl.ds(r0, c), :]
            v = v_ref[pl.ds(r0, c), :]
            blast = b_sc[j, pl.ds(c - 1, 1), :]
            qe = q * jnp.exp(b)
            ke = k * jnp.exp(blast - b)
            st = st_sc[j]
            o_inter = lax.dot_general(qe.astype(BF16), st.astype(BF16), (((1,), (1,)), ((), ())),
                                      preferred_element_type=F32)
            ut = lax.dot_general(v.astype(BF16), ke.astype(BF16), (((0,), (0,)), ((), ())),
                                 preferred_element_type=F32)
            st_sc[j] = st * jnp.exp(blast) + ut * msk

            o_blk = [None] * groups
            for g0 in range(groups):
                lo = g0 * SUBLANES
                pieces = []
                for s in range(lo, lo + SUBLANES):
                    bs = b_sc[j, pl.ds(s, 1), :]
                    ks = k_ref[pl.ds(r0 + s, 1), :]
                    e = jnp.exp(b[lo:, :] - bs)
                    e = jnp.where(row_iota[lo:, :] >= s, e, 0.0)
                    pieces.append(e * q[lo:, :] * ks)
                w = jnp.concatenate(pieces, axis=0).astype(BF16)
                a = jnp.dot(w, ind, preferred_element_type=F32)
                rows = c - lo
                for idx in range(SUBLANES):
                    vs = v_ref[pl.ds(r0 + lo + idx, 1), :]
                    for rb in range(g0, groups):
                        piece = a[idx * rows + (rb - g0) * SUBLANES:
                                  idx * rows + (rb - g0 + 1) * SUBLANES, :] * vs
                        o_blk[rb] = piece if o_blk[rb] is None else o_blk[rb] + piece
            o_intra = jnp.concatenate(o_blk, axis=0) if groups > 1 else o_blk[0]
            o_sc[pl.ds(r0, c), :] = o_inter + o_intra
        return carry

    lax.fori_loop(0, n_chunks, chunk, 0)

    for j in range(nb):
        sT_ref[j] = st_sc[j]

    rt = min(256, nb * t)
    ng = ng_ref[...]
    seg = seg_ref[...]

    def epi(i, carry):
        r0 = pl.multiple_of(i * rt, rt)
        o = o_sc[pl.ds(r0, rt), :]
        ms = jnp.dot(o * o, seg, precision=HIGHEST, preferred_element_type=F32)
        gg = gg_ref[pl.ds(r0, rt), :]
        o_ref[pl.ds(r0, rt), :] = o * lax.rsqrt(ms + RMS_EPS) * ng * (gg * jax.nn.sigmoid(gg))
        return carry

    lax.fori_loop(0, (nb * t) // rt, epi, 0)


def _gla(proj, s0T, ng, b, t, nb):
    n = b * t
    c = min(GLA_CHUNK, t)
    rows = nb * t
    ltri = jnp.asarray(np.tril(np.ones((c, c), np.float32)))
    hk = np.arange(GLA_HEADS * GLA_DK) // GLA_DK
    hv = np.arange(GLA_WIDTH) // GLA_DV
    ind = jnp.asarray((hk[:, None] == hv[None, :]).astype(np.float32)).astype(BF16)
    msk = jnp.asarray((hv[:, None] == hk[None, :]).astype(np.float32))
    seg = jnp.asarray((hv[:, None] == hv[None, :]).astype(np.float32) / GLA_DV)
    const = lambda i: (0, 0)
    kern = functools.partial(_gla_kernel, nb=nb, t=t, c=c)
    return pl.pallas_call(
        kern,
        grid=(b // nb,),
        in_specs=[pl.BlockSpec((rows, LANES), lambda i: (i, C_Q // LANES)),
                  pl.BlockSpec((rows, LANES), lambda i: (i, C_K // LANES)),
                  pl.BlockSpec((rows, GLA_WIDTH), lambda i: (i, C_V // GLA_WIDTH)),
                  pl.BlockSpec((rows, GLA_WIDTH), lambda i: (i, C_GG // GLA_WIDTH)),
                  pl.BlockSpec((rows, LANES), lambda i: (i, C_LOGF // LANES)),
                  pl.BlockSpec((nb, GLA_WIDTH, LANES), lambda i: (i, 0, 0)),
                  pl.BlockSpec((1, GLA_WIDTH), const),
                  pl.BlockSpec((c, c), const),
                  pl.BlockSpec((LANES, GLA_WIDTH), const),
                  pl.BlockSpec((GLA_WIDTH, LANES), const),
                  pl.BlockSpec((GLA_WIDTH, GLA_WIDTH), const)],
        out_specs=[pl.BlockSpec((rows, GLA_WIDTH), lambda i: (i, 0)),
                   pl.BlockSpec((nb, GLA_WIDTH, LANES), lambda i: (i, 0, 0))],
        out_shape=[jax.ShapeDtypeStruct((n, GLA_WIDTH), F32),
                   jax.ShapeDtypeStruct((b, GLA_WIDTH, LANES), F32)],
        scratch_shapes=[pltpu.VMEM((nb, GLA_WIDTH, LANES), F32),
                        pltpu.VMEM((nb, c, LANES), F32),
                        pltpu.VMEM((rows, GLA_WIDTH), F32)],
        compiler_params=_cparams(("parallel",)),
        name="gla",
    )(proj, proj, proj, proj, proj, s0T, ng, ltri, ind, msk, seg)


def _lru_kernel(lx_ref, lg_ref, cbuf_ref, h0_ref, cw_ref, cb_ref, wax_ref, bax_ref, sp_ref,
                o_ref, hn_ref, cn_ref, xp_sc, a_sc, u_sc, *, nb, t):
    pad = SUBLANES
    hist = CONV_WIDTH - 1
    row8 = lax.broadcasted_iota(jnp.int32, (SUBLANES, LRU_WIDTH), 0)
    rows_per_iter = min(32, t)
    sub = rows_per_iter // SUBLANES

    def scan_block(a, u):
        for dd in (1, 2, 4):
            a_s = jnp.where(row8 >= dd, pltpu.roll(a, dd, 0), 1.0)
            u_s = jnp.where(row8 >= dd, pltpu.roll(u, dd, 0), 0.0)
            u = a * u_s + u
            a = a * a_s
        return a, u

    for j in range(nb):
        base = j * t
        xp_sc[pl.ds(0, pad), :] = jnp.zeros((pad, LRU_WIDTH), F32)
        xp_sc[pl.ds(pad - hist, hist), :] = cbuf_ref[j]
        xp_sc[pl.ds(pad, t), :] = lx_ref[pl.ds(base, t), :]
        xc = cb_ref[...]
        for kk in range(CONV_WIDTH):
            xc = xc + xp_sc[pl.ds(pad - hist + kk, t), :] * cw_ref[pl.ds(kk, 1), :]
        cn_ref[j] = xp_sc[pl.ds(pad + t - hist, hist), :]

        ax = jnp.dot(xc.astype(BF16), wax_ref[...], preferred_element_type=F32) + bax_ref[...]
        r = jax.nn.sigmoid(ax[:, 0:LRU_WIDTH])
        ig = jax.nn.sigmoid(ax[:, LRU_WIDTH:2 * LRU_WIDTH])
        log_a = -LRU_C * r * sp_ref[...]
        a = jnp.exp(log_a)
        u = jnp.sqrt((a * a + 1.0) * jnp.tanh(-log_a)) * (ig * xc)
        a_sc[...] = a
        u_sc[...] = u
        u_sc[pl.ds(0, 1), :] = u[0:1, :] + a[0:1, :] * h0_ref[j]

        def step(i, hprev):
            r0 = pl.multiple_of(i * rows_per_iter, rows_per_iter)
            scans = []
            for sb in range(sub):
                ab = a_sc[pl.ds(r0 + sb * SUBLANES, SUBLANES), :]
                ub = u_sc[pl.ds(r0 + sb * SUBLANES, SUBLANES), :]
                scans.append(scan_block(ab, ub))
            for sb in range(sub):
                ac, uc = scans[sb]
                hb = ac * hprev + uc
                u_sc[pl.ds(r0 + sb * SUBLANES, SUBLANES), :] = hb
                hprev = hb[SUBLANES - 1:SUBLANES, :]
            return hprev

        hlast = lax.fori_loop(0, t // rows_per_iter, step, jnp.zeros((1, LRU_WIDTH), F32))
        hn_ref[j] = hlast
        lg = lg_ref[pl.ds(base, t), :]
        o_ref[pl.ds(base, t), :] = jax.nn.gelu(lg, approximate=True) * u_sc[...]


def _lru(proj, cbuf, h0, cw, cb, wax, bax, sp, b, t, nb):
    n = b * t
    rows = nb * t
    const = lambda i: (0, 0)
    kern = functools.partial(_lru_kernel, nb=nb, t=t)
    return pl.pallas_call(
        kern,
        grid=(b // nb,),
        in_specs=[pl.BlockSpec((rows, LRU_WIDTH), lambda i: (i, C_LX // LRU_WIDTH)),
                  pl.BlockSpec((rows, LRU_WIDTH), lambda i: (i, C_LG // LRU_WIDTH)),
                  pl.BlockSpec((nb, CONV_WIDTH - 1, LRU_WIDTH), lambda i: (i, 0, 0)),
                  pl.BlockSpec((nb, 1, LRU_WIDTH), lambda i: (i, 0, 0)),
                  pl.BlockSpec((CONV_WIDTH, LRU_WIDTH), const),
                  pl.BlockSpec((1, LRU_WIDTH), const),
                  pl.BlockSpec((LRU_WIDTH, 2 * LRU_WIDTH), const),
                  pl.BlockSpec((1, 2 * LRU_WIDTH), const),
                  pl.BlockSpec((1, LRU_WIDTH), const)],
        out_specs=[pl.BlockSpec((rows, LRU_WIDTH), lambda i: (i, 0)),
                   pl.BlockSpec((nb, 1, LRU_WIDTH), lambda i: (i, 0, 0)),
                   pl.BlockSpec((nb, CONV_WIDTH - 1, LRU_WIDTH), lambda i: (i, 0, 0))],
        out_shape=[jax.ShapeDtypeStruct((n, LRU_WIDTH), F32),
                   jax.ShapeDtypeStruct((b, 1, LRU_WIDTH), F32),
                   jax.ShapeDtypeStruct((b, CONV_WIDTH - 1, LRU_WIDTH), F32)],
        scratch_shapes=[pltpu.VMEM((t + SUBLANES, LRU_WIDTH), F32),
                        pltpu.VMEM((t, LRU_WIDTH), F32),
                        pltpu.VMEM((t, LRU_WIDTH), F32)],
        compiler_params=_cparams(("parallel",)),
        name="rg_lru",
    )(proj, proj, cbuf, h0, cw, cb, wax, bax, sp)


def _attn_kernel(q_ref, k_ref, o_ref, m_sc, l_sc, acc_sc, *, tq):
    i = pl.program_id(1)
    r = MLA_HEADS * tq
    q = q_ref[0].reshape(r, QK_WIDTH)
    m_sc[...] = jnp.full((r, 1), -jnp.inf, F32)
    l_sc[...] = jnp.zeros((r, 1), F32)
    acc_sc[...] = jnp.zeros((r, MLA_KV_LORA), F32)

    def update(j, masked):
        kb = k_ref[0, pl.ds(pl.multiple_of(j * tq, tq), tq), :]
        s = lax.dot_general(q, kb, (((1,), (1,)), ((), ())), preferred_element_type=F32)
        if masked:
            rows = lax.broadcasted_iota(jnp.int32, (r, tq), 0) & (tq - 1)
            cols = lax.broadcasted_iota(jnp.int32, (r, tq), 1)
            s = jnp.where(cols <= rows, s, -jnp.inf)
        m_old = m_sc[...]
        m_new = jnp.maximum(m_old, jnp.max(s, axis=-1, keepdims=True))
        alpha = jnp.exp(m_old - m_new)
        p = jnp.exp(s - m_new)
        l_sc[...] = alpha * l_sc[...] + jnp.sum(p, axis=-1, keepdims=True)
        acc_sc[...] = alpha * acc_sc[...] + jnp.dot(p.astype(BF16), kb[:, 0:MLA_KV_LORA],
                                                    preferred_element_type=F32)
        m_sc[...] = m_new

    def body(j, carry):
        update(j, False)
        return carry

    lax.fori_loop(0, i, body, 0)
    update(i, True)
    out = acc_sc[...] / l_sc[...]
    for hh in range(MLA_HEADS):
        o_ref[0, :, hh * MLA_KV_LORA:(hh + 1) * MLA_KV_LORA] = \
            out[hh * tq:(hh + 1) * tq, :].astype(BF16)


def _attn_prompt(qcat, kcat, b, t):
    tq = min(256, t)
    r = MLA_HEADS * tq
    kern = functools.partial(_attn_kernel, tq=tq)
    return pl.pallas_call(
        kern,
        grid=(b, t // tq),
        in_specs=[pl.BlockSpec((1, MLA_HEADS, tq, QK_WIDTH), lambda bi, i: (bi, 0, i, 0)),
                  pl.BlockSpec((1, t, QK_WIDTH), lambda bi, i: (bi, 0, 0))],
        out_specs=pl.BlockSpec((1, tq, MLA_HEADS * MLA_KV_LORA), lambda bi, i: (bi, i, 0)),
        out_shape=jax.ShapeDtypeStruct((b, t, MLA_HEADS * MLA_KV_LORA), BF16),
        scratch_shapes=[pltpu.VMEM((r, 1), F32), pltpu.VMEM((r, 1), F32),
                        pltpu.VMEM((r, MLA_KV_LORA), F32)],
        compiler_params=_cparams(("parallel", "parallel")),
        name="attn_prompt",
    )(qcat, kcat.reshape(b, t, QK_WIDTH))


def _attn_paged_kernel(pt_ref, q_ref, cnew_ref, rnew_ref, *refs, pages, ts):
    ck_refs = refs[0:pages]
    kr_refs = refs[pages:2 * pages]
    o_ref, m_sc, l_sc, acc_sc = refs[2 * pages:]
    j = pl.program_id(1)
    r = MLA_HEADS * ts
    q = q_ref[0].reshape(r, QK_WIDTH).astype(BF16)
    q_lat = q[:, 0:MLA_KV_LORA]
    q_rope = q[:, MLA_KV_LORA:MLA_KV_LORA + MLA_ROPE]

    @pl.when(j == 0)
    def _():
        m_sc[...] = jnp.full((r, 1), -jnp.inf, F32)
        l_sc[...] = jnp.zeros((r, 1), F32)
        acc_sc[...] = jnp.zeros((r, MLA_KV_LORA), F32)

    def update(kc, kr, mask):
        s = (lax.dot_general(q_lat, kc, (((1,), (1,)), ((), ())), preferred_element_type=F32)
             + lax.dot_general(q_rope, kr, (((1,), (1,)), ((), ())), preferred_element_type=F32))
        if mask is not None:
            s = jnp.where(mask, s, -jnp.inf)
        m_old = m_sc[...]
        m_new = jnp.maximum(m_old, jnp.max(s, axis=-1, keepdims=True))
        alpha = jnp.exp(m_old - m_new)
        p = jnp.exp(s - m_new)
        l_sc[...] = alpha * l_sc[...] + jnp.sum(p, axis=-1, keepdims=True)
        acc_sc[...] = alpha * acc_sc[...] + jnp.dot(p.astype(BF16), kc, preferred_element_type=F32)
        m_sc[...] = m_new

    kc = jnp.concatenate([c[...].astype(BF16) for c in ck_refs], axis=0)
    kr = jnp.concatenate([c[...].astype(BF16) for c in kr_refs], axis=0)
    update(kc, kr, None)

    @pl.when(j == pl.num_programs(1) - 1)
    def _():
        rows = lax.broadcasted_iota(jnp.int32, (r, ts), 0) & (ts - 1)
        cols = lax.broadcasted_iota(jnp.int32, (r, ts), 1)
        update(cnew_ref[...].astype(BF16), rnew_ref[...].astype(BF16), cols <= rows)
        out = acc_sc[...] / l_sc[...]
        for hh in range(MLA_HEADS):
            o_ref[0, :, hh * MLA_KV_LORA:(hh + 1) * MLA_KV_LORA] = out[hh * ts:(hh + 1) * ts, :]


def _attn_paged(qcat, ckv_new, kr_new, cache_ckv, cache_krope, page_table, layer, pages):
    b, _, ts, _ = qcat.shape
    n_pages = page_table.shape[1]
    kern = functools.partial(_attn_paged_kernel, pages=pages, ts=ts)

    def page_spec(width, i):
        return pl.BlockSpec((None, None, PAGE_SIZE, width),
                            lambda bi, j, pt: (layer, pt[bi, j * pages + i], 0, 0))

    in_specs = [pl.BlockSpec((1, MLA_HEADS, ts, QK_WIDTH), lambda bi, j, pt: (bi, 0, 0, 0)),
                pl.BlockSpec((ts, MLA_KV_LORA), lambda bi, j, pt: (bi, 0)),
                pl.BlockSpec((ts, MLA_ROPE), lambda bi, j, pt: (bi, 0))]
    in_specs += [page_spec(MLA_KV_LORA, i) for i in range(pages)]
    in_specs += [page_spec(MLA_ROPE, i) for i in range(pages)]
    r = MLA_HEADS * ts
    grid_spec = pltpu.PrefetchScalarGridSpec(
        num_scalar_prefetch=1,
        grid=(b, n_pages // pages),
        in_specs=in_specs,
        out_specs=pl.BlockSpec((1, ts, MLA_HEADS * MLA_KV_LORA), lambda bi, j, pt: (bi, 0, 0)),
        scratch_shapes=[pltpu.VMEM((r, 1), F32), pltpu.VMEM((r, 1), F32),
                        pltpu.VMEM((r, MLA_KV_LORA), F32)])
    return pl.pallas_call(
        kern,
        grid_spec=grid_spec,
        out_shape=jax.ShapeDtypeStruct((b, ts, MLA_HEADS * MLA_KV_LORA), F32),
        compiler_params=_cparams(("parallel", "arbitrary")),
        name="attn_paged",
    )(page_table, qcat, ckv_new, kr_new, *([cache_ckv] * pages), *([cache_krope] * pages))


def _outproj_kernel(x_ref, ada_ref, og_ref, ol_ref, oa_ref, wgl_ref, wf_ref, g_ref, b_ref, o_ref,
                    *, alpha):
    bb, tt, d = x_ref.shape
    m = bb * tt
    gate = ada_ref[:, 2:3, :]
    ogl = jnp.concatenate([og_ref[...], ol_ref[...]], axis=1).astype(BF16)
    mix = (jnp.dot(ogl, wgl_ref[...], preferred_element_type=F32)
           + jnp.dot(oa_ref[...].astype(BF16), wf_ref[...], preferred_element_type=F32))
    y = alpha * x_ref[...] + gate * mix.reshape(bb, tt, d)
    o_ref[...] = _layer_norm_rows(y, g_ref[...], b_ref[...])


def _outproj(x, ada, o_gla, o_lru, o_lat, wgl, wfold, g, bias, bb, tt, alpha):
    b, t, d = x.shape
    nt = t // tt
    m = bb * tt
    const = lambda i, j: (0, 0)
    row = lambda i, j: (i * nt + j, 0)
    kern = functools.partial(_outproj_kernel, alpha=alpha)
    return pl.pallas_call(
        kern,
        grid=(b // bb, nt),
        in_specs=[pl.BlockSpec((bb, tt, d), lambda i, j: (i, j, 0)),
                  pl.BlockSpec((bb, 6, d), lambda i, j: (i, 0, 0)),
                  pl.BlockSpec((m, GLA_WIDTH), row),
                  pl.BlockSpec((m, LRU_WIDTH), row),
                  pl.BlockSpec((m, MLA_HEADS * MLA_KV_LORA), row),
                  pl.BlockSpec((GLA_WIDTH + LRU_WIDTH, d), const),
                  pl.BlockSpec((MLA_HEADS * MLA_KV_LORA, d), const),
                  pl.BlockSpec((1, d), const),
                  pl.BlockSpec((1, d), const)],
        out_specs=pl.BlockSpec((bb, tt, d), lambda i, j: (i, j, 0)),
        out_shape=jax.ShapeDtypeStruct((b, t, d), F32),
        compiler_params=_cparams(("parallel", "parallel")),
        name="out_proj",
    )(x, ada, o_gla, o_lru, o_lat, wgl, wfold, g, bias)


def _ffn_kernel(x_ref, ada_ref, wgu_ref, wd_ref, g_ref, b_ref, o_ref, *, alpha, cw):
    bb, tt, d = x_ref.shape
    m = bb * tt
    x = x_ref[...]
    shift = ada_ref[:, 3:4, :]
    scale = ada_ref[:, 4:5, :]
    gate = ada_ref[:, 5:6, :]
    h2 = (x * (1.0 + scale) + shift).reshape(m, d).astype(BF16)
    acc = jnp.zeros((m, d), F32)
    for ci in range(D_FF // cw):
        gf = jnp.dot(h2, wgu_ref[:, ci * cw:(ci + 1) * cw], preferred_element_type=F32)
        uf = jnp.dot(h2, wgu_ref[:, D_FF + ci * cw:D_FF + (ci + 1) * cw],
                     preferred_element_type=F32)
        act = (gf * jax.nn.sigmoid(gf) * uf).astype(BF16)
        acc = acc + jnp.dot(act, wd_ref[ci * cw:(ci + 1) * cw, :], preferred_element_type=F32)
    y = alpha * x + gate * acc.reshape(bb, tt, d)
    o_ref[...] = _layer_norm_rows(y, g_ref[...], b_ref[...])


def _ffn(x, ada, wgu, wd, g, bias, bb, tt, alpha):
    b, t, d = x.shape
    const = lambda i, j: (0, 0)
    kern = functools.partial(_ffn_kernel, alpha=alpha, cw=256)
    return pl.pallas_call(
        kern,
        grid=(b // bb, t // tt),
        in_specs=[pl.BlockSpec((bb, tt, d), lambda i, j: (i, j, 0)),
                  pl.BlockSpec((bb, 6, d), lambda i, j: (i, 0, 0)),
                  pl.BlockSpec((d, 2 * D_FF), const),
                  pl.BlockSpec((D_FF, d), const),
                  pl.BlockSpec((1, d), const),
                  pl.BlockSpec((1, d), const)],
        out_specs=pl.BlockSpec((bb, tt, d), lambda i, j: (i, j, 0)),
        out_shape=jax.ShapeDtypeStruct((b, t, d), F32),
        compiler_params=_cparams(("parallel", "parallel")),
        name="ffn",
    )(x, ada, wgu, wd, g, bias)


def _rotate_half_cols(w):
    half = MLA_ROPE // 2
    return jnp.concatenate([-w[..., half:], w[..., :half]], axis=-1)


def _prep_layer_weights(l, w_in, gla_w_gate, gla_b_gate, gla_norm_g, lru_conv_w, lru_conv_b, lru_w_a,
                        lru_b_a, lru_w_x, lru_b_x, lru_lambda, mla_q_norm_g, mla_w_uq, mla_kv_norm_g,
                        w_qlat, w_out, ffn_w_gu, ffn_w_down):
    w = w_in[l]
    d = w.shape[0]
    o = np.cumsum([0, 128, 128, 256, 256, 16, 256, 256, 256, 128, 32])
    gq, gk, gv, gg, glr, lx, lgt, dq, dkv, kr = [w[:, o[i]:o[i + 1]] for i in range(10)]
    tail = jnp.concatenate([kr, _rotate_half_cols(kr), glr,
                            jnp.zeros((d, LANES - 2 * MLA_ROPE - GLA_LOWRANK), F32)], axis=1)
    w_in_p = jnp.concatenate([gq * (GLA_DK ** -0.5), gk, gv, gg, lx, lgt, dq, dkv, tail],
                             axis=1).astype(BF16)
    wg_p = jnp.zeros((LANES, LANES), F32).at[2 * MLA_ROPE:2 * MLA_ROPE + GLA_LOWRANK].set(
        gla_w_gate[l]).astype(BF16)
    bg = gla_b_gate[l].reshape(1, LANES)
    ng = jnp.tile(gla_norm_g[l], GLA_HEADS).reshape(1, GLA_WIDTH)

    rope_w = mla_w_uq[l][:, :, MLA_NOPE:] * ATTN_SCALE
    rope_blk = jnp.concatenate(
        [rope_w, _rotate_half_cols(rope_w),
         jnp.zeros((MLA_Q_LORA, MLA_HEADS, LANES - 2 * MLA_ROPE), F32)], axis=-1)
    wq2 = jnp.concatenate([w_qlat[l], rope_blk.reshape(MLA_Q_LORA, MLA_HEADS * LANES).astype(BF16)],
                          axis=1)

    def block_diag(wb):
        out = jnp.zeros((LRU_WIDTH, LRU_WIDTH), F32)
        for i in range(LRU_BLOCKS):
            out = out.at[i * LRU_BLOCK_W:(i + 1) * LRU_BLOCK_W,
                         i * LRU_BLOCK_W:(i + 1) * LRU_BLOCK_W].set(wb[i])
        return out

    wax = jnp.concatenate([block_diag(lru_w_a[l]), block_diag(lru_w_x[l])], axis=1).astype(BF16)
    bax = jnp.concatenate([lru_b_a[l], lru_b_x[l]]).reshape(1, 2 * LRU_WIDTH)
    sp = jax.nn.softplus(-lru_lambda[l].astype(F32)).reshape(1, LRU_WIDTH)
    return dict(
        w_in_p=w_in_p, wg_p=wg_p, bg=bg, ng=ng, wq2=wq2,
        qn=mla_q_norm_g[l].reshape(1, MLA_Q_LORA), kvn=mla_kv_norm_g[l].reshape(1, MLA_KV_LORA),
        cw=lru_conv_w[l], cb=lru_conv_b[l].reshape(1, LRU_WIDTH), wax=wax, bax=bax, sp=sp,
        wgl=w_out[l][:GLA_WIDTH + LRU_WIDTH].astype(BF16),
        wgu=ffn_w_gu[l].astype(BF16), wd=ffn_w_down[l].astype(BF16))


def _rope_table(pos):
    half = MLA_ROPE // 2
    inv_freq = ROPE_THETA ** (-jnp.arange(half, dtype=F32) / half)
    ang = pos.astype(F32)[:, None] * inv_freq[None, :]
    cos, sin = jnp.cos(ang), jnp.sin(ang)
    return jnp.concatenate([cos, cos, sin, sin,
                            jnp.zeros((pos.shape[0], LANES - 2 * MLA_ROPE), F32)], axis=1)


def _state_to_blockdiag_t(s):
    b = s.shape[0]
    out = jnp.zeros((b, GLA_WIDTH, GLA_HEADS * GLA_DK), F32)
    for h in range(GLA_HEADS):
        out = out.at[:, h * GLA_DV:(h + 1) * GLA_DV, h * GLA_DK:(h + 1) * GLA_DK].set(
            jnp.swapaxes(s[:, h], 1, 2))
    return out


def _blockdiag_t_to_state(st):
    return jnp.stack([jnp.swapaxes(st[:, h * GLA_DV:(h + 1) * GLA_DV, h * GLA_DK:(h + 1) * GLA_DK], 1, 2)
                      for h in range(GLA_HEADS)], axis=1)


def _group_layer(x, ada, lw, wfold_l, cs, s0T, h0, cbuf, tiles, alpha, ln1, ln2, attend):
    b, t, _ = x.shape
    bb, tt, nb = tiles
    proj, qcat, ckv_new, kr_new, kcat = _inproj(x, ada, lw["w_in_p"], lw["wg_p"], lw["bg"], lw["qn"],
                                                lw["kvn"], lw["wq2"], cs, bb, tt)
    o_gla, sT = _gla(proj, s0T, lw["ng"], b, t, nb)
    o_lru, h_new, conv_new = _lru(proj, cbuf, h0, lw["cw"], lw["cb"], lw["wax"], lw["bax"], lw["sp"],
                                  b, t, nb)
    o_lat = attend(qcat, kcat, ckv_new, kr_new)
    x1 = _outproj(x, ada, o_gla, o_lru, o_lat.reshape(b * t, -1), lw["wgl"], wfold_l,
                  ln1[0], ln1[1], bb, tt, alpha)
    x2 = _ffn(x1, ada, lw["wgu"], lw["wd"], ln2[0], ln2[1], bb, tt, alpha)
    states = (_blockdiag_t_to_state(sT), h_new.reshape(b, LRU_WIDTH), conv_new,
              ckv_new.reshape(b, t, MLA_KV_LORA), kr_new.reshape(b, t, MLA_ROPE))
    return x2, states


def kernel(x_prompt, x_sample, c_prompt, c_sample, state_gla, state_lru, state_conv, cache_ckv, cache_krope, page_table, ln_in_g, ln_in_b, w_ada, b_ada, w_in, gla_w_gate, gla_b_gate, gla_norm_g, lru_conv_w, lru_conv_b, lru_w_a, lru_b_a, lru_w_x, lru_b_x, lru_lambda, mla_q_norm_g, mla_w_uq, mla_kv_norm_g, mla_w_uk, mla_w_uv, w_out, ln1_g, ln1_b, ffn_w_gu, ffn_w_down, ln2_g, ln2_b):
    bp, tp, d = x_prompt.shape
    bs, ts, _ = x_sample.shape
    depth = w_in.shape[0]
    n_pages = page_table.shape[1]
    past_len = n_pages * PAGE_SIZE
    alpha = (2.0 * depth) ** 0.25

    tiles_p = (1, min(512, tp), 2 if bp % 2 == 0 else 1)
    tiles_s = (bs, ts, 4 if bs % 4 == 0 else 1)
    pages_per_step = 32 if n_pages % 32 == 0 else n_pages

    ada = _ada_all(jnp.concatenate([c_prompt, c_sample], axis=0), w_ada, b_ada)
    ada = ada.reshape(depth, bp + bs, 6, d)
    xp = _ln_in(x_prompt.reshape(bp * tp, d), ln_in_g, ln_in_b).reshape(bp, tp, d)
    xs = _ln_in(x_sample.reshape(bs * ts, d), ln_in_g, ln_in_b).reshape(bs, ts, d)
    w_qlat, w_fold = _fold_weights(mla_w_uq, mla_w_uk, mla_w_uv, w_out)

    cs_p = _rope_table(jnp.arange(tp, dtype=jnp.int32))
    cs_s = jnp.tile(_rope_table(past_len + jnp.arange(ts, dtype=jnp.int32)), (bs, 1))
    zero_sT = jnp.zeros((bp, GLA_WIDTH, GLA_HEADS * GLA_DK), F32)
    zero_h = jnp.zeros((bp, 1, LRU_WIDTH), F32)
    zero_conv = jnp.zeros((bp, CONV_WIDTH - 1, LRU_WIDTH), F32)

    st_p, st_s = [], []
    for l in range(depth):
        lw = _prep_layer_weights(l, w_in, gla_w_gate, gla_b_gate, gla_norm_g, lru_conv_w, lru_conv_b,
                                 lru_w_a, lru_b_a, lru_w_x, lru_b_x, lru_lambda, mla_q_norm_g, mla_w_uq,
                                 mla_kv_norm_g, w_qlat, w_out, ffn_w_gu, ffn_w_down)
        ln1 = (ln1_g[l].reshape(1, d), ln1_b[l].reshape(1, d))
        ln2 = (ln2_g[l].reshape(1, d), ln2_b[l].reshape(1, d))

        def attend_p(qcat, kcat, ckv_new, kr_new):
            return _attn_prompt(qcat, kcat, bp, tp)

        def attend_s(qcat, kcat, ckv_new, kr_new, l=l):
            return _attn_paged(qcat, ckv_new, kr_new, cache_ckv, cache_krope, page_table, l,
                               pages_per_step)

        xp, sp = _group_layer(xp, ada[l, :bp], lw, w_fold[l], cs_p, zero_sT, zero_h, zero_conv,
                              tiles_p, alpha, ln1, ln2, attend_p)
        xs, ss = _group_layer(xs, ada[l, bp:], lw, w_fold[l], cs_s, _state_to_blockdiag_t(state_gla[l]),
                              state_lru[l].reshape(bs, 1, LRU_WIDTH), state_conv[l],
                              tiles_s, alpha, ln1, ln2, attend_s)
        st_p.append(sp)
        st_s.append(ss)

    def stk(outs, j):
        return jnp.stack([o[j] for o in outs])

    return (xp, xs, stk(st_p, 0), stk(st_s, 0), stk(st_p, 1), stk(st_s, 1), stk(st_p, 2), stk(st_s, 2),
            stk(st_p, 3), stk(st_s, 3), stk(st_p, 4), stk(st_s, 4))
```

```python
import functools
import math

import numpy as np
import jax
import jax.numpy as jnp
from jax import lax
from jax.experimental import pallas as pl
from jax.experimental.pallas import tpu as pltpu

F32 = jnp.float32
BF16 = jnp.bfloat16
HIGHEST = lax.Precision.HIGHEST

D_MODEL = 1024
PAGE_SIZE = 128
GLA_HEADS = 4
GLA_DK = 32
GLA_DV = 64
GLA_WIDTH = GLA_HEADS * GLA_DV
GLA_LOWRANK = 16
GLA_GATE_TAU = 16.0
GLA_CHUNK = 32
LRU_WIDTH = 256
LRU_BLOCKS = 4
LRU_BLOCK_W = LRU_WIDTH // LRU_BLOCKS
CONV_WIDTH = 4
LRU_C = 8.0
MLA_HEADS = 8
MLA_NOPE = 64
MLA_ROPE = 32
MLA_V = 64
MLA_Q_LORA = 256
MLA_KV_LORA = 128
ROPE_THETA = 10000.0
D_FF = 2816
LN_EPS = 1e-5
RMS_EPS = 1e-6
ATTN_SCALE = (MLA_NOPE + MLA_ROPE) ** -0.5
QUERY_SCALE = ATTN_SCALE * math.log2(math.e)
ATTN_ROW_GROUP = 256

LANES = 128
SUBLANES = 8
VMEM_LIMIT_BYTES = 56 * 1024 * 1024

C_Q, C_K, C_V, C_GG, C_LX, C_LG = 0, 128, 256, 512, 768, 1024
C_DQ, C_DKV, C_TAIL = 1280, 1536, 1664
W_IN_COLS = 1792
C_LOGF = 1280
PROJ_COLS = 1408
QK_WIDTH = 256


def _cparams(sem):
    return pltpu.CompilerParams(dimension_semantics=sem, vmem_limit_bytes=VMEM_LIMIT_BYTES)


def _layer_norm_rows(y, g, b):
    mu = jnp.mean(y, axis=-1, keepdims=True)
    yc = y - mu
    var = jnp.mean(yc * yc, axis=-1, keepdims=True)
    return yc * lax.rsqrt(var + LN_EPS) * g + b


def _rms_rows(y, g):
    return y * lax.rsqrt(jnp.mean(y * y, axis=-1, keepdims=True) + RMS_EPS) * g


def _ada_kernel(c_ref, w_ref, b_ref, o_ref):
    c = c_ref[...]
    s = (c * jax.nn.sigmoid(c)).astype(BF16)
    o_ref[0] = jnp.dot(s, w_ref[0].astype(BF16), preferred_element_type=F32) + b_ref[0]


def _ada_all(c_all, w_ada, b_ada):
    depth, d, n = w_ada.shape
    bt = c_all.shape[0]
    tn = 1536
    return pl.pallas_call(
        _ada_kernel,
        grid=(depth, n // tn),
        in_specs=[pl.BlockSpec((bt, d), lambda l, j: (0, 0)),
                  pl.BlockSpec((1, d, tn), lambda l, j: (l, 0, j)),
                  pl.BlockSpec((1, 1, tn), lambda l, j: (l, 0, j))],
        out_specs=pl.BlockSpec((1, bt, tn), lambda l, j: (l, 0, j)),
        out_shape=jax.ShapeDtypeStruct((depth, bt, n), F32),
        compiler_params=_cparams(("parallel", "parallel")),
        name="ada_mod",
    )(c_all, w_ada, b_ada.reshape(depth, 1, n))


def _ln_kernel(x_ref, g_ref, b_ref, o_ref):
    o_ref[...] = _layer_norm_rows(x_ref[...], g_ref[...], b_ref[...])


def _ln_in(x2d, g, b):
    n, d = x2d.shape
    tm = min(1024, n)
    return pl.pallas_call(
        _ln_kernel,
        grid=(n // tm,),
        in_specs=[pl.BlockSpec((tm, d), lambda i: (i, 0)),
                  pl.BlockSpec((1, d), lambda i: (0, 0)),
                  pl.BlockSpec((1, d), lambda i: (0, 0))],
        out_specs=pl.BlockSpec((tm, d), lambda i: (i, 0)),
        out_shape=jax.ShapeDtypeStruct((n, d), F32),
        compiler_params=_cparams(("parallel",)),
        name="ln_in",
    )(x2d, g.reshape(1, d), b.reshape(1, d))


def _fold_kernel(uq_ref, uk_ref, uv_ref, wo_ref, qlat_ref, fold_ref):
    a = uq_ref[0, 0]
    b = uk_ref[0, 0]
    ql = lax.dot_general(a, b, (((1,), (1,)), ((), ())), precision=HIGHEST,
                         preferred_element_type=F32)
    qlat_ref[0] = (ql * QUERY_SCALE).astype(BF16)
    fd = jnp.dot(uv_ref[0, 0], wo_ref[0], precision=HIGHEST, preferred_element_type=F32)
    fold_ref[0] = fd.astype(BF16)


def _fold_weights(mla_w_uq, mla_w_uk, mla_w_uv, w_out):
    depth = mla_w_uq.shape[0]
    uq_n = jnp.transpose(mla_w_uq[..., :MLA_NOPE], (0, 2, 1, 3))
    uk_t = jnp.transpose(mla_w_uk, (0, 2, 1, 3))
    uv_t = jnp.transpose(mla_w_uv, (0, 2, 1, 3))
    mla_row0 = (GLA_WIDTH + LRU_WIDTH) // MLA_V
    return pl.pallas_call(
        _fold_kernel,
        grid=(depth, MLA_HEADS),
        in_specs=[pl.BlockSpec((1, 1, MLA_Q_LORA, MLA_NOPE), lambda l, h: (l, h, 0, 0)),
                  pl.BlockSpec((1, 1, MLA_KV_LORA, MLA_NOPE), lambda l, h: (l, h, 0, 0)),
                  pl.BlockSpec((1, 1, MLA_KV_LORA, MLA_V), lambda l, h: (l, h, 0, 0)),
                  pl.BlockSpec((1, MLA_V, D_MODEL), lambda l, h: (l, mla_row0 + h, 0))],
        out_specs=[pl.BlockSpec((1, MLA_Q_LORA, MLA_KV_LORA), lambda l, h: (l, 0, h)),
                   pl.BlockSpec((1, MLA_KV_LORA, D_MODEL), lambda l, h: (l, h, 0))],
        out_shape=[jax.ShapeDtypeStruct((depth, MLA_Q_LORA, MLA_HEADS * MLA_KV_LORA), BF16),
                   jax.ShapeDtypeStruct((depth, MLA_HEADS * MLA_KV_LORA, D_MODEL), BF16)],
        compiler_params=_cparams(("parallel", "parallel")),
        name="fold_weights",
    )(uq_n, uk_t, uv_t, w_out)


def _inproj_kernel(x_ref, ada_ref, w_ref, wg_ref, bg_ref, qn_ref, kvn_ref, wq2_ref, cs_ref,
                   proj_ref, qcat_ref, ckv_ref, kr_ref, kcat_ref):
    bb, tt, d = x_ref.shape
    m = bb * tt
    x = x_ref[...]
    shift = ada_ref[:, 0:1, :]
    scale = ada_ref[:, 1:2, :]
    h = (x * (1.0 + scale) + shift).reshape(m, d).astype(BF16)
    p = jnp.dot(h, w_ref[...], preferred_element_type=F32)
    proj_ref[:, 0:C_DQ] = p[:, 0:C_DQ]

    tail = p[:, C_TAIL:C_TAIL + LANES]
    z = jnp.dot(tail.astype(BF16), wg_ref[...], preferred_element_type=F32) + bg_ref[...]
    proj_ref[:, C_LOGF:C_LOGF + LANES] = jax.nn.log_sigmoid(z) / GLA_GATE_TAU

    cs = cs_ref[...]
    lane = lax.broadcasted_iota(jnp.int32, (m, LANES), 1)
    rope_lanes = lane < MLA_ROPE

    def rope(block):
        r = block * cs
        r = r + pltpu.roll(r, LANES - MLA_ROPE, 1)
        return jnp.where(rope_lanes, r, 0.0)

    kr = rope(tail)
    kr_ref[...] = kr[:, 0:MLA_ROPE]
    ckv = _rms_rows(p[:, C_DKV:C_DKV + MLA_KV_LORA], kvn_ref[...])
    ckv_ref[...] = ckv
    kcat_ref[...] = jnp.concatenate([ckv.astype(BF16), kr.astype(BF16)], axis=1)

    dqn = _rms_rows(p[:, C_DQ:C_DQ + MLA_Q_LORA], qn_ref[...]).astype(BF16)
    q2 = jnp.dot(dqn, wq2_ref[...], preferred_element_type=F32)
    for hh in range(MLA_HEADS):
        lat = q2[:, hh * LANES:(hh + 1) * LANES]
        rp = rope(q2[:, (MLA_HEADS + hh) * LANES:(MLA_HEADS + hh + 1) * LANES])
        qc = jnp.concatenate([lat, rp], axis=1)
        qcat_ref[:, hh] = qc.reshape(bb, tt, QK_WIDTH).astype(qcat_ref.dtype)


def _inproj(x, ada, w_in_p, wg_p, bg, qn, kvn, wq2, cs, bb, tt):
    b, t, d = x.shape
    n = b * t
    m = bb * tt
    grid = (b // bb, t // tt)
    nt = t // tt
    const = lambda i, j: (0, 0)
    row = lambda i, j: (i * nt + j, 0)
    return pl.pallas_call(
        _inproj_kernel,
        grid=grid,
        in_specs=[pl.BlockSpec((bb, tt, d), lambda i, j: (i, j, 0)),
                  pl.BlockSpec((bb, 6, d), lambda i, j: (i, 0, 0)),
                  pl.BlockSpec((d, W_IN_COLS), const),
                  pl.BlockSpec((LANES, LANES), const),
                  pl.BlockSpec((1, LANES), const),
                  pl.BlockSpec((1, MLA_Q_LORA), const),
                  pl.BlockSpec((1, MLA_KV_LORA), const),
                  pl.BlockSpec((MLA_Q_LORA, 2 * MLA_HEADS * LANES), const),
                  pl.BlockSpec((m, LANES), lambda i, j: (j, 0))],
        out_specs=[pl.BlockSpec((m, PROJ_COLS), row),
                   pl.BlockSpec((bb, MLA_HEADS, tt, QK_WIDTH), lambda i, j: (i, 0, j, 0)),
                   pl.BlockSpec((m, MLA_KV_LORA), row),
                   pl.BlockSpec((m, MLA_ROPE), row),
                   pl.BlockSpec((m, QK_WIDTH), row)],
        out_shape=[jax.ShapeDtypeStruct((n, PROJ_COLS), F32),
                   jax.ShapeDtypeStruct((b, MLA_HEADS, t, QK_WIDTH), BF16 if tt % 16 == 0 else F32),
                   jax.ShapeDtypeStruct((n, MLA_KV_LORA), F32),
                   jax.ShapeDtypeStruct((n, MLA_ROPE), F32),
                   jax.ShapeDtypeStruct((n, QK_WIDTH), BF16)],
        compiler_params=_cparams(("parallel", "parallel")),
        name="in_proj",
    )(x, ada, w_in_p, wg_p, bg, qn, kvn, wq2, cs)


def _gla_kernel(q_ref, k_ref, v_ref, gg_ref, g_ref, s0_ref, ng_ref, ltri_ref, ind_ref,
                msk_ref, seg_ref, o_ref, sT_ref, st_sc, b_sc, o_sc, *, nb, t, c):
    n_chunks = t // c
    groups = c // SUBLANES
    ltri = ltri_ref[...]
    ind = ind_ref[...]
    msk = msk_ref[...]
    row_iota = lax.broadcasted_iota(jnp.int32, (c, LANES), 0)
    for j in range(nb):
        st_sc[j] = s0_ref[j]

    def chunk(ci, carry):
        for j in range(nb):
            r0 = pl.multiple_of(j * t + ci * c, c)
            g = g_ref[pl.ds(r0, c), :]
            b = jnp.dot(ltri, g, precision=HIGHEST, preferred_element_type=F32)
            b_sc[j] = b
            q = q_ref[pl.ds(r0, c), :]
            k = k_ref[pl.ds(r0, c), :]
            v = v_ref[pl.ds(r0, c), :]
            blast = b_sc[j, pl.ds(c - 1, 1), :]
            qe = q * jnp.exp(b)
            ke = k * jnp.exp(blast - b)
            st = st_sc[j]
            o_inter = lax.dot_general(qe.astype(BF16), st.astype(BF16), (((1,), (1,)), ((), ())),
                                      preferred_element_type=F32)
            ut = lax.dot_general(v.astype(BF16), ke.astype(BF16), (((0,), (0,)), ((), ())),
                                 preferred_element_type=F32)
            st_sc[j] = st * jnp.exp(blast) + ut * msk

            o_blk = [None] * groups
            for g0 in range(groups):
                lo = g0 * SUBLANES
                pieces = []
                for s in range(lo, lo + SUBLANES):
                    bs = b_sc[j, pl.ds(s, 1), :]
                    ks = k_ref[pl.ds(r0 + s, 1), :]
                    e = jnp.exp(b[lo:, :] - bs)
                    head = jnp.where(row_iota[lo:lo + SUBLANES, :] >= s, e[0:SUBLANES, :], 0.0)
                    e = head if c - lo == SUBLANES else jnp.concatenate([head, e[SUBLANES:, :]], axis=0)
                    pieces.append(e * q[lo:, :] * ks)
                w = jnp.concatenate(pieces, axis=0).astype(BF16)
                a = jnp.dot(w, ind, preferred_element_type=F32)
                rows = c - lo
                for idx in range(SUBLANES):
                    vs = v_ref[pl.ds(r0 + lo + idx, 1), :]
                    for rb in range(g0, groups):
                        piece = a[idx * rows + (rb - g0) * SUBLANES:
                                  idx * rows + (rb - g0 + 1) * SUBLANES, :] * vs
                        o_blk[rb] = piece if o_blk[rb] is None else o_blk[rb] + piece
            o_intra = jnp.concatenate(o_blk, axis=0) if groups > 1 else o_blk[0]
            o_sc[pl.ds(r0, c), :] = o_inter + o_intra
        return carry

    lax.fori_loop(0, n_chunks, chunk, 0)

    for j in range(nb):
        sT_ref[j] = st_sc[j]

    rt = min(256, nb * t)
    ng = ng_ref[...]
    seg = seg_ref[...]

    def epi(i, carry):
        r0 = pl.multiple_of(i * rt, rt)
        o = o_sc[pl.ds(r0, rt), :]
        ms = jnp.dot(o * o, seg, precision=HIGHEST, preferred_element_type=F32)
        gg = gg_ref[pl.ds(r0, rt), :]
        o_ref[pl.ds(r0, rt), :] = o * lax.rsqrt(ms + RMS_EPS) * ng * (gg * jax.nn.sigmoid(gg))
        return carry

    lax.fori_loop(0, (nb * t) // rt, epi, 0)


def _gla(proj, s0T, ng, b, t, nb):
    n = b * t
    c = min(GLA_CHUNK, t)
    rows = nb * t
    ltri = jnp.asarray(np.tril(np.ones((c, c), np.float32)))
    hk = np.arange(GLA_HEADS * GLA_DK) // GLA_DK
    hv = np.arange(GLA_WIDTH) // GLA_DV
    ind = jnp.asarray((hk[:, None] == hv[None, :]).astype(np.float32)).astype(BF16)
    msk = jnp.asarray((hv[:, None] == hk[None, :]).astype(np.float32))
    seg = jnp.asarray((hv[:, None] == hv[None, :]).astype(np.float32) / GLA_DV)
    const = lambda i: (0, 0)
    kern = functools.partial(_gla_kernel, nb=nb, t=t, c=c)
    return pl.pallas_call(
        kern,
        grid=(b // nb,),
        in_specs=[pl.BlockSpec((rows, LANES), lambda i: (i, C_Q // LANES)),
                  pl.BlockSpec((rows, LANES), lambda i: (i, C_K // LANES)),
                  pl.BlockSpec((rows, GLA_WIDTH), lambda i: (i, C_V // GLA_WIDTH)),
                  pl.BlockSpec((rows, GLA_WIDTH), lambda i: (i, C_GG // GLA_WIDTH)),
                  pl.BlockSpec((rows, LANES), lambda i: (i, C_LOGF // LANES)),
                  pl.BlockSpec((nb, GLA_WIDTH, LANES), lambda i: (i, 0, 0)),
                  pl.BlockSpec((1, GLA_WIDTH), const),
                  pl.BlockSpec((c, c), const),
                  pl.BlockSpec((LANES, GLA_WIDTH), const),
                  pl.BlockSpec((GLA_WIDTH, LANES), const),
                  pl.BlockSpec((GLA_WIDTH, GLA_WIDTH), const)],
        out_specs=[pl.BlockSpec((rows, GLA_WIDTH), lambda i: (i, 0)),
                   pl.BlockSpec((nb, GLA_WIDTH, LANES), lambda i: (i, 0, 0))],
        out_shape=[jax.ShapeDtypeStruct((n, GLA_WIDTH), F32),
                   jax.ShapeDtypeStruct((b, GLA_WIDTH, LANES), F32)],
        scratch_shapes=[pltpu.VMEM((nb, GLA_WIDTH, LANES), F32),
                        pltpu.VMEM((nb, c, LANES), F32),
                        pltpu.VMEM((rows, GLA_WIDTH), F32)],
        compiler_params=_cparams(("parallel",)),
        name="gla",
    )(proj, proj, proj, proj, proj, s0T, ng, ltri, ind, msk, seg)


def _lru_kernel(lx_ref, lg_ref, cbuf_ref, h0_ref, cw_ref, cb_ref, wax_ref, bax_ref, sp_ref,
                o_ref, hn_ref, cn_ref, xp_sc, a_sc, u_sc, *, nb, t):
    pad = SUBLANES
    hist = CONV_WIDTH - 1
    row8 = lax.broadcasted_iota(jnp.int32, (SUBLANES, LRU_WIDTH), 0)
    rows_per_iter = min(32, t)
    sub = rows_per_iter // SUBLANES

    def scan_block(a, u):
        for dd in (1, 2, 4):
            a_s = jnp.where(row8 >= dd, pltpu.roll(a, dd, 0), 1.0)
            u_s = jnp.where(row8 >= dd, pltpu.roll(u, dd, 0), 0.0)
            u = a * u_s + u
            a = a * a_s
        return a, u

    for j in range(nb):
        base = j * t
        xp_sc[pl.ds(0, pad), :] = jnp.zeros((pad, LRU_WIDTH), F32)
        xp_sc[pl.ds(pad - hist, hist), :] = cbuf_ref[j]
        xp_sc[pl.ds(pad, t), :] = lx_ref[pl.ds(base, t), :]
        xc = cb_ref[...]
        for kk in range(CONV_WIDTH):
            xc = xc + xp_sc[pl.ds(pad - hist + kk, t), :] * cw_ref[pl.ds(kk, 1), :]
        cn_ref[j] = xp_sc[pl.ds(pad + t - hist, hist), :]

        ax = jnp.dot(xc.astype(BF16), wax_ref[...], preferred_element_type=F32) + bax_ref[...]
        r = jax.nn.sigmoid(ax[:, 0:LRU_WIDTH])
        ig = jax.nn.sigmoid(ax[:, LRU_WIDTH:2 * LRU_WIDTH])
        log_a = -LRU_C * r * sp_ref[...]
        a = jnp.exp(log_a)
        u = jnp.sqrt((a * a + 1.0) * jnp.tanh(-log_a)) * (ig * xc)
        a_sc[...] = a
        u_sc[...] = u
        u_sc[pl.ds(0, 1), :] = u[0:1, :] + a[0:1, :] * h0_ref[j]

        def step(i, hprev):
            r0 = pl.multiple_of(i * rows_per_iter, rows_per_iter)
            scans = []
            for sb in range(sub):
                ab = a_sc[pl.ds(r0 + sb * SUBLANES, SUBLANES), :]
                ub = u_sc[pl.ds(r0 + sb * SUBLANES, SUBLANES), :]
                scans.append(scan_block(ab, ub))
            for sb in range(sub):
                ac, uc = scans[sb]
                hb = ac * hprev + uc
                u_sc[pl.ds(r0 + sb * SUBLANES, SUBLANES), :] = hb
                hprev = hb[SUBLANES - 1:SUBLANES, :]
            return hprev

        hlast = lax.fori_loop(0, t // rows_per_iter, step, jnp.zeros((1, LRU_WIDTH), F32))
        hn_ref[j] = hlast
        lg = lg_ref[pl.ds(base, t), :]
        o_ref[pl.ds(base, t), :] = jax.nn.gelu(lg, approximate=True) * u_sc[...]


def _lru(proj, cbuf, h0, cw, cb, wax, bax, sp, b, t, nb):
    n = b * t
    rows = nb * t
    const = lambda i: (0, 0)
    kern = functools.partial(_lru_kernel, nb=nb, t=t)
    return pl.pallas_call(
        kern,
        grid=(b // nb,),
        in_specs=[pl.BlockSpec((rows, LRU_WIDTH), lambda i: (i, C_LX // LRU_WIDTH)),
                  pl.BlockSpec((rows, LRU_WIDTH), lambda i: (i, C_LG // LRU_WIDTH)),
                  pl.BlockSpec((nb, CONV_WIDTH - 1, LRU_WIDTH), lambda i: (i, 0, 0)),
                  pl.BlockSpec((nb, 1, LRU_WIDTH), lambda i: (i, 0, 0)),
                  pl.BlockSpec((CONV_WIDTH, LRU_WIDTH), const),
                  pl.BlockSpec((1, LRU_WIDTH), const),
                  pl.BlockSpec((LRU_WIDTH, 2 * LRU_WIDTH), const),
                  pl.BlockSpec((1, 2 * LRU_WIDTH), const),
                  pl.BlockSpec((1, LRU_WIDTH), const)],
        out_specs=[pl.BlockSpec((rows, LRU_WIDTH), lambda i: (i, 0)),
                   pl.BlockSpec((nb, 1, LRU_WIDTH), lambda i: (i, 0, 0)),
                   pl.BlockSpec((nb, CONV_WIDTH - 1, LRU_WIDTH), lambda i: (i, 0, 0))],
        out_shape=[jax.ShapeDtypeStruct((n, LRU_WIDTH), F32),
                   jax.ShapeDtypeStruct((b, 1, LRU_WIDTH), F32),
                   jax.ShapeDtypeStruct((b, CONV_WIDTH - 1, LRU_WIDTH), F32)],
        scratch_shapes=[pltpu.VMEM((t + SUBLANES, LRU_WIDTH), F32),
                        pltpu.VMEM((t, LRU_WIDTH), F32),
                        pltpu.VMEM((t, LRU_WIDTH), F32)],
        compiler_params=_cparams(("parallel",)),
        name="rg_lru",
    )(proj, proj, cbuf, h0, cw, cb, wax, bax, sp)


def _softmax_step(s, m_ref, l_ref, acc_ref, rows, v):
    reps = s.shape[1] // LANES
    m_prev = m_ref[rows, :]
    m_next = jnp.maximum(m_prev, jnp.max(s, axis=1, keepdims=True))
    alpha = jnp.exp2(m_prev - m_next)
    m_wide = m_next if reps == 1 else jnp.concatenate([m_next] * reps, axis=1)
    p = jnp.exp2(s - m_wide)
    l_ref[rows, :] = alpha * l_ref[rows, :] + jnp.sum(p, axis=1, keepdims=True)
    acc_ref[rows, :] = alpha * acc_ref[rows, :] + jnp.dot(p.astype(BF16), v,
                                                          preferred_element_type=F32)
    m_ref[rows, :] = m_next


def _attn_kernel(q_ref, k_ref, o_ref, m_sc, l_sc, acc_sc, *, tq, rg):
    i = pl.program_id(1)
    r = MLA_HEADS * tq
    m_sc[...] = jnp.full((r, LANES), -jnp.inf, F32)
    l_sc[...] = jnp.zeros((r, LANES), F32)
    acc_sc[...] = jnp.zeros((r, MLA_KV_LORA), F32)

    def update(j, masked):
        kb = k_ref[0, pl.ds(pl.multiple_of(j * tq, tq), tq), :]
        v = kb[:, 0:MLA_KV_LORA]
        for g in range(r // rg):
            hh, t0 = divmod(g * rg, tq)
            rows = pl.ds(g * rg, rg)
            q = q_ref[0, hh, pl.ds(t0, rg), :]
            s = lax.dot_general(q, kb, (((1,), (1,)), ((), ())), preferred_element_type=F32)
            if masked:
                qpos = t0 + lax.broadcasted_iota(jnp.int32, (rg, tq), 0)
                kpos = lax.broadcasted_iota(jnp.int32, (rg, tq), 1)
                s = jnp.where(kpos <= qpos, s, -jnp.inf)
            _softmax_step(s, m_sc, l_sc, acc_sc, rows, v)

    def body(j, carry):
        update(j, False)
        return carry

    lax.fori_loop(0, i, body, 0)
    update(i, True)
    for hh in range(MLA_HEADS):
        rows = pl.ds(hh * tq, tq)
        o_ref[0, :, hh * MLA_KV_LORA:(hh + 1) * MLA_KV_LORA] = \
            (acc_sc[rows, :] / l_sc[rows, :]).astype(BF16)


def _attn_prompt(qcat, kcat, b, t):
    tq = min(256, t)
    r = MLA_HEADS * tq
    kern = functools.partial(_attn_kernel, tq=tq, rg=min(ATTN_ROW_GROUP, tq))
    return pl.pallas_call(
        kern,
        grid=(b, t // tq),
        in_specs=[pl.BlockSpec((1, MLA_HEADS, tq, QK_WIDTH), lambda bi, i: (bi, 0, i, 0)),
                  pl.BlockSpec((1, t, QK_WIDTH), lambda bi, i: (bi, 0, 0))],
        out_specs=pl.BlockSpec((1, tq, MLA_HEADS * MLA_KV_LORA), lambda bi, i: (bi, i, 0)),
        out_shape=jax.ShapeDtypeStruct((b, t, MLA_HEADS * MLA_KV_LORA), BF16),
        scratch_shapes=[pltpu.VMEM((r, LANES), F32), pltpu.VMEM((r, LANES), F32),
                        pltpu.VMEM((r, MLA_KV_LORA), F32)],
        compiler_params=_cparams(("parallel", "parallel")),
        name="attn_prompt",
    )(qcat, kcat.reshape(b, t, QK_WIDTH))


def _attn_paged_kernel(pt_ref, q_ref, cnew_ref, rnew_ref, *refs, pages, ts):
    ck_refs = refs[0:pages]
    kr_refs = refs[pages:2 * pages]
    o_ref, m_sc, l_sc, acc_sc = refs[2 * pages:]
    j = pl.program_id(1)
    r = MLA_HEADS * ts
    q = q_ref[0].reshape(r, QK_WIDTH).astype(BF16)
    q_lat = q[:, 0:MLA_KV_LORA]
    q_rope = q[:, MLA_KV_LORA:MLA_KV_LORA + MLA_ROPE]

    all_rows = pl.ds(0, r)

    @pl.when(j == 0)
    def _():
        m_sc[...] = jnp.full((r, LANES), -jnp.inf, F32)
        l_sc[...] = jnp.zeros((r, LANES), F32)
        acc_sc[...] = jnp.zeros((r, MLA_KV_LORA), F32)

    kc = jnp.concatenate([c[...].astype(BF16) for c in ck_refs], axis=0)
    krt = jnp.concatenate([c[...].astype(BF16) for c in kr_refs], axis=1)
    s = (lax.dot_general(q_lat, kc, (((1,), (1,)), ((), ())), preferred_element_type=F32)
         + jnp.dot(q_rope, krt, preferred_element_type=F32))
    _softmax_step(s, m_sc, l_sc, acc_sc, all_rows, kc)

    @pl.when(j == pl.num_programs(1) - 1)
    def _():
        pad = jnp.zeros((LANES - ts, MLA_KV_LORA), F32)
        kc_new = jnp.concatenate([cnew_ref[...], pad], axis=0).astype(BF16)
        kr_new = jnp.concatenate([rnew_ref[...], pad[:, 0:MLA_ROPE]], axis=0).astype(BF16)
        s_new = (lax.dot_general(q_lat, kc_new, (((1,), (1,)), ((), ())), preferred_element_type=F32)
                 + lax.dot_general(q_rope, kr_new, (((1,), (1,)), ((), ())),
                                   preferred_element_type=F32))
        qpos = lax.broadcasted_iota(jnp.int32, (r, LANES), 0) & (ts - 1)
        kpos = lax.broadcasted_iota(jnp.int32, (r, LANES), 1)
        s_new = jnp.where(kpos <= qpos, s_new, -jnp.inf)
        _softmax_step(s_new, m_sc, l_sc, acc_sc, all_rows, kc_new)
        out = acc_sc[...] / l_sc[...]
        for hh in range(MLA_HEADS):
            o_ref[0, :, hh * MLA_KV_LORA:(hh + 1) * MLA_KV_LORA] = out[hh * ts:(hh + 1) * ts, :]


def _attn_paged(qcat, ckv_new, kr_new, cache_ckv, cache_krope_t, page_table, layer, pages):
    b, _, ts, _ = qcat.shape
    n_pages = page_table.shape[1]
    kern = functools.partial(_attn_paged_kernel, pages=pages, ts=ts)

    def page_spec(rows, width, i):
        return pl.BlockSpec((None, None, rows, width),
                            lambda bi, j, pt: (layer, pt[bi, j * pages + i], 0, 0))

    in_specs = [pl.BlockSpec((1, MLA_HEADS, ts, QK_WIDTH), lambda bi, j, pt: (bi, 0, 0, 0)),
                pl.BlockSpec((ts, MLA_KV_LORA), lambda bi, j, pt: (bi, 0)),
                pl.BlockSpec((ts, MLA_ROPE), lambda bi, j, pt: (bi, 0))]
    in_specs += [page_spec(PAGE_SIZE, MLA_KV_LORA, i) for i in range(pages)]
    in_specs += [page_spec(MLA_ROPE, PAGE_SIZE, i) for i in range(pages)]
    r = MLA_HEADS * ts
    grid_spec = pltpu.PrefetchScalarGridSpec(
        num_scalar_prefetch=1,
        grid=(b, n_pages // pages),
        in_specs=in_specs,
        out_specs=pl.BlockSpec((1, ts, MLA_HEADS * MLA_KV_LORA), lambda bi, j, pt: (bi, 0, 0)),
        scratch_shapes=[pltpu.VMEM((r, LANES), F32), pltpu.VMEM((r, LANES), F32),
                        pltpu.VMEM((r, MLA_KV_LORA), F32)])
    return pl.pallas_call(
        kern,
        grid_spec=grid_spec,
        out_shape=jax.ShapeDtypeStruct((b, ts, MLA_HEADS * MLA_KV_LORA), F32),
        compiler_params=_cparams(("parallel", "arbitrary")),
        name="attn_paged",
    )(page_table, qcat, ckv_new, kr_new, *([cache_ckv] * pages), *([cache_krope_t] * pages))


def _outproj_kernel(x_ref, ada_ref, og_ref, ol_ref, oa_ref, wgl_ref, wf_ref, g_ref, b_ref, o_ref,
                    *, alpha):
    bb, tt, d = x_ref.shape
    m = bb * tt
    gate = ada_ref[:, 2:3, :]
    ogl = jnp.concatenate([og_ref[...], ol_ref[...]], axis=1).astype(BF16)
    mix = (jnp.dot(ogl, wgl_ref[...], preferred_element_type=F32)
           + jnp.dot(oa_ref[...].astype(BF16), wf_ref[...], preferred_element_type=F32))
    y = alpha * x_ref[...] + gate * mix.reshape(bb, tt, d)
    o_ref[...] = _layer_norm_rows(y, g_ref[...], b_ref[...])


def _outproj(x, ada, o_gla, o_lru, o_lat, wgl, wfold, g, bias, bb, tt, alpha):
    b, t, d = x.shape
    nt = t // tt
    m = bb * tt
    const = lambda i, j: (0, 0)
    row = lambda i, j: (i * nt + j, 0)
    kern = functools.partial(_outproj_kernel, alpha=alpha)
    return pl.pallas_call(
        kern,
        grid=(b // bb, nt),
        in_specs=[pl.BlockSpec((bb, tt, d), lambda i, j: (i, j, 0)),
                  pl.BlockSpec((bb, 6, d), lambda i, j: (i, 0, 0)),
                  pl.BlockSpec((m, GLA_WIDTH), row),
                  pl.BlockSpec((m, LRU_WIDTH), row),
                  pl.BlockSpec((m, MLA_HEADS * MLA_KV_LORA), row),
                  pl.BlockSpec((GLA_WIDTH + LRU_WIDTH, d), const),
                  pl.BlockSpec((MLA_HEADS * MLA_KV_LORA, d), const),
                  pl.BlockSpec((1, d), const),
                  pl.BlockSpec((1, d), const)],
        out_specs=pl.BlockSpec((bb, tt, d), lambda i, j: (i, j, 0)),
        out_shape=jax.ShapeDtypeStruct((b, t, d), F32),
        compiler_params=_cparams(("parallel", "parallel")),
        name="out_proj",
    )(x, ada, o_gla, o_lru, o_lat, wgl, wfold, g, bias)


def _ffn_kernel(x_ref, ada_ref, wgu_ref, wd_ref, g_ref, b_ref, o_ref, *, alpha, cw):
    bb, tt, d = x_ref.shape
    m = bb * tt
    x = x_ref[...]
    shift = ada_ref[:, 3:4, :]
    scale = ada_ref[:, 4:5, :]
    gate = ada_ref[:, 5:6, :]
    h2 = (x * (1.0 + scale) + shift).reshape(m, d).astype(BF16)
    acc = jnp.zeros((m, d), F32)
    for ci in range(D_FF // cw):
        gf = jnp.dot(h2, wgu_ref[:, ci * cw:(ci + 1) * cw], preferred_element_type=F32)
        uf = jnp.dot(h2, wgu_ref[:, D_FF + ci * cw:D_FF + (ci + 1) * cw],
                     preferred_element_type=F32)
        act = (gf * jax.nn.sigmoid(gf) * uf).astype(BF16)
        acc = acc + jnp.dot(act, wd_ref[ci * cw:(ci + 1) * cw, :], preferred_element_type=F32)
    y = alpha * x + gate * acc.reshape(bb, tt, d)
    o_ref[...] = _layer_norm_rows(y, g_ref[...], b_ref[...])


def _ffn(x, ada, wgu, wd, g, bias, bb, tt, alpha):
    b, t, d = x.shape
    const = lambda i, j: (0, 0)
    kern = functools.partial(_ffn_kernel, alpha=alpha, cw=256)
    return pl.pallas_call(
        kern,
        grid=(b // bb, t // tt),
        in_specs=[pl.BlockSpec((bb, tt, d), lambda i, j: (i, j, 0)),
                  pl.BlockSpec((bb, 6, d), lambda i, j: (i, 0, 0)),
                  pl.BlockSpec((d, 2 * D_FF), const),
                  pl.BlockSpec((D_FF, d), const),
                  pl.BlockSpec((1, d), const),
                  pl.BlockSpec((1, d), const)],
        out_specs=pl.BlockSpec((bb, tt, d), lambda i, j: (i, j, 0)),
        out_shape=jax.ShapeDtypeStruct((b, t, d), F32),
        compiler_params=_cparams(("parallel", "parallel")),
        name="ffn",
    )(x, ada, wgu, wd, g, bias)


def _rotate_half_cols(w):
    half = MLA_ROPE // 2
    return jnp.concatenate([-w[..., half:], w[..., :half]], axis=-1)


def _prep_layer_weights(l, w_in, gla_w_gate, gla_b_gate, gla_norm_g, lru_conv_w, lru_conv_b, lru_w_a,
                        lru_b_a, lru_w_x, lru_b_x, lru_lambda, mla_q_norm_g, mla_w_uq, mla_kv_norm_g,
                        w_qlat, w_out, ffn_w_gu, ffn_w_down):
    w = w_in[l]
    d = w.shape[0]
    o = np.cumsum([0, 128, 128, 256, 256, 16, 256, 256, 256, 128, 32])
    gq, gk, gv, gg, glr, lx, lgt, dq, dkv, kr = [w[:, o[i]:o[i + 1]] for i in range(10)]
    tail = jnp.concatenate([kr, _rotate_half_cols(kr), glr,
                            jnp.zeros((d, LANES - 2 * MLA_ROPE - GLA_LOWRANK), F32)], axis=1)
    w_in_p = jnp.concatenate([gq * (GLA_DK ** -0.5), gk, gv, gg, lx, lgt, dq, dkv, tail],
                             axis=1).astype(BF16)
    wg_p = jnp.pad(gla_w_gate[l], ((2 * MLA_ROPE, LANES - 2 * MLA_ROPE - GLA_LOWRANK),
                                   (0, 0))).astype(BF16)
    bg = gla_b_gate[l].reshape(1, LANES)
    ng = jnp.tile(gla_norm_g[l], GLA_HEADS).reshape(1, GLA_WIDTH)

    rope_w = mla_w_uq[l][:, :, MLA_NOPE:] * QUERY_SCALE
    rope_blk = jnp.concatenate(
        [rope_w, _rotate_half_cols(rope_w),
         jnp.zeros((MLA_Q_LORA, MLA_HEADS, LANES - 2 * MLA_ROPE), F32)], axis=-1)
    wq2 = jnp.concatenate([w_qlat[l], rope_blk.reshape(MLA_Q_LORA, MLA_HEADS * LANES).astype(BF16)],
                          axis=1)

    def block_diag(wb):
        on_diag = jnp.eye(LRU_BLOCKS, dtype=bool)[:, None, :, None]
        return jnp.where(on_diag, wb[:, :, None, :], 0.0).reshape(LRU_WIDTH, LRU_WIDTH)

    wax = jnp.concatenate([block_diag(lru_w_a[l]), block_diag(lru_w_x[l])], axis=1).astype(BF16)
    bax = jnp.concatenate([lru_b_a[l], lru_b_x[l]]).reshape(1, 2 * LRU_WIDTH)
    sp = jax.nn.softplus(-lru_lambda[l].astype(F32)).reshape(1, LRU_WIDTH)
    return dict(
        w_in_p=w_in_p, wg_p=wg_p, bg=bg, ng=ng, wq2=wq2,
        qn=mla_q_norm_g[l].reshape(1, MLA_Q_LORA), kvn=mla_kv_norm_g[l].reshape(1, MLA_KV_LORA),
        cw=lru_conv_w[l], cb=lru_conv_b[l].reshape(1, LRU_WIDTH), wax=wax, bax=bax, sp=sp,
        wgl=w_out[l][:GLA_WIDTH + LRU_WIDTH].astype(BF16),
        wgu=ffn_w_gu[l].astype(BF16), wd=ffn_w_down[l].astype(BF16))


def _rope_table(pos):
    half = MLA_ROPE // 2
    inv_freq = ROPE_THETA ** (-jnp.arange(half, dtype=F32) / half)
    ang = pos.astype(F32)[:, None] * inv_freq[None, :]
    cos, sin = jnp.cos(ang), jnp.sin(ang)
    return jnp.concatenate([cos, cos, sin, sin,
                            jnp.zeros((pos.shape[0], LANES - 2 * MLA_ROPE), F32)], axis=1)


def _state_to_blockdiag_t(s):
    b = s.shape[0]
    on_diag = jnp.eye(GLA_HEADS, dtype=bool)[None, :, None, :, None]
    st = jnp.swapaxes(s, 2, 3)[:, :, :, None, :]
    return jnp.where(on_diag, st, 0.0).reshape(b, GLA_WIDTH, GLA_HEADS * GLA_DK)


def _blockdiag_t_to_state(st):
    return jnp.stack([jnp.swapaxes(st[:, h * GLA_DV:(h + 1) * GLA_DV, h * GLA_DK:(h + 1) * GLA_DK], 1, 2)
                      for h in range(GLA_HEADS)], axis=1)


def _group_layer(x, ada, lw, wfold_l, cs, s0T, h0, cbuf, tiles, alpha, ln1, ln2, attend):
    b, t, _ = x.shape
    bb, tt, nb = tiles
    proj, qcat, ckv_new, kr_new, kcat = _inproj(x, ada, lw["w_in_p"], lw["wg_p"], lw["bg"], lw["qn"],
                                                lw["kvn"], lw["wq2"], cs, bb, tt)
    o_gla, sT = _gla(proj, s0T, lw["ng"], b, t, nb)
    o_lru, h_new, conv_new = _lru(proj, cbuf, h0, lw["cw"], lw["cb"], lw["wax"], lw["bax"], lw["sp"],
                                  b, t, nb)
    o_lat = attend(qcat, kcat, ckv_new, kr_new)
    x1 = _outproj(x, ada, o_gla, o_lru, o_lat.reshape(b * t, -1), lw["wgl"], wfold_l,
                  ln1[0], ln1[1], bb, tt, alpha)
    x2 = _ffn(x1, ada, lw["wgu"], lw["wd"], ln2[0], ln2[1], bb, tt, alpha)
    states = (_blockdiag_t_to_state(sT), h_new.reshape(b, LRU_WIDTH), conv_new,
              ckv_new.reshape(b, t, MLA_KV_LORA), kr_new.reshape(b, t, MLA_ROPE))
    return x2, states


def kernel(x_prompt, x_sample, c_prompt, c_sample, state_gla, state_lru, state_conv, cache_ckv, cache_krope, page_table, ln_in_g, ln_in_b, w_ada, b_ada, w_in, gla_w_gate, gla_b_gate, gla_norm_g, lru_conv_w, lru_conv_b, lru_w_a, lru_b_a, lru_w_x, lru_b_x, lru_lambda, mla_q_norm_g, mla_w_uq, mla_kv_norm_g, mla_w_uk, mla_w_uv, w_out, ln1_g, ln1_b, ffn_w_gu, ffn_w_down, ln2_g, ln2_b):
    bp, tp, d = x_prompt.shape
    bs, ts, _ = x_sample.shape
    depth = w_in.shape[0]
    n_pages = page_table.shape[1]
    past_len = n_pages * PAGE_SIZE
    alpha = (2.0 * depth) ** 0.25

    tiles_p = (1, min(512, tp), 2 if bp % 2 == 0 else 1)
    tiles_s = (bs, ts, 4 if bs % 4 == 0 else 1)
    pages_per_step = 32 if n_pages % 32 == 0 else n_pages

    ada = _ada_all(jnp.concatenate([c_prompt, c_sample], axis=0), w_ada, b_ada)
    ada = ada.reshape(depth, bp + bs, 6, d)
    xp = _ln_in(x_prompt.reshape(bp * tp, d), ln_in_g, ln_in_b).reshape(bp, tp, d)
    xs = _ln_in(x_sample.reshape(bs * ts, d), ln_in_g, ln_in_b).reshape(bs, ts, d)
    w_qlat, w_fold = _fold_weights(mla_w_uq, mla_w_uk, mla_w_uv, w_out)

    cache_krope_t = jnp.swapaxes(cache_krope, 2, 3)
    cs_p = _rope_table(jnp.arange(tp, dtype=jnp.int32))
    cs_s = jnp.tile(_rope_table(past_len + jnp.arange(ts, dtype=jnp.int32)), (bs, 1))
    zero_sT = jnp.zeros((bp, GLA_WIDTH, GLA_HEADS * GLA_DK), F32)
    zero_h = jnp.zeros((bp, 1, LRU_WIDTH), F32)
    zero_conv = jnp.zeros((bp, CONV_WIDTH - 1, LRU_WIDTH), F32)

    st_p, st_s = [], []
    for l in range(depth):
        lw = _prep_layer_weights(l, w_in, gla_w_gate, gla_b_gate, gla_norm_g, lru_conv_w, lru_conv_b,
                                 lru_w_a, lru_b_a, lru_w_x, lru_b_x, lru_lambda, mla_q_norm_g, mla_w_uq,
                                 mla_kv_norm_g, w_qlat, w_out, ffn_w_gu, ffn_w_down)
        ln1 = (ln1_g[l].reshape(1, d), ln1_b[l].reshape(1, d))
        ln2 = (ln2_g[l].reshape(1, d), ln2_b[l].reshape(1, d))

        def attend_p(qcat, kcat, ckv_new, kr_new):
            return _attn_prompt(qcat, kcat, bp, tp)

        def attend_s(qcat, kcat, ckv_new, kr_new, l=l):
            return _attn_paged(qcat, ckv_new, kr_new, cache_ckv, cache_krope_t, page_table, l,
                               pages_per_step)

        xp, sp = _group_layer(xp, ada[l, :bp], lw, w_fold[l], cs_p, zero_sT, zero_h, zero_conv,
                              tiles_p, alpha, ln1, ln2, attend_p)
        xs, ss = _group_layer(xs, ada[l, bp:], lw, w_fold[l], cs_s, _state_to_blockdiag_t(state_gla[l]),
                              state_lru[l].reshape(bs, 1, LRU_WIDTH), state_conv[l],
                              tiles_s, alpha, ln1, ln2, attend_s)
        st_p.append(sp)
        st_s.append(ss)

    def stk(outs, j):
        return jnp.stack([o[j] for o in outs])

    return (xp, xs, stk(st_p, 0), stk(st_s, 0), stk(st_p, 1), stk(st_s, 1), stk(st_p, 2), stk(st_s, 2),
            stk(st_p, 3), stk(st_s, 3), stk(st_p, 4), stk(st_s, 4))
```

```python
import functools
import math

import numpy as np
import jax
import jax.numpy as jnp
from jax import lax
from jax.experimental import pallas as pl
from jax.experimental.pallas import tpu as pltpu

F32 = jnp.float32
BF16 = jnp.bfloat16
HIGHEST = lax.Precision.HIGHEST

D_MODEL = 1024
PAGE_SIZE = 128
GLA_HEADS = 4
GLA_DK = 32
GLA_DV = 64
GLA_WIDTH = GLA_HEADS * GLA_DV
GLA_LOWRANK = 16
GLA_GATE_TAU = 16.0
GLA_CHUNK = 32
LRU_WIDTH = 256
LRU_BLOCKS = 4
LRU_BLOCK_W = LRU_WIDTH // LRU_BLOCKS
CONV_WIDTH = 4
LRU_C = 8.0
MLA_HEADS = 8
MLA_NOPE = 64
MLA_ROPE = 32
MLA_V = 64
MLA_Q_LORA = 256
MLA_KV_LORA = 128
ROPE_THETA = 10000.0
D_FF = 2816
LN_EPS = 1e-5
RMS_EPS = 1e-6
ATTN_SCALE = (MLA_NOPE + MLA_ROPE) ** -0.5
QUERY_SCALE = ATTN_SCALE * math.log2(math.e)
ATTN_BLOCK = 512
ATTN_HEADS_PER_GROUP = 8

LANES = 128
SUBLANES = 8
VMEM_LIMIT_BYTES = 56 * 1024 * 1024

C_Q, C_K, C_V, C_GG, C_LX, C_LG = 0, 128, 256, 512, 768, 1024
C_DQ, C_DKV, C_TAIL = 1280, 1536, 1664
W_IN_COLS = 1792
C_LOGF = 1280
PROJ_COLS = 1408
QK_WIDTH = 256


def _cparams(sem):
    return pltpu.CompilerParams(dimension_semantics=sem, vmem_limit_bytes=VMEM_LIMIT_BYTES)


def _layer_norm_rows(y, g, b):
    mu = jnp.mean(y, axis=-1, keepdims=True)
    yc = y - mu
    var = jnp.mean(yc * yc, axis=-1, keepdims=True)
    return yc * lax.rsqrt(var + LN_EPS) * g + b


def _rms_rows(y, g):
    return y * lax.rsqrt(jnp.mean(y * y, axis=-1, keepdims=True) + RMS_EPS) * g


def _ada_kernel(c_ref, w_ref, b_ref, o_ref):
    c = c_ref[...]
    s = (c * jax.nn.sigmoid(c)).astype(BF16)
    o_ref[0] = jnp.dot(s, w_ref[0].astype(BF16), preferred_element_type=F32) + b_ref[0]


def _ada_all(c_all, w_ada, b_ada):
    depth, d, n = w_ada.shape
    bt = c_all.shape[0]
    tn = 1536
    return pl.pallas_call(
        _ada_kernel,
        grid=(depth, n // tn),
        in_specs=[pl.BlockSpec((bt, d), lambda l, j: (0, 0)),
                  pl.BlockSpec((1, d, tn), lambda l, j: (l, 0, j)),
                  pl.BlockSpec((1, 1, tn), lambda l, j: (l, 0, j))],
        out_specs=pl.BlockSpec((1, bt, tn), lambda l, j: (l, 0, j)),
        out_shape=jax.ShapeDtypeStruct((depth, bt, n), F32),
        compiler_params=_cparams(("parallel", "parallel")),
        name="ada_mod",
    )(c_all, w_ada, b_ada.reshape(depth, 1, n))


def _ln_kernel(x_ref, g_ref, b_ref, o_ref):
    o_ref[...] = _layer_norm_rows(x_ref[...], g_ref[...], b_ref[...])


def _ln_in(x2d, g, b):
    n, d = x2d.shape
    tm = min(1024, n)
    return pl.pallas_call(
        _ln_kernel,
        grid=(n // tm,),
        in_specs=[pl.BlockSpec((tm, d), lambda i: (i, 0)),
                  pl.BlockSpec((1, d), lambda i: (0, 0)),
                  pl.BlockSpec((1, d), lambda i: (0, 0))],
        out_specs=pl.BlockSpec((tm, d), lambda i: (i, 0)),
        out_shape=jax.ShapeDtypeStruct((n, d), F32),
        compiler_params=_cparams(("parallel",)),
        name="ln_in",
    )(x2d, g.reshape(1, d), b.reshape(1, d))


def _fold_kernel(uq_ref, uk_ref, uv_ref, wo_ref, qlat_ref, fold_ref):
    a = uq_ref[0, 0]
    b = uk_ref[0, 0]
    ql = lax.dot_general(a, b, (((1,), (1,)), ((), ())), precision=HIGHEST,
                         preferred_element_type=F32)
    qlat_ref[0] = (ql * QUERY_SCALE).astype(BF16)
    fd = jnp.dot(uv_ref[0, 0], wo_ref[0], precision=HIGHEST, preferred_element_type=F32)
    fold_ref[0] = fd.astype(BF16)


def _fold_weights(mla_w_uq, mla_w_uk, mla_w_uv, w_out):
    depth = mla_w_uq.shape[0]
    uq_n = jnp.transpose(mla_w_uq[..., :MLA_NOPE], (0, 2, 1, 3))
    uk_t = jnp.transpose(mla_w_uk, (0, 2, 1, 3))
    uv_t = jnp.transpose(mla_w_uv, (0, 2, 1, 3))
    mla_row0 = (GLA_WIDTH + LRU_WIDTH) // MLA_V
    return pl.pallas_call(
        _fold_kernel,
        grid=(depth, MLA_HEADS),
        in_specs=[pl.BlockSpec((1, 1, MLA_Q_LORA, MLA_NOPE), lambda l, h: (l, h, 0, 0)),
                  pl.BlockSpec((1, 1, MLA_KV_LORA, MLA_NOPE), lambda l, h: (l, h, 0, 0)),
                  pl.BlockSpec((1, 1, MLA_KV_LORA, MLA_V), lambda l, h: (l, h, 0, 0)),
                  pl.BlockSpec((1, MLA_V, D_MODEL), lambda l, h: (l, mla_row0 + h, 0))],
        out_specs=[pl.BlockSpec((1, MLA_Q_LORA, MLA_KV_LORA), lambda l, h: (l, 0, h)),
                   pl.BlockSpec((1, MLA_KV_LORA, D_MODEL), lambda l, h: (l, h, 0))],
        out_shape=[jax.ShapeDtypeStruct((depth, MLA_Q_LORA, MLA_HEADS * MLA_KV_LORA), BF16),
                   jax.ShapeDtypeStruct((depth, MLA_HEADS * MLA_KV_LORA, D_MODEL), BF16)],
        compiler_params=_cparams(("parallel", "parallel")),
        name="fold_weights",
    )(uq_n, uk_t, uv_t, w_out)


def _inproj_kernel(x_ref, ada_ref, w_ref, wg_ref, bg_ref, qn_ref, kvn_ref, wq2_ref, cs_ref,
                   proj_ref, qcat_ref, ckv_ref, kr_ref, kcat_ref):
    bb, tt, d = x_ref.shape
    m = bb * tt
    x = x_ref[...]
    shift = ada_ref[:, 0:1, :]
    scale = ada_ref[:, 1:2, :]
    h = (x * (1.0 + scale) + shift).reshape(m, d).astype(BF16)
    p = jnp.dot(h, w_ref[...], preferred_element_type=F32)
    proj_ref[:, 0:C_DQ] = p[:, 0:C_DQ]

    tail = p[:, C_TAIL:C_TAIL + LANES]
    z = jnp.dot(tail.astype(BF16), wg_ref[...], preferred_element_type=F32) + bg_ref[...]
    proj_ref[:, C_LOGF:C_LOGF + LANES] = jax.nn.log_sigmoid(z) / GLA_GATE_TAU

    cs = cs_ref[...]
    lane = lax.broadcasted_iota(jnp.int32, (m, LANES), 1)
    rope_lanes = lane < MLA_ROPE

    def rope(block):
        r = block * cs
        r = r + pltpu.roll(r, LANES - MLA_ROPE, 1)
        return jnp.where(rope_lanes, r, 0.0)

    kr = rope(tail)
    kr_ref[...] = kr[:, 0:MLA_ROPE]
    ckv = _rms_rows(p[:, C_DKV:C_DKV + MLA_KV_LORA], kvn_ref[...])
    ckv_ref[...] = ckv
    kcat_ref[...] = jnp.concatenate([ckv.astype(BF16), kr.astype(BF16)], axis=1)

    dqn = _rms_rows(p[:, C_DQ:C_DQ + MLA_Q_LORA], qn_ref[...]).astype(BF16)
    q2 = jnp.dot(dqn, wq2_ref[...], preferred_element_type=F32)
    for hh in range(MLA_HEADS):
        lat = q2[:, hh * LANES:(hh + 1) * LANES]
        rp = rope(q2[:, (MLA_HEADS + hh) * LANES:(MLA_HEADS + hh + 1) * LANES])
        qc = jnp.concatenate([lat, rp], axis=1)
        qcat_ref[:, hh] = qc.reshape(bb, tt, QK_WIDTH).astype(qcat_ref.dtype)


def _inproj(x, ada, w_in_p, wg_p, bg, qn, kvn, wq2, cs, bb, tt):
    b, t, d = x.shape
    n = b * t
    m = bb * tt
    grid = (b // bb, t // tt)
    nt = t // tt
    const = lambda i, j: (0, 0)
    row = lambda i, j: (i * nt + j, 0)
    return pl.pallas_call(
        _inproj_kernel,
        grid=grid,
        in_specs=[pl.BlockSpec((bb, tt, d), lambda i, j: (i, j, 0)),
                  pl.BlockSpec((bb, 6, d), lambda i, j: (i, 0, 0)),
                  pl.BlockSpec((d, W_IN_COLS), const),
                  pl.BlockSpec((LANES, LANES), const),
                  pl.BlockSpec((1, LANES), const),
                  pl.BlockSpec((1, MLA_Q_LORA), const),
                  pl.BlockSpec((1, MLA_KV_LORA), const),
                  pl.BlockSpec((MLA_Q_LORA, 2 * MLA_HEADS * LANES), const),
                  pl.BlockSpec((m, LANES), lambda i, j: (j, 0))],
        out_specs=[pl.BlockSpec((m, PROJ_COLS), row),
                   pl.BlockSpec((bb, MLA_HEADS, tt, QK_WIDTH), lambda i, j: (i, 0, j, 0)),
                   pl.BlockSpec((m, MLA_KV_LORA), row),
                   pl.BlockSpec((m, MLA_ROPE), row),
                   pl.BlockSpec((m, QK_WIDTH), row)],
        out_shape=[jax.ShapeDtypeStruct((n, PROJ_COLS), F32),
                   jax.ShapeDtypeStruct((b, MLA_HEADS, t, QK_WIDTH), BF16 if tt % 16 == 0 else F32),
                   jax.ShapeDtypeStruct((n, MLA_KV_LORA), F32),
                   jax.ShapeDtypeStruct((n, MLA_ROPE), F32),
                   jax.ShapeDtypeStruct((n, QK_WIDTH), BF16)],
        compiler_params=_cparams(("parallel", "parallel")),
        name="in_proj",
    )(x, ada, w_in_p, wg_p, bg, qn, kvn, wq2, cs)


def _gla_kernel(q_ref, k_ref, v_ref, gg_ref, g_ref, s0_ref, ng_ref, ltri_ref, ind_ref,
                msk_ref, seg_ref, o_ref, sT_ref, st_sc, b_sc, o_sc, *, nb, t, c):
    n_chunks = t // c
    groups = c // SUBLANES
    ltri = ltri_ref[...]
    ind = ind_ref[...]
    msk = msk_ref[...]
    row_iota = lax.broadcasted_iota(jnp.int32, (c, LANES), 0)
    for j in range(nb):
        st_sc[j] = s0_ref[j]

    def chunk(ci, carry):
        for j in range(nb):
            r0 = pl.multiple_of(j * t + ci * c, c)
            g = g_ref[pl.ds(r0, c), :]
            b = jnp.dot(ltri, g, precision=HIGHEST, preferred_element_type=F32)
            b_sc[j] = b
            q = q_ref[pl.ds(r0, c), :]
            k = k_ref[pl.ds(r0, c), :]
            v = v_ref[pl.ds(r0, c), :]
            blast = b_sc[j, pl.ds(c - 1, 1), :]
            qe = q * jnp.exp(b)
            ke = k * jnp.exp(blast - b)
            st = st_sc[j]
            o_inter = lax.dot_general(qe.astype(BF16), st.astype(BF16), (((1,), (1,)), ((), ())),
                                      preferred_element_type=F32)
            ut = lax.dot_general(v.astype(BF16), ke.astype(BF16), (((0,), (0,)), ((), ())),
                                 preferred_element_type=F32)
            st_sc[j] = st * jnp.exp(blast) + ut * msk

            o_blk = [None] * groups
            for g0 in range(groups):
                lo = g0 * SUBLANES
                pieces = []
                for s in range(lo, lo + SUBLANES):
                    bs = b_sc[j, pl.ds(s, 1), :]
                    ks = k_ref[pl.ds(r0 + s, 1), :]
                    e = jnp.exp(b[lo:, :] - bs)
                    head = jnp.where(row_iota[lo:lo + SUBLANES, :] >= s, e[0:SUBLANES, :], 0.0)
                    e = head if c - lo == SUBLANES else jnp.concatenate([head, e[SUBLANES:, :]], axis=0)
                    pieces.append(e * q[lo:, :] * ks)
                w = jnp.concatenate(pieces, axis=0).astype(BF16)
                a = jnp.dot(w, ind, preferred_element_type=F32)
                rows = c - lo
                for idx in range(SUBLANES):
                    vs = v_ref[pl.ds(r0 + lo + idx, 1), :]
                    for rb in range(g0, groups):
                        piece = a[idx * rows + (rb - g0) * SUBLANES:
                                  idx * rows + (rb - g0 + 1) * SUBLANES, :] * vs
                        o_blk[rb] = piece if o_blk[rb] is None else o_blk[rb] + piece
            o_intra = jnp.concatenate(o_blk, axis=0) if groups > 1 else o_blk[0]
            o_sc[pl.ds(r0, c), :] = o_inter + o_intra
        return carry

    lax.fori_loop(0, n_chunks, chunk, 0)

    for j in range(nb):
        sT_ref[j] = st_sc[j]

    rt = min(256, nb * t)
    ng = ng_ref[...]
    seg = seg_ref[...]

    def epi(i, carry):
        r0 = pl.multiple_of(i * rt, rt)
        o = o_sc[pl.ds(r0, rt), :]
        ms = jnp.dot(o * o, seg, precision=HIGHEST, preferred_element_type=F32)
        gg = gg_ref[pl.ds(r0, rt), :]
        o_ref[pl.ds(r0, rt), :] = o * lax.rsqrt(ms + RMS_EPS) * ng * (gg * jax.nn.sigmoid(gg))
        return carry

    lax.fori_loop(0, (nb * t) // rt, epi, 0)


def _gla(proj, s0T, ng, b, t, nb):
    n = b * t
    c = min(GLA_CHUNK, t)
    rows = nb * t
    ltri = jnp.asarray(np.tril(np.ones((c, c), np.float32)))
    hk = np.arange(GLA_HEADS * GLA_DK) // GLA_DK
    hv = np.arange(GLA_WIDTH) // GLA_DV
    ind = jnp.asarray((hk[:, None] == hv[None, :]).astype(np.float32)).astype(BF16)
    msk = jnp.asarray((hv[:, None] == hk[None, :]).astype(np.float32))
    seg = jnp.asarray((hv[:, None] == hv[None, :]).astype(np.float32) / GLA_DV)
    const = lambda i: (0, 0)
    kern = functools.partial(_gla_kernel, nb=nb, t=t, c=c)
    return pl.pallas_call(
        kern,
        grid=(b // nb,),
        in_specs=[pl.BlockSpec((rows, LANES), lambda i: (i, C_Q // LANES)),
                  pl.BlockSpec((rows, LANES), lambda i: (i, C_K // LANES)),
                  pl.BlockSpec((rows, GLA_WIDTH), lambda i: (i, C_V // GLA_WIDTH)),
                  pl.BlockSpec((rows, GLA_WIDTH), lambda i: (i, C_GG // GLA_WIDTH)),
                  pl.BlockSpec((rows, LANES), lambda i: (i, C_LOGF // LANES)),
                  pl.BlockSpec((nb, GLA_WIDTH, LANES), lambda i: (i, 0, 0)),
                  pl.BlockSpec((1, GLA_WIDTH), const),
                  pl.BlockSpec((c, c), const),
                  pl.BlockSpec((LANES, GLA_WIDTH), const),
                  pl.BlockSpec((GLA_WIDTH, LANES), const),
                  pl.BlockSpec((GLA_WIDTH, GLA_WIDTH), const)],
        out_specs=[pl.BlockSpec((rows, GLA_WIDTH), lambda i: (i, 0)),
                   pl.BlockSpec((nb, GLA_WIDTH, LANES), lambda i: (i, 0, 0))],
        out_shape=[jax.ShapeDtypeStruct((n, GLA_WIDTH), F32),
                   jax.ShapeDtypeStruct((b, GLA_WIDTH, LANES), F32)],
        scratch_shapes=[pltpu.VMEM((nb, GLA_WIDTH, LANES), F32),
                        pltpu.VMEM((nb, c, LANES), F32),
                        pltpu.VMEM((rows, GLA_WIDTH), F32)],
        compiler_params=_cparams(("parallel",)),
        name="gla",
    )(proj, proj, proj, proj, proj, s0T, ng, ltri, ind, msk, seg)


def _lru_kernel(lx_ref, lg_ref, cbuf_ref, h0_ref, cw_ref, cb_ref, wax_ref, bax_ref, sp_ref,
                o_ref, hn_ref, cn_ref, xp_sc, a_sc, u_sc, *, nb, t):
    pad = SUBLANES
    hist = CONV_WIDTH - 1
    row8 = lax.broadcasted_iota(jnp.int32, (SUBLANES, LRU_WIDTH), 0)
    rows_per_iter = min(32, t)
    sub = rows_per_iter // SUBLANES

    def scan_block(a, u):
        for dd in (1, 2, 4):
            a_s = jnp.where(row8 >= dd, pltpu.roll(a, dd, 0), 1.0)
            u_s = jnp.where(row8 >= dd, pltpu.roll(u, dd, 0), 0.0)
            u = a * u_s + u
            a = a * a_s
        return a, u

    for j in range(nb):
        base = j * t
        xp_sc[pl.ds(0, pad), :] = jnp.zeros((pad, LRU_WIDTH), F32)
        xp_sc[pl.ds(pad - hist, hist), :] = cbuf_ref[j]
        xp_sc[pl.ds(pad, t), :] = lx_ref[pl.ds(base, t), :]
        xc = cb_ref[...]
        for kk in range(CONV_WIDTH):
            xc = xc + xp_sc[pl.ds(pad - hist + kk, t), :] * cw_ref[pl.ds(kk, 1), :]
        cn_ref[j] = xp_sc[pl.ds(pad + t - hist, hist), :]

        ax = jnp.dot(xc.astype(BF16), wax_ref[...], preferred_element_type=F32) + bax_ref[...]
        r = jax.nn.sigmoid(ax[:, 0:LRU_WIDTH])
        ig = jax.nn.sigmoid(ax[:, LRU_WIDTH:2 * LRU_WIDTH])
        log_a = -LRU_C * r * sp_ref[...]
        a = jnp.exp(log_a)
        u = jnp.sqrt((a * a + 1.0) * jnp.tanh(-log_a)) * (ig * xc)
        a_sc[...] = a
        u_sc[...] = u
        u_sc[pl.ds(0, 1), :] = u[0:1, :] + a[0:1, :] * h0_ref[j]

        def step(i, hprev):
            r0 = pl.multiple_of(i * rows_per_iter, rows_per_iter)
            scans = []
            for sb in range(sub):
                ab = a_sc[pl.ds(r0 + sb * SUBLANES, SUBLANES), :]
                ub = u_sc[pl.ds(r0 + sb * SUBLANES, SUBLANES), :]
                scans.append(scan_block(ab, ub))
            for sb in range(sub):
                ac, uc = scans[sb]
                hb = ac * hprev + uc
                u_sc[pl.ds(r0 + sb * SUBLANES, SUBLANES), :] = hb
                hprev = hb[SUBLANES - 1:SUBLANES, :]
            return hprev

        hlast = lax.fori_loop(0, t // rows_per_iter, step, jnp.zeros((1, LRU_WIDTH), F32))
        hn_ref[j] = hlast
        lg = lg_ref[pl.ds(base, t), :]
        o_ref[pl.ds(base, t), :] = jax.nn.gelu(lg, approximate=True) * u_sc[...]


def _lru(proj, cbuf, h0, cw, cb, wax, bax, sp, b, t, nb):
    n = b * t
    rows = nb * t
    const = lambda i: (0, 0)
    kern = functools.partial(_lru_kernel, nb=nb, t=t)
    return pl.pallas_call(
        kern,
        grid=(b // nb,),
        in_specs=[pl.BlockSpec((rows, LRU_WIDTH), lambda i: (i, C_LX // LRU_WIDTH)),
                  pl.BlockSpec((rows, LRU_WIDTH), lambda i: (i, C_LG // LRU_WIDTH)),
                  pl.BlockSpec((nb, CONV_WIDTH - 1, LRU_WIDTH), lambda i: (i, 0, 0)),
                  pl.BlockSpec((nb, 1, LRU_WIDTH), lambda i: (i, 0, 0)),
                  pl.BlockSpec((CONV_WIDTH, LRU_WIDTH), const),
                  pl.BlockSpec((1, LRU_WIDTH), const),
                  pl.BlockSpec((LRU_WIDTH, 2 * LRU_WIDTH), const),
                  pl.BlockSpec((1, 2 * LRU_WIDTH), const),
                  pl.BlockSpec((1, LRU_WIDTH), const)],
        out_specs=[pl.BlockSpec((rows, LRU_WIDTH), lambda i: (i, 0)),
                   pl.BlockSpec((nb, 1, LRU_WIDTH), lambda i: (i, 0, 0)),
                   pl.BlockSpec((nb, CONV_WIDTH - 1, LRU_WIDTH), lambda i: (i, 0, 0))],
        out_shape=[jax.ShapeDtypeStruct((n, LRU_WIDTH), F32),
                   jax.ShapeDtypeStruct((b, 1, LRU_WIDTH), F32),
                   jax.ShapeDtypeStruct((b, CONV_WIDTH - 1, LRU_WIDTH), F32)],
        scratch_shapes=[pltpu.VMEM((t + SUBLANES, LRU_WIDTH), F32),
                        pltpu.VMEM((t, LRU_WIDTH), F32),
                        pltpu.VMEM((t, LRU_WIDTH), F32)],
        compiler_params=_cparams(("parallel",)),
        name="rg_lru",
    )(proj, proj, cbuf, h0, cw, cb, wax, bax, sp)


def _softmax_step(s, m_ref, l_ref, acc_ref, rows, v):
    reps = s.shape[1] // LANES
    m_prev = m_ref[rows, :]
    m_next = jnp.maximum(m_prev, jnp.max(s, axis=1, keepdims=True))
    alpha = jnp.exp2(m_prev - m_next)
    m_wide = m_next if reps == 1 else jnp.concatenate([m_next] * reps, axis=1)
    p = jnp.exp2(s - m_wide)
    l_ref[rows, :] = alpha * l_ref[rows, :] + jnp.sum(p, axis=1, keepdims=True)
    acc_ref[rows, :] = alpha * acc_ref[rows, :] + jnp.dot(p.astype(BF16), v,
                                                          preferred_element_type=F32)
    m_ref[rows, :] = m_next


def _attn_kernel(q_ref, k_ref, o_ref, m_sc, l_sc, acc_sc, *, tq, rg):
    i = pl.program_id(1)
    r = MLA_HEADS * tq
    m_sc[...] = jnp.full((r, LANES), -jnp.inf, F32)
    l_sc[...] = jnp.zeros((r, LANES), F32)
    acc_sc[...] = jnp.zeros((r, MLA_KV_LORA), F32)

    def update(j, masked):
        kb = k_ref[0, pl.ds(pl.multiple_of(j * tq, tq), tq), :]
        v = kb[:, 0:MLA_KV_LORA]
        for g in range(r // rg):
            rows = pl.ds(g * rg, rg)
            hpg = rg // tq
            q = q_ref[0, g * hpg:(g + 1) * hpg].reshape(rg, QK_WIDTH)
            s = lax.dot_general(q, kb, (((1,), (1,)), ((), ())), preferred_element_type=F32)
            if masked:
                qpos = lax.broadcasted_iota(jnp.int32, (rg, tq), 0) & (tq - 1)
                kpos = lax.broadcasted_iota(jnp.int32, (rg, tq), 1)
                s = jnp.where(kpos <= qpos, s, -jnp.inf)
            _softmax_step(s, m_sc, l_sc, acc_sc, rows, v)

    def body(j, carry):
        update(j, False)
        return carry

    lax.fori_loop(0, i, body, 0)
    update(i, True)
    for hh in range(MLA_HEADS):
        rows = pl.ds(hh * tq, tq)
        o_ref[0, :, hh * MLA_KV_LORA:(hh + 1) * MLA_KV_LORA] = \
            (acc_sc[rows, :] / l_sc[rows, :]).astype(BF16)


def _attn_prompt(qcat, kcat, b, t):
    tq = min(ATTN_BLOCK, t)
    r = MLA_HEADS * tq
    kern = functools.partial(_attn_kernel, tq=tq, rg=ATTN_HEADS_PER_GROUP * tq)
    return pl.pallas_call(
        kern,
        grid=(b, t // tq),
        in_specs=[pl.BlockSpec((1, MLA_HEADS, tq, QK_WIDTH), lambda bi, i: (bi, 0, i, 0)),
                  pl.BlockSpec((1, t, QK_WIDTH), lambda bi, i: (bi, 0, 0))],
        out_specs=pl.BlockSpec((1, tq, MLA_HEADS * MLA_KV_LORA), lambda bi, i: (bi, i, 0)),
        out_shape=jax.ShapeDtypeStruct((b, t, MLA_HEADS * MLA_KV_LORA), BF16),
        scratch_shapes=[pltpu.VMEM((r, LANES), F32), pltpu.VMEM((r, LANES), F32),
                        pltpu.VMEM((r, MLA_KV_LORA), F32)],
        compiler_params=_cparams(("parallel", "parallel")),
        name="attn_prompt",
    )(qcat, kcat.reshape(b, t, QK_WIDTH))


PAGE_SLOTS = 3


def _attn_paged_kernel(pt_ref, q_ref, cnew_ref, rnew_ref, ckv_hbm, krt_hbm, o_ref,
                       kc_buf, kr_buf, sem, m_sc, l_sc, acc_sc, *, pages, ts, layer, steps, n_steps):
    j = pl.program_id(1)
    step = pl.program_id(0) * steps + j
    r = MLA_HEADS * ts

    def page_copy(src_page, slot, i):
        return (pltpu.make_async_copy(ckv_hbm.at[layer, src_page],
                                      kc_buf.at[slot, pl.ds(i * PAGE_SIZE, PAGE_SIZE), :],
                                      sem.at[slot, 0]),
                pltpu.make_async_copy(krt_hbm.at[layer, src_page],
                                      kr_buf.at[slot, :, pl.ds(i * PAGE_SIZE, PAGE_SIZE)],
                                      sem.at[slot, 1]))

    def start_step(step_id, slot):
        seq = step_id // steps
        first = (step_id - seq * steps) * pages
        for i in range(pages):
            for cp in page_copy(pt_ref[seq, first + i], slot, i):
                cp.start()

    @pl.when(step == 0)
    def _():
        for ahead in range(min(PAGE_SLOTS - 1, n_steps)):
            start_step(ahead, ahead)

    @pl.when(j == 0)
    def _():
        m_sc[...] = jnp.full((r, LANES), -jnp.inf, F32)
        l_sc[...] = jnp.zeros((r, LANES), F32)
        acc_sc[...] = jnp.zeros((r, MLA_KV_LORA), F32)

    slot = lax.rem(step, PAGE_SLOTS)
    for i in range(pages):
        for cp in page_copy(0, slot, i):
            cp.wait()

    q = q_ref[0].reshape(r, QK_WIDTH).astype(BF16)
    q_lat = q[:, 0:MLA_KV_LORA]
    q_rope = q[:, MLA_KV_LORA:MLA_KV_LORA + MLA_ROPE]
    all_rows = pl.ds(0, r)
    kc = kc_buf[slot].astype(BF16)
    krt = kr_buf[slot].astype(BF16)
    s = (lax.dot_general(q_lat, kc, (((1,), (1,)), ((), ())), preferred_element_type=F32)
         + jnp.dot(q_rope, krt, preferred_element_type=F32))
    _softmax_step(s, m_sc, l_sc, acc_sc, all_rows, kc)

    nxt = step + (PAGE_SLOTS - 1)

    @pl.when(nxt < n_steps)
    def _():
        start_step(nxt, lax.rem(nxt, PAGE_SLOTS))

    @pl.when(j == steps - 1)
    def _():
        pad = jnp.zeros((LANES - ts, MLA_KV_LORA), F32)
        kc_new = jnp.concatenate([cnew_ref[...], pad], axis=0).astype(BF16)
        kr_new = jnp.concatenate([rnew_ref[...], pad[:, 0:MLA_ROPE]], axis=0).astype(BF16)
        s_new = (lax.dot_general(q_lat, kc_new, (((1,), (1,)), ((), ())), preferred_element_type=F32)
                 + lax.dot_general(q_rope, kr_new, (((1,), (1,)), ((), ())),
                                   preferred_element_type=F32))
        qpos = lax.broadcasted_iota(jnp.int32, (r, LANES), 0) & (ts - 1)
        kpos = lax.broadcasted_iota(jnp.int32, (r, LANES), 1)
        s_new = jnp.where(kpos <= qpos, s_new, -jnp.inf)
        _softmax_step(s_new, m_sc, l_sc, acc_sc, all_rows, kc_new)
        out = acc_sc[...] / l_sc[...]
        for hh in range(MLA_HEADS):
            o_ref[0, :, hh * MLA_KV_LORA:(hh + 1) * MLA_KV_LORA] = out[hh * ts:(hh + 1) * ts, :]


def _attn_paged(qcat, ckv_new, kr_new, cache_ckv, cache_krope_t, page_table, layer, pages):
    b, _, ts, _ = qcat.shape
    n_pages = page_table.shape[1]
    steps = n_pages // pages
    kern = functools.partial(_attn_paged_kernel, pages=pages, ts=ts, layer=layer, steps=steps,
                             n_steps=b * steps)
    r = MLA_HEADS * ts
    grid_spec = pltpu.PrefetchScalarGridSpec(
        num_scalar_prefetch=1,
        grid=(b, steps),
        in_specs=[pl.BlockSpec((1, MLA_HEADS, ts, QK_WIDTH), lambda bi, j, pt: (bi, 0, 0, 0)),
                  pl.BlockSpec((ts, MLA_KV_LORA), lambda bi, j, pt: (bi, 0)),
                  pl.BlockSpec((ts, MLA_ROPE), lambda bi, j, pt: (bi, 0)),
                  pl.BlockSpec(memory_space=pl.ANY),
                  pl.BlockSpec(memory_space=pl.ANY)],
        out_specs=pl.BlockSpec((1, ts, MLA_HEADS * MLA_KV_LORA), lambda bi, j, pt: (bi, 0, 0)),
        scratch_shapes=[pltpu.VMEM((PAGE_SLOTS, pages * PAGE_SIZE, MLA_KV_LORA), F32),
                        pltpu.VMEM((PAGE_SLOTS, MLA_ROPE, pages * PAGE_SIZE), F32),
                        pltpu.SemaphoreType.DMA((PAGE_SLOTS, 2)),
                        pltpu.VMEM((r, LANES), F32), pltpu.VMEM((r, LANES), F32),
                        pltpu.VMEM((r, MLA_KV_LORA), F32)])
    return pl.pallas_call(
        kern,
        grid_spec=grid_spec,
        out_shape=jax.ShapeDtypeStruct((b, ts, MLA_HEADS * MLA_KV_LORA), F32),
        compiler_params=_cparams(("arbitrary", "arbitrary")),
        name="attn_paged",
    )(page_table, qcat, ckv_new, kr_new, cache_ckv, cache_krope_t)


FFN_CHUNK = 256


def _mix_ffn_kernel(x_ref, ada_ref, og_ref, ol_ref, oa_ref, wgl_ref, wf_ref, g1_ref, b1_ref,
                    wgu_ref, wd_ref, g2_ref, b2_ref, o_ref, *, alpha):
    bb, tt, d = x_ref.shape
    m = bb * tt
    gate1 = ada_ref[:, 2:3, :]
    shift2 = ada_ref[:, 3:4, :]
    scale2 = ada_ref[:, 4:5, :]
    gate2 = ada_ref[:, 5:6, :]
    ogl = jnp.concatenate([og_ref[...], ol_ref[...]], axis=1).astype(BF16)
    mix = (jnp.dot(ogl, wgl_ref[...], preferred_element_type=F32)
           + jnp.dot(oa_ref[...].astype(BF16), wf_ref[...], preferred_element_type=F32))
    x1 = _layer_norm_rows(alpha * x_ref[...] + gate1 * mix.reshape(bb, tt, d), g1_ref[...], b1_ref[...])

    h2 = (x1 * (1.0 + scale2) + shift2).reshape(m, d).astype(BF16)
    acc = jnp.zeros((m, d), F32)
    for ci in range(D_FF // FFN_CHUNK):
        lo = ci * FFN_CHUNK
        gf = jnp.dot(h2, wgu_ref[:, lo:lo + FFN_CHUNK], preferred_element_type=F32)
        uf = jnp.dot(h2, wgu_ref[:, D_FF + lo:D_FF + lo + FFN_CHUNK], preferred_element_type=F32)
        act = (gf * jax.nn.sigmoid(gf) * uf).astype(BF16)
        acc = acc + jnp.dot(act, wd_ref[lo:lo + FFN_CHUNK, :], preferred_element_type=F32)
    y = alpha * x1 + gate2 * acc.reshape(bb, tt, d)
    o_ref[...] = _layer_norm_rows(y, g2_ref[...], b2_ref[...])


def _mix_ffn(x, ada, o_gla, o_lru, o_lat, wgl, wfold, ln1, wgu, wd, ln2, bb, tt, alpha):
    b, t, d = x.shape
    nt = t // tt
    m = bb * tt
    row = lambda i, j: (i * nt + j, 0)

    def resident(shape):
        return pl.BlockSpec(shape, lambda i, j: (0, 0), pipeline_mode=pl.Buffered(1))

    kern = functools.partial(_mix_ffn_kernel, alpha=alpha)
    return pl.pallas_call(
        kern,
        grid=(b // bb, nt),
        in_specs=[pl.BlockSpec((bb, tt, d), lambda i, j: (i, j, 0)),
                  pl.BlockSpec((bb, 6, d), lambda i, j: (i, 0, 0)),
                  pl.BlockSpec((m, GLA_WIDTH), row),
                  pl.BlockSpec((m, LRU_WIDTH), row),
                  pl.BlockSpec((m, MLA_HEADS * MLA_KV_LORA), row),
                  resident((GLA_WIDTH + LRU_WIDTH, d)),
                  resident((MLA_HEADS * MLA_KV_LORA, d)),
                  resident((1, d)), resident((1, d)),
                  resident((d, 2 * D_FF)),
                  resident((D_FF, d)),
                  resident((1, d)), resident((1, d))],
        out_specs=pl.BlockSpec((bb, tt, d), lambda i, j: (i, j, 0)),
        out_shape=jax.ShapeDtypeStruct((b, t, d), F32),
        compiler_params=_cparams(("parallel", "parallel")),
        name="mix_ffn",
    )(x, ada, o_gla, o_lru, o_lat, wgl, wfold, ln1[0], ln1[1], wgu, wd, ln2[0], ln2[1])


def _rotate_half_cols(w):
    half = MLA_ROPE // 2
    return jnp.concatenate([-w[..., half:], w[..., :half]], axis=-1)


def _prep_layer_weights(l, w_in, gla_w_gate, gla_b_gate, gla_norm_g, lru_conv_w, lru_conv_b, lru_w_a,
                        lru_b_a, lru_w_x, lru_b_x, lru_lambda, mla_q_norm_g, mla_w_uq, mla_kv_norm_g,
                        w_qlat, w_out, ffn_w_gu, ffn_w_down):
    w = w_in[l]
    d = w.shape[0]
    o = np.cumsum([0, 128, 128, 256, 256, 16, 256, 256, 256, 128, 32])
    gq, gk, gv, gg, glr, lx, lgt, dq, dkv, kr = [w[:, o[i]:o[i + 1]] for i in range(10)]
    tail = jnp.concatenate([kr, _rotate_half_cols(kr), glr,
                            jnp.zeros((d, LANES - 2 * MLA_ROPE - GLA_LOWRANK), F32)], axis=1)
    w_in_p = jnp.concatenate([gq * (GLA_DK ** -0.5), gk, gv, gg, lx, lgt, dq, dkv, tail],
                             axis=1).astype(BF16)
    wg_p = jnp.pad(gla_w_gate[l], ((2 * MLA_ROPE, LANES - 2 * MLA_ROPE - GLA_LOWRANK),
                                   (0, 0))).astype(BF16)
    bg = gla_b_gate[l].reshape(1, LANES)
    ng = jnp.tile(gla_norm_g[l], GLA_HEADS).reshape(1, GLA_WIDTH)

    rope_w = mla_w_uq[l][:, :, MLA_NOPE:] * QUERY_SCALE
    rope_blk = jnp.concatenate(
        [rope_w, _rotate_half_cols(rope_w),
         jnp.zeros((MLA_Q_LORA, MLA_HEADS, LANES - 2 * MLA_ROPE), F32)], axis=-1)
    wq2 = jnp.concatenate([w_qlat[l], rope_blk.reshape(MLA_Q_LORA, MLA_HEADS * LANES).astype(BF16)],
                          axis=1)

    def block_diag(wb):
        on_diag = jnp.eye(LRU_BLOCKS, dtype=bool)[:, None, :, None]
        return jnp.where(on_diag, wb[:, :, None, :], 0.0).reshape(LRU_WIDTH, LRU_WIDTH)

    wax = jnp.concatenate([block_diag(lru_w_a[l]), block_diag(lru_w_x[l])], axis=1).astype(BF16)
    bax = jnp.concatenate([lru_b_a[l], lru_b_x[l]]).reshape(1, 2 * LRU_WIDTH)
    sp = jax.nn.softplus(-lru_lambda[l].astype(F32)).reshape(1, LRU_WIDTH)
    return dict(
        w_in_p=w_in_p, wg_p=wg_p, bg=bg, ng=ng, wq2=wq2,
        qn=mla_q_norm_g[l].reshape(1, MLA_Q_LORA), kvn=mla_kv_norm_g[l].reshape(1, MLA_KV_LORA),
        cw=lru_conv_w[l], cb=lru_conv_b[l].reshape(1, LRU_WIDTH), wax=wax, bax=bax, sp=sp,
        wgl=w_out[l][:GLA_WIDTH + LRU_WIDTH].astype(BF16),
        wgu=ffn_w_gu[l].astype(BF16), wd=ffn_w_down[l].astype(BF16))


def _rope_table(pos):
    half = MLA_ROPE // 2
    inv_freq = ROPE_THETA ** (-jnp.arange(half, dtype=F32) / half)
    ang = pos.astype(F32)[:, None] * inv_freq[None, :]
    cos, sin = jnp.cos(ang), jnp.sin(ang)
    return jnp.concatenate([cos, cos, sin, sin,
                            jnp.zeros((pos.shape[0], LANES - 2 * MLA_ROPE), F32)], axis=1)


def _state_to_blockdiag_t(s):
    b = s.shape[0]
    on_diag = jnp.eye(GLA_HEADS, dtype=bool)[None, :, None, :, None]
    st = jnp.swapaxes(s, 2, 3)[:, :, :, None, :]
    return jnp.where(on_diag, st, 0.0).reshape(b, GLA_WIDTH, GLA_HEADS * GLA_DK)


def _blockdiag_t_to_state(st):
    return jnp.stack([jnp.swapaxes(st[:, h * GLA_DV:(h + 1) * GLA_DV, h * GLA_DK:(h + 1) * GLA_DK], 1, 2)
                      for h in range(GLA_HEADS)], axis=1)


def _group_layer(x, ada, lw, wfold_l, cs, s0T, h0, cbuf, tiles, alpha, ln1, ln2, attend):
    b, t, _ = x.shape
    bb, tt, nb = tiles
    proj, qcat, ckv_new, kr_new, kcat = _inproj(x, ada, lw["w_in_p"], lw["wg_p"], lw["bg"], lw["qn"],
                                                lw["kvn"], lw["wq2"], cs, bb, tt)
    o_gla, sT = _gla(proj, s0T, lw["ng"], b, t, nb)
    o_lru, h_new, conv_new = _lru(proj, cbuf, h0, lw["cw"], lw["cb"], lw["wax"], lw["bax"], lw["sp"],
                                  b, t, nb)
    o_lat = attend(qcat, kcat, ckv_new, kr_new)
    x2 = _mix_ffn(x, ada, o_gla, o_lru, o_lat.reshape(b * t, -1), lw["wgl"], wfold_l, ln1,
                  lw["wgu"], lw["wd"], ln2, bb, tt, alpha)
    states = (_blockdiag_t_to_state(sT), h_new.reshape(b, LRU_WIDTH), conv_new,
              ckv_new.reshape(b, t, MLA_KV_LORA), kr_new.reshape(b, t, MLA_ROPE))
    return x2, states


def kernel(x_prompt, x_sample, c_prompt, c_sample, state_gla, state_lru, state_conv, cache_ckv, cache_krope, page_table, ln_in_g, ln_in_b, w_ada, b_ada, w_in, gla_w_gate, gla_b_gate, gla_norm_g, lru_conv_w, lru_conv_b, lru_w_a, lru_b_a, lru_w_x, lru_b_x, lru_lambda, mla_q_norm_g, mla_w_uq, mla_kv_norm_g, mla_w_uk, mla_w_uv, w_out, ln1_g, ln1_b, ffn_w_gu, ffn_w_down, ln2_g, ln2_b):
    bp, tp, d = x_prompt.shape
    bs, ts, _ = x_sample.shape
    depth = w_in.shape[0]
    n_pages = page_table.shape[1]
    past_len = n_pages * PAGE_SIZE
    alpha = (2.0 * depth) ** 0.25

    tiles_p = (1, min(512, tp), 2 if bp % 2 == 0 else 1)
    tiles_s = (bs, ts, 4 if bs % 4 == 0 else 1)
    pages_per_step = 32 if n_pages % 32 == 0 else n_pages

    ada = _ada_all(jnp.concatenate([c_prompt, c_sample], axis=0), w_ada, b_ada)
    ada = ada.reshape(depth, bp + bs, 6, d)
    xp = _ln_in(x_prompt.reshape(bp * tp, d), ln_in_g, ln_in_b).reshape(bp, tp, d)
    xs = _ln_in(x_sample.reshape(bs * ts, d), ln_in_g, ln_in_b).reshape(bs, ts, d)
    w_qlat, w_fold = _fold_weights(mla_w_uq, mla_w_uk, mla_w_uv, w_out)

    cache_krope_t = jnp.swapaxes(cache_krope, 2, 3)
    cs_p = _rope_table(jnp.arange(tp, dtype=jnp.int32))
    cs_s = jnp.tile(_rope_table(past_len + jnp.arange(ts, dtype=jnp.int32)), (bs, 1))
    zero_sT = jnp.zeros((bp, GLA_WIDTH, GLA_HEADS * GLA_DK), F32)
    zero_h = jnp.zeros((bp, 1, LRU_WIDTH), F32)
    zero_conv = jnp.zeros((bp, CONV_WIDTH - 1, LRU_WIDTH), F32)

    st_p, st_s = [], []
    for l in range(depth):
        lw = _prep_layer_weights(l, w_in, gla_w_gate, gla_b_gate, gla_norm_g, lru_conv_w, lru_conv_b,
                                 lru_w_a, lru_b_a, lru_w_x, lru_b_x, lru_lambda, mla_q_norm_g, mla_w_uq,
                                 mla_kv_norm_g, w_qlat, w_out, ffn_w_gu, ffn_w_down)
        ln1 = (ln1_g[l].reshape(1, d), ln1_b[l].reshape(1, d))
        ln2 = (ln2_g[l].reshape(1, d), ln2_b[l].reshape(1, d))

        def attend_p(qcat, kcat, ckv_new, kr_new):
            return _attn_prompt(qcat, kcat, bp, tp)

        def attend_s(qcat, kcat, ckv_new, kr_new, l=l):
            return _attn_paged(qcat, ckv_new, kr_new, cache_ckv, cache_krope_t, page_table, l,
                               pages_per_step)

        xp, sp = _group_layer(xp, ada[l, :bp], lw, w_fold[l], cs_p, zero_sT, zero_h, zero_conv,
                              tiles_p, alpha, ln1, ln2, attend_p)
        xs, ss = _group_layer(xs, ada[l, bp:], lw, w_fold[l], cs_s, _state_to_blockdiag_t(state_gla[l]),
                              state_lru[l].reshape(bs, 1, LRU_WIDTH), state_conv[l],
                              tiles_s, alpha, ln1, ln2, attend_s)
        st_p.append(sp)
        st_s.append(ss)

    def stk(outs, j):
        return jnp.stack([o[j] for o in outs])

    return (xp, xs, stk(st_p, 0), stk(st_s, 0), stk(st_p, 1), stk(st_s, 1), stk(st_p, 2), stk(st_s, 2),
            stk(st_p, 3), stk(st_s, 3), stk(st_p, 4), stk(st_s, 4))
```

```python
import functools
import math

import numpy as np
import jax
import jax.numpy as jnp
from jax import lax
from jax.experimental import pallas as pl
from jax.experimental.pallas import tpu as pltpu

F32 = jnp.float32
BF16 = jnp.bfloat16
HIGHEST = lax.Precision.HIGHEST

D_MODEL = 1024
PAGE_SIZE = 128
GLA_HEADS = 4
GLA_DK = 32
GLA_DV = 64
GLA_WIDTH = GLA_HEADS * GLA_DV
GLA_LOWRANK = 16
GLA_GATE_TAU = 16.0
GLA_CHUNK = 32
LRU_WIDTH = 256
LRU_BLOCKS = 4
LRU_BLOCK_W = LRU_WIDTH // LRU_BLOCKS
CONV_WIDTH = 4
LRU_C = 8.0
MLA_HEADS = 8
MLA_NOPE = 64
MLA_ROPE = 32
MLA_V = 64
MLA_Q_LORA = 256
MLA_KV_LORA = 128
ROPE_THETA = 10000.0
D_FF = 2816
LN_EPS = 1e-5
RMS_EPS = 1e-6
ATTN_SCALE = (MLA_NOPE + MLA_ROPE) ** -0.5
QUERY_SCALE = ATTN_SCALE * math.log2(math.e)
ATTN_BLOCK = 512

LANES = 128
SUBLANES = 8
VMEM_LIMIT_BYTES = 56 * 1024 * 1024

C_Q, C_K, C_V, C_GG, C_LX, C_LG = 0, 128, 256, 512, 768, 1024
C_DQ, C_DKV, C_TAIL = 1280, 1536, 1664
W_IN_COLS = 1792
C_LOGF = 1280
PROJ_COLS = 1408
QK_WIDTH = 256


def _cparams(sem):
    return pltpu.CompilerParams(dimension_semantics=sem, vmem_limit_bytes=VMEM_LIMIT_BYTES)


def _layer_norm_rows(y, g, b):
    mu = jnp.mean(y, axis=-1, keepdims=True)
    yc = y - mu
    var = jnp.mean(yc * yc, axis=-1, keepdims=True)
    return yc * lax.rsqrt(var + LN_EPS) * g + b


def _rms_rows(y, g):
    return y * lax.rsqrt(jnp.mean(y * y, axis=-1, keepdims=True) + RMS_EPS) * g


def _ada_kernel(c_ref, w_ref, b_ref, o_ref):
    c = c_ref[...]
    s = (c * jax.nn.sigmoid(c)).astype(BF16)
    o_ref[0] = jnp.dot(s, w_ref[0].astype(BF16), preferred_element_type=F32) + b_ref[0]


def _ada_all(c_all, w_ada, b_ada):
    depth, d, n = w_ada.shape
    bt = c_all.shape[0]
    tn = 1536
    return pl.pallas_call(
        _ada_kernel,
        grid=(depth, n // tn),
        in_specs=[pl.BlockSpec((bt, d), lambda l, j: (0, 0)),
                  pl.BlockSpec((1, d, tn), lambda l, j: (l, 0, j)),
                  pl.BlockSpec((1, 1, tn), lambda l, j: (l, 0, j))],
        out_specs=pl.BlockSpec((1, bt, tn), lambda l, j: (l, 0, j)),
        out_shape=jax.ShapeDtypeStruct((depth, bt, n), F32),
        compiler_params=_cparams(("parallel", "parallel")),
        name="ada_mod",
    )(c_all, w_ada, b_ada.reshape(depth, 1, n))


def _ln_kernel(x_ref, g_ref, b_ref, o_ref):
    o_ref[...] = _layer_norm_rows(x_ref[...], g_ref[...], b_ref[...])


def _ln_in(x2d, g, b):
    n, d = x2d.shape
    tm = min(1024, n)
    return pl.pallas_call(
        _ln_kernel,
        grid=(n // tm,),
        in_specs=[pl.BlockSpec((tm, d), lambda i: (i, 0)),
                  pl.BlockSpec((1, d), lambda i: (0, 0)),
                  pl.BlockSpec((1, d), lambda i: (0, 0))],
        out_specs=pl.BlockSpec((tm, d), lambda i: (i, 0)),
        out_shape=jax.ShapeDtypeStruct((n, d), F32),
        compiler_params=_cparams(("parallel",)),
        name="ln_in",
    )(x2d, g.reshape(1, d), b.reshape(1, d))


def _fold_kernel(uq_ref, uk_ref, uv_ref, wo_ref, qlat_ref, fold_ref):
    a = uq_ref[0, 0]
    b = uk_ref[0, 0]
    ql = lax.dot_general(a, b, (((1,), (1,)), ((), ())), precision=HIGHEST,
                         preferred_element_type=F32)
    qlat_ref[0] = (ql * QUERY_SCALE).astype(BF16)
    fd = jnp.dot(uv_ref[0, 0], wo_ref[0], precision=HIGHEST, preferred_element_type=F32)
    fold_ref[0] = fd.astype(BF16)


def _fold_weights(mla_w_uq, mla_w_uk, mla_w_uv, w_out):
    depth = mla_w_uq.shape[0]
    uq_n = jnp.transpose(mla_w_uq[..., :MLA_NOPE], (0, 2, 1, 3))
    uk_t = jnp.transpose(mla_w_uk, (0, 2, 1, 3))
    uv_t = jnp.transpose(mla_w_uv, (0, 2, 1, 3))
    mla_row0 = (GLA_WIDTH + LRU_WIDTH) // MLA_V
    return pl.pallas_call(
        _fold_kernel,
        grid=(depth, MLA_HEADS),
        in_specs=[pl.BlockSpec((1, 1, MLA_Q_LORA, MLA_NOPE), lambda l, h: (l, h, 0, 0)),
                  pl.BlockSpec((1, 1, MLA_KV_LORA, MLA_NOPE), lambda l, h: (l, h, 0, 0)),
                  pl.BlockSpec((1, 1, MLA_KV_LORA, MLA_V), lambda l, h: (l, h, 0, 0)),
                  pl.BlockSpec((1, MLA_V, D_MODEL), lambda l, h: (l, mla_row0 + h, 0))],
        out_specs=[pl.BlockSpec((1, MLA_Q_LORA, MLA_KV_LORA), lambda l, h: (l, 0, h)),
                   pl.BlockSpec((1, MLA_KV_LORA, D_MODEL), lambda l, h: (l, h, 0))],
        out_shape=[jax.ShapeDtypeStruct((depth, MLA_Q_LORA, MLA_HEADS * MLA_KV_LORA), BF16),
                   jax.ShapeDtypeStruct((depth, MLA_HEADS * MLA_KV_LORA, D_MODEL), BF16)],
        compiler_params=_cparams(("parallel", "parallel")),
        name="fold_weights",
    )(uq_n, uk_t, uv_t, w_out)


def _inproj_kernel(x_ref, ada_ref, w_ref, wg_ref, bg_ref, qn_ref, kvn_ref, wq2_ref, cs_ref,
                   proj_ref, qcat_ref, ckv_ref, kr_ref, kcat_ref):
    bb, tt, d = x_ref.shape
    m = bb * tt
    x = x_ref[...]
    shift = ada_ref[:, 0:1, :]
    scale = ada_ref[:, 1:2, :]
    h = (x * (1.0 + scale) + shift).reshape(m, d).astype(BF16)
    p = jnp.dot(h, w_ref[...], preferred_element_type=F32)
    proj_ref[:, 0:C_DQ] = p[:, 0:C_DQ]

    tail = p[:, C_TAIL:C_TAIL + LANES]
    z = jnp.dot(tail.astype(BF16), wg_ref[...], preferred_element_type=F32) + bg_ref[...]
    proj_ref[:, C_LOGF:C_LOGF + LANES] = jax.nn.log_sigmoid(z) / GLA_GATE_TAU

    cs = cs_ref[...]
    lane = lax.broadcasted_iota(jnp.int32, (m, LANES), 1)
    rope_lanes = lane < MLA_ROPE

    def rope(block):
        r = block * cs
        r = r + pltpu.roll(r, LANES - MLA_ROPE, 1)
        return jnp.where(rope_lanes, r, 0.0)

    kr = rope(tail)
    kr_ref[...] = kr[:, 0:MLA_ROPE]
    ckv = _rms_rows(p[:, C_DKV:C_DKV + MLA_KV_LORA], kvn_ref[...])
    ckv_ref[...] = ckv
    kcat_ref[...] = jnp.concatenate([ckv.astype(BF16), kr.astype(BF16)], axis=1)

    dqn = _rms_rows(p[:, C_DQ:C_DQ + MLA_Q_LORA], qn_ref[...]).astype(BF16)
    q2 = jnp.dot(dqn, wq2_ref[...], preferred_element_type=F32)
    for hh in range(MLA_HEADS):
        lat = q2[:, hh * LANES:(hh + 1) * LANES]
        rp = rope(q2[:, (MLA_HEADS + hh) * LANES:(MLA_HEADS + hh + 1) * LANES])
        qc = jnp.concatenate([lat, rp], axis=1)
        qcat_ref[:, hh] = qc.reshape(bb, tt, QK_WIDTH).astype(qcat_ref.dtype)


def _inproj(x, ada, w_in_p, wg_p, bg, qn, kvn, wq2, cs, bb, tt):
    b, t, d = x.shape
    n = b * t
    m = bb * tt
    grid = (b // bb, t // tt)
    nt = t // tt
    const = lambda i, j: (0, 0)
    row = lambda i, j: (i * nt + j, 0)
    return pl.pallas_call(
        _inproj_kernel,
        grid=grid,
        in_specs=[pl.BlockSpec((bb, tt, d), lambda i, j: (i, j, 0)),
                  pl.BlockSpec((bb, 6, d), lambda i, j: (i, 0, 0)),
                  pl.BlockSpec((d, W_IN_COLS), const),
                  pl.BlockSpec((LANES, LANES), const),
                  pl.BlockSpec((1, LANES), const),
                  pl.BlockSpec((1, MLA_Q_LORA), const),
                  pl.BlockSpec((1, MLA_KV_LORA), const),
                  pl.BlockSpec((MLA_Q_LORA, 2 * MLA_HEADS * LANES), const),
                  pl.BlockSpec((m, LANES), lambda i, j: (j, 0))],
        out_specs=[pl.BlockSpec((m, PROJ_COLS), row),
                   pl.BlockSpec((bb, MLA_HEADS, tt, QK_WIDTH), lambda i, j: (i, 0, j, 0)),
                   pl.BlockSpec((m, MLA_KV_LORA), row),
                   pl.BlockSpec((m, MLA_ROPE), row),
                   pl.BlockSpec((m, QK_WIDTH), row)],
        out_shape=[jax.ShapeDtypeStruct((n, PROJ_COLS), F32),
                   jax.ShapeDtypeStruct((b, MLA_HEADS, t, QK_WIDTH), BF16 if tt % 16 == 0 else F32),
                   jax.ShapeDtypeStruct((n, MLA_KV_LORA), F32),
                   jax.ShapeDtypeStruct((n, MLA_ROPE), F32),
                   jax.ShapeDtypeStruct((n, QK_WIDTH), BF16)],
        compiler_params=_cparams(("parallel", "parallel")),
        name="in_proj",
    )(x, ada, w_in_p, wg_p, bg, qn, kvn, wq2, cs)


def _gla_kernel(q_ref, k_ref, v_ref, gg_ref, g_ref, s0_ref, ng_ref, ltri_ref, ind_ref,
                msk_ref, seg_ref, o_ref, sT_ref, st_sc, b_sc, o_sc, *, nb, tt, c):
    tj = pl.program_id(1)
    n_chunks = tt // c
    groups = c // SUBLANES
    ltri = ltri_ref[...]
    ind = ind_ref[...]
    msk = msk_ref[...]
    row_iota = lax.broadcasted_iota(jnp.int32, (c, LANES), 0)

    @pl.when(tj == 0)
    def _():
        st_sc[...] = s0_ref[...]

    def chunk(ci, carry):
        for j in range(nb):
            r0 = pl.multiple_of(ci * c, c)
            g = g_ref[j, pl.ds(r0, c), :]
            b = jnp.dot(ltri, g, precision=HIGHEST, preferred_element_type=F32)
            b_sc[j] = b
            q = q_ref[j, pl.ds(r0, c), :]
            k = k_ref[j, pl.ds(r0, c), :]
            v = v_ref[j, pl.ds(r0, c), :]
            blast = b_sc[j, pl.ds(c - 1, 1), :]
            qe = q * jnp.exp(b)
            ke = k * jnp.exp(blast - b)
            st = st_sc[j]
            o_inter = lax.dot_general(qe.astype(BF16), st.astype(BF16), (((1,), (1,)), ((), ())),
                                      preferred_element_type=F32)
            ut = lax.dot_general(v.astype(BF16), ke.astype(BF16), (((0,), (0,)), ((), ())),
                                 preferred_element_type=F32)
            st_sc[j] = st * jnp.exp(blast) + ut * msk

            o_blk = [None] * groups
            for g0 in range(groups):
                lo = g0 * SUBLANES
                pieces = []
                for s in range(lo, lo + SUBLANES):
                    bs = b_sc[j, pl.ds(s, 1), :]
                    ks = k_ref[j, pl.ds(r0 + s, 1), :]
                    e = jnp.exp(b[lo:, :] - bs)
                    head = jnp.where(row_iota[lo:lo + SUBLANES, :] >= s, e[0:SUBLANES, :], 0.0)
                    e = head if c - lo == SUBLANES else jnp.concatenate([head, e[SUBLANES:, :]], axis=0)
                    pieces.append(e * q[lo:, :] * ks)
                w = jnp.concatenate(pieces, axis=0).astype(BF16)
                a = jnp.dot(w, ind, preferred_element_type=F32)
                rows = c - lo
                for idx in range(SUBLANES):
                    vs = v_ref[j, pl.ds(r0 + lo + idx, 1), :]
                    for rb in range(g0, groups):
                        piece = a[idx * rows + (rb - g0) * SUBLANES:
                                  idx * rows + (rb - g0 + 1) * SUBLANES, :] * vs
                        o_blk[rb] = piece if o_blk[rb] is None else o_blk[rb] + piece
            o_intra = jnp.concatenate(o_blk, axis=0) if groups > 1 else o_blk[0]
            o_sc[j, pl.ds(r0, c), :] = o_inter + o_intra
        return carry

    lax.fori_loop(0, n_chunks, chunk, 0)

    @pl.when(tj == pl.num_programs(1) - 1)
    def _():
        sT_ref[...] = st_sc[...]

    ng = ng_ref[...]
    seg = seg_ref[...]

    def epi(j, carry):
        o = o_sc[j]
        ms = jnp.dot(o * o, seg, precision=HIGHEST, preferred_element_type=F32)
        gg = gg_ref[j]
        o_ref[j] = o * lax.rsqrt(ms + RMS_EPS) * ng * (gg * jax.nn.sigmoid(gg))
        return carry

    lax.fori_loop(0, nb, epi, 0)


def _gla(proj, s0T, ng, b, t, nb, tt):
    n = b * t
    c = min(GLA_CHUNK, t)
    proj3 = proj.reshape(b, t, PROJ_COLS)
    ltri = jnp.asarray(np.tril(np.ones((c, c), np.float32)))
    hk = np.arange(GLA_HEADS * GLA_DK) // GLA_DK
    hv = np.arange(GLA_WIDTH) // GLA_DV
    ind = jnp.asarray((hk[:, None] == hv[None, :]).astype(np.float32)).astype(BF16)
    msk = jnp.asarray((hv[:, None] == hk[None, :]).astype(np.float32))
    seg = jnp.asarray((hv[:, None] == hv[None, :]).astype(np.float32) / GLA_DV)
    const = lambda i, j: (0, 0)
    kern = functools.partial(_gla_kernel, nb=nb, tt=tt, c=c)

    def cols(width, col0):
        return pl.BlockSpec((nb, tt, width), lambda i, j: (i, j, col0 // width))

    o_gla, sT = pl.pallas_call(
        kern,
        grid=(b // nb, t // tt),
        in_specs=[cols(LANES, C_Q), cols(LANES, C_K), cols(GLA_WIDTH, C_V), cols(GLA_WIDTH, C_GG),
                  cols(LANES, C_LOGF),
                  pl.BlockSpec((nb, GLA_WIDTH, LANES), lambda i, j: (i, 0, 0)),
                  pl.BlockSpec((1, GLA_WIDTH), const),
                  pl.BlockSpec((c, c), const),
                  pl.BlockSpec((LANES, GLA_WIDTH), const),
                  pl.BlockSpec((GLA_WIDTH, LANES), const),
                  pl.BlockSpec((GLA_WIDTH, GLA_WIDTH), const)],
        out_specs=[pl.BlockSpec((nb, tt, GLA_WIDTH), lambda i, j: (i, j, 0)),
                   pl.BlockSpec((nb, GLA_WIDTH, LANES), lambda i, j: (i, 0, 0))],
        out_shape=[jax.ShapeDtypeStruct((b, t, GLA_WIDTH), F32),
                   jax.ShapeDtypeStruct((b, GLA_WIDTH, LANES), F32)],
        scratch_shapes=[pltpu.VMEM((nb, GLA_WIDTH, LANES), F32),
                        pltpu.VMEM((nb, c, LANES), F32),
                        pltpu.VMEM((nb, tt, GLA_WIDTH), F32)],
        compiler_params=_cparams(("parallel", "arbitrary")),
        name="gla",
    )(proj3, proj3, proj3, proj3, proj3, s0T, ng, ltri, ind, msk, seg)
    return o_gla.reshape(n, GLA_WIDTH), sT


def _lru_kernel(lx_ref, lg_ref, cbuf_ref, h0_ref, cw_ref, cb_ref, wax_ref, bax_ref, sp_ref,
                o_ref, hn_ref, cn_ref, xp_sc, a_sc, u_sc, *, nb, t):
    pad = SUBLANES
    hist = CONV_WIDTH - 1
    row8 = lax.broadcasted_iota(jnp.int32, (SUBLANES, LRU_WIDTH), 0)
    rows_per_iter = min(32, t)
    sub = rows_per_iter // SUBLANES

    def scan_block(a, u):
        for dd in (1, 2, 4):
            a_s = jnp.where(row8 >= dd, pltpu.roll(a, dd, 0), 1.0)
            u_s = jnp.where(row8 >= dd, pltpu.roll(u, dd, 0), 0.0)
            u = a * u_s + u
            a = a * a_s
        return a, u

    for j in range(nb):
        base = j * t
        xp_sc[pl.ds(0, pad), :] = jnp.zeros((pad, LRU_WIDTH), F32)
        xp_sc[pl.ds(pad - hist, hist), :] = cbuf_ref[j]
        xp_sc[pl.ds(pad, t), :] = lx_ref[pl.ds(base, t), :]
        xc = cb_ref[...]
        for kk in range(CONV_WIDTH):
            xc = xc + xp_sc[pl.ds(pad - hist + kk, t), :] * cw_ref[pl.ds(kk, 1), :]
        cn_ref[j] = xp_sc[pl.ds(pad + t - hist, hist), :]

        ax = jnp.dot(xc.astype(BF16), wax_ref[...], preferred_element_type=F32) + bax_ref[...]
        r = jax.nn.sigmoid(ax[:, 0:LRU_WIDTH])
        ig = jax.nn.sigmoid(ax[:, LRU_WIDTH:2 * LRU_WIDTH])
        log_a = -LRU_C * r * sp_ref[...]
        a = jnp.exp(log_a)
        u = jnp.sqrt((a * a + 1.0) * jnp.tanh(-log_a)) * (ig * xc)
        a_sc[...] = a
        u_sc[...] = u
        u_sc[pl.ds(0, 1), :] = u[0:1, :] + a[0:1, :] * h0_ref[j]

        def step(i, hprev):
            r0 = pl.multiple_of(i * rows_per_iter, rows_per_iter)
            scans = []
            for sb in range(sub):
                ab = a_sc[pl.ds(r0 + sb * SUBLANES, SUBLANES), :]
                ub = u_sc[pl.ds(r0 + sb * SUBLANES, SUBLANES), :]
                scans.append(scan_block(ab, ub))
            for sb in range(sub):
                ac, uc = scans[sb]
                hb = ac * hprev + uc
                u_sc[pl.ds(r0 + sb * SUBLANES, SUBLANES), :] = hb
                hprev = hb[SUBLANES - 1:SUBLANES, :]
            return hprev

        hlast = lax.fori_loop(0, t // rows_per_iter, step, jnp.zeros((1, LRU_WIDTH), F32))
        hn_ref[j] = hlast
        lg = lg_ref[pl.ds(base, t), :]
        o_ref[pl.ds(base, t), :] = jax.nn.gelu(lg, approximate=True) * u_sc[...]


def _lru(proj, cbuf, h0, cw, cb, wax, bax, sp, b, t, nb):
    n = b * t
    rows = nb * t
    const = lambda i: (0, 0)
    kern = functools.partial(_lru_kernel, nb=nb, t=t)
    return pl.pallas_call(
        kern,
        grid=(b // nb,),
        in_specs=[pl.BlockSpec((rows, LRU_WIDTH), lambda i: (i, C_LX // LRU_WIDTH)),
                  pl.BlockSpec((rows, LRU_WIDTH), lambda i: (i, C_LG // LRU_WIDTH)),
                  pl.BlockSpec((nb, CONV_WIDTH - 1, LRU_WIDTH), lambda i: (i, 0, 0)),
                  pl.BlockSpec((nb, 1, LRU_WIDTH), lambda i: (i, 0, 0)),
                  pl.BlockSpec((CONV_WIDTH, LRU_WIDTH), const),
                  pl.BlockSpec((1, LRU_WIDTH), const),
                  pl.BlockSpec((LRU_WIDTH, 2 * LRU_WIDTH), const),
                  pl.BlockSpec((1, 2 * LRU_WIDTH), const),
                  pl.BlockSpec((1, LRU_WIDTH), const)],
        out_specs=[pl.BlockSpec((rows, LRU_WIDTH), lambda i: (i, 0)),
                   pl.BlockSpec((nb, 1, LRU_WIDTH), lambda i: (i, 0, 0)),
                   pl.BlockSpec((nb, CONV_WIDTH - 1, LRU_WIDTH), lambda i: (i, 0, 0))],
        out_shape=[jax.ShapeDtypeStruct((n, LRU_WIDTH), F32),
                   jax.ShapeDtypeStruct((b, 1, LRU_WIDTH), F32),
                   jax.ShapeDtypeStruct((b, CONV_WIDTH - 1, LRU_WIDTH), F32)],
        scratch_shapes=[pltpu.VMEM((t + SUBLANES, LRU_WIDTH), F32),
                        pltpu.VMEM((t, LRU_WIDTH), F32),
                        pltpu.VMEM((t, LRU_WIDTH), F32)],
        compiler_params=_cparams(("parallel",)),
        name="rg_lru",
    )(proj, proj, cbuf, h0, cw, cb, wax, bax, sp)


def _softmax_step(s, m_ref, l_ref, acc_ref, v):
    shape3 = m_ref.shape
    rows = shape3[0] * shape3[1]
    reps = s.shape[1] // LANES
    m_prev = m_ref[...].reshape(rows, LANES)
    m_next = jnp.maximum(m_prev, jnp.max(s, axis=1, keepdims=True))
    alpha = jnp.exp2(m_prev - m_next)
    m_wide = m_next if reps == 1 else jnp.concatenate([m_next] * reps, axis=1)
    p = jnp.exp2(s - m_wide)
    l_next = alpha * l_ref[...].reshape(rows, LANES) + jnp.sum(p, axis=1, keepdims=True)
    acc_next = alpha * acc_ref[...].reshape(rows, LANES) + jnp.dot(p.astype(BF16), v,
                                                                    preferred_element_type=F32)
    l_ref[...] = l_next.reshape(shape3)
    acc_ref[...] = acc_next.reshape(shape3)
    m_ref[...] = m_next.reshape(shape3)


def _causal_mask(s, n):
    tpos = lax.broadcasted_iota(jnp.int32, s.shape, 0) & (n - 1)
    kpos = lax.broadcasted_iota(jnp.int32, s.shape, 1)
    return jnp.where(kpos <= tpos, s, -jnp.inf)


def _attn_kernel(q_ref, k_ref, o_ref, m_sc, l_sc, acc_sc, *, tq):
    i = pl.program_id(1)
    half = tq // 2
    m_sc[...] = jnp.full(m_sc.shape, -jnp.inf, F32)
    l_sc[...] = jnp.zeros(l_sc.shape, F32)
    acc_sc[...] = jnp.zeros(acc_sc.shape, F32)
    nt = (((1,), (1,)), ((), ()))

    def body(j, carry):
        kb = k_ref[0, pl.ds(pl.multiple_of(j * tq, tq), tq), :]
        q = q_ref[0].reshape(MLA_HEADS * tq, QK_WIDTH)
        s = lax.dot_general(q, kb, nt, preferred_element_type=F32)
        _softmax_step(s, m_sc, l_sc, acc_sc, kb[:, 0:MLA_KV_LORA])
        return carry

    lax.fori_loop(0, i, body, 0)

    k0 = pl.multiple_of(i * tq, tq)
    kb_a = k_ref[0, pl.ds(k0, half), :]
    q = q_ref[0].reshape(MLA_HEADS * tq, QK_WIDTH)
    s_a = lax.dot_general(q, kb_a, nt, preferred_element_type=F32)
    tpos = lax.broadcasted_iota(jnp.int32, s_a.shape, 0) & (tq - 1)
    kpos = lax.broadcasted_iota(jnp.int32, s_a.shape, 1)
    s_a = jnp.where(kpos <= tpos, s_a, -jnp.inf)
    _softmax_step(s_a, m_sc, l_sc, acc_sc, kb_a[:, 0:MLA_KV_LORA])

    kb_b = k_ref[0, pl.ds(k0 + half, half), :]
    late = pl.ds(half, half)
    q_b = q_ref[0, :, late, :].reshape(MLA_HEADS * half, QK_WIDTH)
    s_b = _causal_mask(lax.dot_general(q_b, kb_b, nt, preferred_element_type=F32), half)
    _softmax_step(s_b, m_sc.at[:, late, :], l_sc.at[:, late, :], acc_sc.at[:, late, :],
                  kb_b[:, 0:MLA_KV_LORA])

    for hh in range(MLA_HEADS):
        o_ref[0, :, hh * MLA_KV_LORA:(hh + 1) * MLA_KV_LORA] = (acc_sc[hh] / l_sc[hh]).astype(BF16)


def _attn_prompt(qcat, kcat, b, t):
    tq = min(ATTN_BLOCK, t)
    kern = functools.partial(_attn_kernel, tq=tq)
    stat = pltpu.VMEM((MLA_HEADS, tq, LANES), F32)
    return pl.pallas_call(
        kern,
        grid=(b, t // tq),
        in_specs=[pl.BlockSpec((1, MLA_HEADS, tq, QK_WIDTH), lambda bi, i: (bi, 0, i, 0)),
                  pl.BlockSpec((1, t, QK_WIDTH), lambda bi, i: (bi, 0, 0))],
        out_specs=pl.BlockSpec((1, tq, MLA_HEADS * MLA_KV_LORA), lambda bi, i: (bi, i, 0)),
        out_shape=jax.ShapeDtypeStruct((b, t, MLA_HEADS * MLA_KV_LORA), BF16),
        scratch_shapes=[stat, stat, stat],
        compiler_params=_cparams(("parallel", "parallel")),
        name="attn_prompt",
    )(qcat, kcat.reshape(b, t, QK_WIDTH))


PAGE_SLOTS = 3
PAGED_SUB_KEYS = 1024
PAGED_LAG = 2


def _attn_paged_kernel(pt_ref, q_ref, cnew_ref, rnew_ref, ckv_hbm, krt_hbm, o_ref,
                       kc_buf, kr_buf, sem, m_sc, l_sc, acc_sc, *, pages, ts, layer, steps, n_steps):
    j = pl.program_id(1)
    step = pl.program_id(0) * steps + j
    r = MLA_HEADS * ts

    def page_copy(src_page, slot, i):
        return (pltpu.make_async_copy(ckv_hbm.at[layer, src_page],
                                      kc_buf.at[slot, pl.ds(i * PAGE_SIZE, PAGE_SIZE), :],
                                      sem.at[slot, 0]),
                pltpu.make_async_copy(krt_hbm.at[layer, src_page],
                                      kr_buf.at[slot, :, pl.ds(i * PAGE_SIZE, PAGE_SIZE)],
                                      sem.at[slot, 1]))

    def start_step(step_id, slot):
        seq = step_id // steps
        first = (step_id - seq * steps) * pages
        for i in range(pages):
            for cp in page_copy(pt_ref[seq, first + i], slot, i):
                cp.start()

    @pl.when(step == 0)
    def _():
        for ahead in range(min(PAGE_SLOTS - 1, n_steps)):
            start_step(ahead, ahead)

    @pl.when(j == 0)
    def _():
        m_sc[...] = jnp.full(m_sc.shape, -jnp.inf, F32)
        l_sc[...] = jnp.zeros(l_sc.shape, F32)
        acc_sc[...] = jnp.zeros(acc_sc.shape, F32)

    slot = lax.rem(step, PAGE_SLOTS)
    for i in range(pages):
        for cp in page_copy(0, slot, i):
            cp.wait()

    q = q_ref[0].reshape(r, QK_WIDTH).astype(BF16)
    q_lat = q[:, 0:MLA_KV_LORA]
    q_rope = q[:, MLA_KV_LORA:MLA_KV_LORA + MLA_ROPE]
    sub = min(PAGED_SUB_KEYS, pages * PAGE_SIZE)
    n_sub = (pages * PAGE_SIZE) // sub

    def scores(ci):
        kc = kc_buf[slot, pl.ds(ci * sub, sub), :].astype(BF16)
        krt = kr_buf[slot, :, pl.ds(ci * sub, sub)].astype(BF16)
        s = (lax.dot_general(q_lat, kc, (((1,), (1,)), ((), ())), preferred_element_type=F32)
             + jnp.dot(q_rope, krt, preferred_element_type=F32))
        m_c = jnp.broadcast_to(jnp.max(s, axis=1, keepdims=True), (r, LANES))
        p = jnp.exp2(s - jnp.concatenate([m_c] * (sub // LANES), axis=1))
        l_c = jnp.broadcast_to(jnp.sum(p, axis=1, keepdims=True), (r, LANES))
        return m_c, l_c, p.astype(BF16), kc

    staged, parts = [], []
    for ci in range(n_sub + PAGED_LAG):
        if ci < n_sub:
            staged.append(scores(ci))
        if ci >= PAGED_LAG:
            m_c, l_c, p_c, kc_c = staged[ci - PAGED_LAG]
            parts.append((m_c, l_c, jnp.dot(p_c, kc_c, preferred_element_type=F32)))
    stat3 = m_sc.shape
    m_prev = m_sc[...].reshape(r, LANES)
    m_next = m_prev
    for m_c, _, _ in parts:
        m_next = jnp.maximum(m_next, m_c)
    alpha = jnp.exp2(m_prev - m_next)
    l_next = alpha * l_sc[...].reshape(r, LANES)
    acc_next = alpha * acc_sc[...].reshape(r, LANES)
    for m_c, l_c, acc_c in parts:
        w_c = jnp.exp2(m_c - m_next)
        l_next = l_next + w_c * l_c
        acc_next = acc_next + w_c * acc_c
    m_sc[...] = m_next.reshape(stat3)
    l_sc[...] = l_next.reshape(stat3)
    acc_sc[...] = acc_next.reshape(stat3)

    nxt = step + (PAGE_SLOTS - 1)

    @pl.when(nxt < n_steps)
    def _():
        start_step(nxt, lax.rem(nxt, PAGE_SLOTS))

    @pl.when(j == steps - 1)
    def _():
        pad = jnp.zeros((LANES - ts, MLA_KV_LORA), F32)
        kc_new = jnp.concatenate([cnew_ref[...], pad], axis=0).astype(BF16)
        kr_new = jnp.concatenate([rnew_ref[...], pad[:, 0:MLA_ROPE]], axis=0).astype(BF16)
        s_new = (lax.dot_general(q_lat, kc_new, (((1,), (1,)), ((), ())), preferred_element_type=F32)
                 + lax.dot_general(q_rope, kr_new, (((1,), (1,)), ((), ())),
                                   preferred_element_type=F32))
        _softmax_step(_causal_mask(s_new, ts), m_sc, l_sc, acc_sc, kc_new)
        for hh in range(MLA_HEADS):
            o_ref[0, :, hh * MLA_KV_LORA:(hh + 1) * MLA_KV_LORA] = acc_sc[hh] / l_sc[hh]


def _attn_paged(qcat, ckv_new, kr_new, cache_ckv, cache_krope_t, page_table, layer, pages):
    b, _, ts, _ = qcat.shape
    n_pages = page_table.shape[1]
    steps = n_pages // pages
    kern = functools.partial(_attn_paged_kernel, pages=pages, ts=ts, layer=layer, steps=steps,
                             n_steps=b * steps)
    stat = pltpu.VMEM((MLA_HEADS, ts, LANES), F32)
    grid_spec = pltpu.PrefetchScalarGridSpec(
        num_scalar_prefetch=1,
        grid=(b, steps),
        in_specs=[pl.BlockSpec((1, MLA_HEADS, ts, QK_WIDTH), lambda bi, j, pt: (bi, 0, 0, 0)),
                  pl.BlockSpec((ts, MLA_KV_LORA), lambda bi, j, pt: (bi, 0)),
                  pl.BlockSpec((ts, MLA_ROPE), lambda bi, j, pt: (bi, 0)),
                  pl.BlockSpec(memory_space=pl.ANY),
                  pl.BlockSpec(memory_space=pl.ANY)],
        out_specs=pl.BlockSpec((1, ts, MLA_HEADS * MLA_KV_LORA), lambda bi, j, pt: (bi, 0, 0)),
        scratch_shapes=[pltpu.VMEM((PAGE_SLOTS, pages * PAGE_SIZE, MLA_KV_LORA), F32),
                        pltpu.VMEM((PAGE_SLOTS, MLA_ROPE, pages * PAGE_SIZE), F32),
                        pltpu.SemaphoreType.DMA((PAGE_SLOTS, 2)),
                        stat, stat, stat])
    return pl.pallas_call(
        kern,
        grid_spec=grid_spec,
        out_shape=jax.ShapeDtypeStruct((b, ts, MLA_HEADS * MLA_KV_LORA), F32),
        compiler_params=_cparams(("arbitrary", "arbitrary")),
        name="attn_paged",
    )(page_table, qcat, ckv_new, kr_new, cache_ckv, cache_krope_t)


FFN_CHUNK = 256


def _mix_ffn_kernel(x_ref, ada_ref, og_ref, ol_ref, oa_ref, wgl_ref, wf_ref, g1_ref, b1_ref,
                    wgu_ref, wd_ref, g2_ref, b2_ref, o_ref, *, alpha):
    bb, tt, d = x_ref.shape
    m = bb * tt
    gate1 = ada_ref[:, 2:3, :]
    shift2 = ada_ref[:, 3:4, :]
    scale2 = ada_ref[:, 4:5, :]
    gate2 = ada_ref[:, 5:6, :]
    ogl = jnp.concatenate([og_ref[...], ol_ref[...]], axis=1).astype(BF16)
    mix = (jnp.dot(ogl, wgl_ref[...], preferred_element_type=F32)
           + jnp.dot(oa_ref[...].astype(BF16), wf_ref[...], preferred_element_type=F32))
    x1 = _layer_norm_rows(alpha * x_ref[...] + gate1 * mix.reshape(bb, tt, d), g1_ref[...], b1_ref[...])

    h2 = (x1 * (1.0 + scale2) + shift2).reshape(m, d).astype(BF16)
    acc = jnp.zeros((m, d), F32)
    for ci in range(D_FF // FFN_CHUNK):
        lo = ci * FFN_CHUNK
        gf = jnp.dot(h2, wgu_ref[:, lo:lo + FFN_CHUNK], preferred_element_type=F32)
        uf = jnp.dot(h2, wgu_ref[:, D_FF + lo:D_FF + lo + FFN_CHUNK], preferred_element_type=F32)
        act = (gf * jax.nn.sigmoid(gf) * uf).astype(BF16)
        acc = acc + jnp.dot(act, wd_ref[lo:lo + FFN_CHUNK, :], preferred_element_type=F32)
    y = alpha * x1 + gate2 * acc.reshape(bb, tt, d)
    o_ref[...] = _layer_norm_rows(y, g2_ref[...], b2_ref[...])


def _mix_ffn(x, ada, o_gla, o_lru, o_lat, wgl, wfold, ln1, wgu, wd, ln2, bb, tt, alpha):
    b, t, d = x.shape
    nt = t // tt
    m = bb * tt
    row = lambda i, j: (i * nt + j, 0)

    def resident(shape):
        return pl.BlockSpec(shape, lambda i, j: (0, 0), pipeline_mode=pl.Buffered(1))

    kern = functools.partial(_mix_ffn_kernel, alpha=alpha)
    return pl.pallas_call(
        kern,
        grid=(b // bb, nt),
        in_specs=[pl.BlockSpec((bb, tt, d), lambda i, j: (i, j, 0)),
                  pl.BlockSpec((bb, 6, d), lambda i, j: (i, 0, 0)),
                  pl.BlockSpec((m, GLA_WIDTH), row),
                  pl.BlockSpec((m, LRU_WIDTH), row),
                  pl.BlockSpec((m, MLA_HEADS * MLA_KV_LORA), row),
                  resident((GLA_WIDTH + LRU_WIDTH, d)),
                  resident((MLA_HEADS * MLA_KV_LORA, d)),
                  resident((1, d)), resident((1, d)),
                  resident((d, 2 * D_FF)),
                  resident((D_FF, d)),
                  resident((1, d)), resident((1, d))],
        out_specs=pl.BlockSpec((bb, tt, d), lambda i, j: (i, j, 0)),
        out_shape=jax.ShapeDtypeStruct((b, t, d), F32),
        compiler_params=_cparams(("parallel", "parallel")),
        name="mix_ffn",
    )(x, ada, o_gla, o_lru, o_lat, wgl, wfold, ln1[0], ln1[1], wgu, wd, ln2[0], ln2[1])


def _rotate_half_cols(w):
    half = MLA_ROPE // 2
    return jnp.concatenate([-w[..., half:], w[..., :half]], axis=-1)


def _prep_layer_weights(l, w_in, gla_w_gate, gla_b_gate, gla_norm_g, lru_conv_w, lru_conv_b, lru_w_a,
                        lru_b_a, lru_w_x, lru_b_x, lru_lambda, mla_q_norm_g, mla_w_uq, mla_kv_norm_g,
                        w_qlat, w_out, ffn_w_gu, ffn_w_down):
    w = w_in[l]
    d = w.shape[0]
    o = np.cumsum([0, 128, 128, 256, 256, 16, 256, 256, 256, 128, 32])
    gq, gk, gv, gg, glr, lx, lgt, dq, dkv, kr = [w[:, o[i]:o[i + 1]] for i in range(10)]
    tail = jnp.concatenate([kr, _rotate_half_cols(kr), glr,
                            jnp.zeros((d, LANES - 2 * MLA_ROPE - GLA_LOWRANK), F32)], axis=1)
    w_in_p = jnp.concatenate([gq * (GLA_DK ** -0.5), gk, gv, gg, lx, lgt, dq, dkv, tail],
                             axis=1).astype(BF16)
    wg_p = jnp.pad(gla_w_gate[l], ((2 * MLA_ROPE, LANES - 2 * MLA_ROPE - GLA_LOWRANK),
                                   (0, 0))).astype(BF16)
    bg = gla_b_gate[l].reshape(1, LANES)
    ng = jnp.tile(gla_norm_g[l], GLA_HEADS).reshape(1, GLA_WIDTH)

    rope_w = mla_w_uq[l][:, :, MLA_NOPE:] * QUERY_SCALE
    rope_blk = jnp.concatenate(
        [rope_w, _rotate_half_cols(rope_w),
         jnp.zeros((MLA_Q_LORA, MLA_HEADS, LANES - 2 * MLA_ROPE), F32)], axis=-1)
    wq2 = jnp.concatenate([w_qlat[l], rope_blk.reshape(MLA_Q_LORA, MLA_HEADS * LANES).astype(BF16)],
                          axis=1)

    def block_diag(wb):
        on_diag = jnp.eye(LRU_BLOCKS, dtype=bool)[:, None, :, None]
        return jnp.where(on_diag, wb[:, :, None, :], 0.0).reshape(LRU_WIDTH, LRU_WIDTH)

    wax = jnp.concatenate([block_diag(lru_w_a[l]), block_diag(lru_w_x[l])], axis=1).astype(BF16)
    bax = jnp.concatenate([lru_b_a[l], lru_b_x[l]]).reshape(1, 2 * LRU_WIDTH)
    sp = jax.nn.softplus(-lru_lambda[l].astype(F32)).reshape(1, LRU_WIDTH)
    return dict(
        w_in_p=w_in_p, wg_p=wg_p, bg=bg, ng=ng, wq2=wq2,
        qn=mla_q_norm_g[l].reshape(1, MLA_Q_LORA), kvn=mla_kv_norm_g[l].reshape(1, MLA_KV_LORA),
        cw=lru_conv_w[l], cb=lru_conv_b[l].reshape(1, LRU_WIDTH), wax=wax, bax=bax, sp=sp,
        wgl=w_out[l][:GLA_WIDTH + LRU_WIDTH].astype(BF16),
        wgu=ffn_w_gu[l].astype(BF16), wd=ffn_w_down[l].astype(BF16))


def _rope_table(pos):
    half = MLA_ROPE // 2
    inv_freq = ROPE_THETA ** (-jnp.arange(half, dtype=F32) / half)
    ang = pos.astype(F32)[:, None] * inv_freq[None, :]
    cos, sin = jnp.cos(ang), jnp.sin(ang)
    return jnp.concatenate([cos, cos, sin, sin,
                            jnp.zeros((pos.shape[0], LANES - 2 * MLA_ROPE), F32)], axis=1)


def _state_to_blockdiag_t(s):
    b = s.shape[0]
    on_diag = jnp.eye(GLA_HEADS, dtype=bool)[None, :, None, :, None]
    st = jnp.swapaxes(s, 2, 3)[:, :, :, None, :]
    return jnp.where(on_diag, st, 0.0).reshape(b, GLA_WIDTH, GLA_HEADS * GLA_DK)


def _blockdiag_t_to_state(st):
    return jnp.stack([jnp.swapaxes(st[:, h * GLA_DV:(h + 1) * GLA_DV, h * GLA_DK:(h + 1) * GLA_DK], 1, 2)
                      for h in range(GLA_HEADS)], axis=1)


def _group_layer(x, ada, lw, wfold_l, cs, s0T, h0, cbuf, tiles, alpha, ln1, ln2, attend):
    b, t, _ = x.shape
    bb, tt, nb, gla_nb, gla_tt = tiles
    proj, qcat, ckv_new, kr_new, kcat = _inproj(x, ada, lw["w_in_p"], lw["wg_p"], lw["bg"], lw["qn"],
                                                lw["kvn"], lw["wq2"], cs, bb, tt)
    o_gla, sT = _gla(proj, s0T, lw["ng"], b, t, gla_nb, gla_tt)
    o_lru, h_new, conv_new = _lru(proj, cbuf, h0, lw["cw"], lw["cb"], lw["wax"], lw["bax"], lw["sp"],
                                  b, t, nb)
    o_lat = attend(qcat, kcat, ckv_new, kr_new)
    x2 = _mix_ffn(x, ada, o_gla, o_lru, o_lat.reshape(b * t, -1), lw["wgl"], wfold_l, ln1,
                  lw["wgu"], lw["wd"], ln2, bb, tt, alpha)
    states = (_blockdiag_t_to_state(sT), h_new.reshape(b, LRU_WIDTH), conv_new,
              ckv_new.reshape(b, t, MLA_KV_LORA), kr_new.reshape(b, t, MLA_ROPE))
    return x2, states


def kernel(x_prompt, x_sample, c_prompt, c_sample, state_gla, state_lru, state_conv, cache_ckv, cache_krope, page_table, ln_in_g, ln_in_b, w_ada, b_ada, w_in, gla_w_gate, gla_b_gate, gla_norm_g, lru_conv_w, lru_conv_b, lru_w_a, lru_b_a, lru_w_x, lru_b_x, lru_lambda, mla_q_norm_g, mla_w_uq, mla_kv_norm_g, mla_w_uk, mla_w_uv, w_out, ln1_g, ln1_b, ffn_w_gu, ffn_w_down, ln2_g, ln2_b):
    bp, tp, d = x_prompt.shape
    bs, ts, _ = x_sample.shape
    depth = w_in.shape[0]
    n_pages = page_table.shape[1]
    past_len = n_pages * PAGE_SIZE
    alpha = (2.0 * depth) ** 0.25

    tiles_p = (1, min(512, tp), 2 if bp % 2 == 0 else 1, 8 if bp % 8 == 0 else 1, min(256, tp))
    tiles_s = (bs, ts, 4 if bs % 4 == 0 else 1, 8 if bs % 8 == 0 else 1, ts)
    pages_per_step = 64 if n_pages % 64 == 0 else n_pages

    ada = _ada_all(jnp.concatenate([c_prompt, c_sample], axis=0), w_ada, b_ada)
    ada = ada.reshape(depth, bp + bs, 6, d)
    xp = _ln_in(x_prompt.reshape(bp * tp, d), ln_in_g, ln_in_b).reshape(bp, tp, d)
    xs = _ln_in(x_sample.reshape(bs * ts, d), ln_in_g, ln_in_b).reshape(bs, ts, d)
    w_qlat, w_fold = _fold_weights(mla_w_uq, mla_w_uk, mla_w_uv, w_out)

    cache_krope_t = jnp.swapaxes(cache_krope, 2, 3)
    cs_p = _rope_table(jnp.arange(tp, dtype=jnp.int32))
    cs_s = jnp.tile(_rope_table(past_len + jnp.arange(ts, dtype=jnp.int32)), (bs, 1))
    zero_sT = jnp.zeros((bp, GLA_WIDTH, GLA_HEADS * GLA_DK), F32)
    zero_h = jnp.zeros((bp, 1, LRU_WIDTH), F32)
    zero_conv = jnp.zeros((bp, CONV_WIDTH - 1, LRU_WIDTH), F32)

    st_p, st_s = [], []
    for l in range(depth):
        lw = _prep_layer_weights(l, w_in, gla_w_gate, gla_b_gate, gla_norm_g, lru_conv_w, lru_conv_b,
                                 lru_w_a, lru_b_a, lru_w_x, lru_b_x, lru_lambda, mla_q_norm_g, mla_w_uq,
                                 mla_kv_norm_g, w_qlat, w_out, ffn_w_gu, ffn_w_down)
        ln1 = (ln1_g[l].reshape(1, d), ln1_b[l].reshape(1, d))
        ln2 = (ln2_g[l].reshape(1, d), ln2_b[l].reshape(1, d))

        def attend_p(qcat, kcat, ckv_new, kr_new):
            return _attn_prompt(qcat, kcat, bp, tp)

        def attend_s(qcat, kcat, ckv_new, kr_new, l=l):
            return _attn_paged(qcat, ckv_new, kr_new, cache_ckv, cache_krope_t, page_table, l,
                               pages_per_step)

        xp, sp = _group_layer(xp, ada[l, :bp], lw, w_fold[l], cs_p, zero_sT, zero_h, zero_conv,
                              tiles_p, alpha, ln1, ln2, attend_p)
        xs, ss = _group_layer(xs, ada[l, bp:], lw, w_fold[l], cs_s, _state_to_blockdiag_t(state_gla[l]),
                              state_lru[l].reshape(bs, 1, LRU_WIDTH), state_conv[l],
                              tiles_s, alpha, ln1, ln2, attend_s)
        st_p.append(sp)
        st_s.append(ss)

    def stk(outs, j):
        return jnp.stack([o[j] for o in outs])

    return (xp, xs, stk(st_p, 0), stk(st_s, 0), stk(st_p, 1), stk(st_s, 1), stk(st_p, 2), stk(st_s, 2),
            stk(st_p, 3), stk(st_s, 3), stk(st_p, 4), stk(st_s, 4))
```

```python
import functools
import math

import numpy as np
import jax
import jax.numpy as jnp
from jax import lax
from jax.experimental import pallas as pl
from jax.experimental.pallas import tpu as pltpu

F32 = jnp.float32
BF16 = jnp.bfloat16
HIGHEST = lax.Precision.HIGHEST

D_MODEL = 1024
PAGE_SIZE = 128
GLA_HEADS = 4
GLA_DK = 32
GLA_DV = 64
GLA_WIDTH = GLA_HEADS * GLA_DV
GLA_LOWRANK = 16
GLA_GATE_TAU = 16.0
GLA_CHUNK = 32
LRU_WIDTH = 256
LRU_BLOCKS = 4
LRU_BLOCK_W = LRU_WIDTH // LRU_BLOCKS
CONV_WIDTH = 4
LRU_C = 8.0
MLA_HEADS = 8
MLA_NOPE = 64
MLA_ROPE = 32
MLA_V = 64
MLA_Q_LORA = 256
MLA_KV_LORA = 128
ROPE_THETA = 10000.0
D_FF = 2816
LN_EPS = 1e-5
RMS_EPS = 1e-6
ATTN_SCALE = (MLA_NOPE + MLA_ROPE) ** -0.5
QUERY_SCALE = ATTN_SCALE * math.log2(math.e)
ATTN_BLOCK = 512

LANES = 128
SUBLANES = 8
VMEM_LIMIT_BYTES = 56 * 1024 * 1024

C_Q, C_K, C_V, C_GG, C_LX, C_LG = 0, 128, 256, 512, 768, 1024
C_DQ, C_DKV, C_TAIL = 1280, 1536, 1664
W_IN_COLS = 1792
C_LOGF = 1280
PROJ_COLS = 1408
QK_WIDTH = 256


def _cparams(sem):
    return pltpu.CompilerParams(dimension_semantics=sem, vmem_limit_bytes=VMEM_LIMIT_BYTES)


def _layer_norm_rows(y, g, b):
    mu = jnp.mean(y, axis=-1, keepdims=True)
    yc = y - mu
    var = jnp.mean(yc * yc, axis=-1, keepdims=True)
    return yc * lax.rsqrt(var + LN_EPS) * g + b


def _rms_rows(y, g):
    return y * lax.rsqrt(jnp.mean(y * y, axis=-1, keepdims=True) + RMS_EPS) * g


def _ada_kernel(c_ref, w_ref, b_ref, o_ref):
    c = c_ref[...]
    s = (c * jax.nn.sigmoid(c)).astype(BF16)
    o_ref[0] = jnp.dot(s, w_ref[0].astype(BF16), preferred_element_type=F32) + b_ref[0]


def _ada_all(c_all, w_ada, b_ada):
    depth, d, n = w_ada.shape
    bt = c_all.shape[0]
    tn = 1536
    return pl.pallas_call(
        _ada_kernel,
        grid=(depth, n // tn),
        in_specs=[pl.BlockSpec((bt, d), lambda l, j: (0, 0)),
                  pl.BlockSpec((1, d, tn), lambda l, j: (l, 0, j)),
                  pl.BlockSpec((1, 1, tn), lambda l, j: (l, 0, j))],
        out_specs=pl.BlockSpec((1, bt, tn), lambda l, j: (l, 0, j)),
        out_shape=jax.ShapeDtypeStruct((depth, bt, n), F32),
        compiler_params=_cparams(("parallel", "parallel")),
        name="ada_mod",
    )(c_all, w_ada, b_ada.reshape(depth, 1, n))


def _ln_kernel(x_ref, g_ref, b_ref, o_ref):
    o_ref[...] = _layer_norm_rows(x_ref[...], g_ref[...], b_ref[...])


def _ln_in(x2d, g, b):
    n, d = x2d.shape
    tm = min(1024, n)
    return pl.pallas_call(
        _ln_kernel,
        grid=(n // tm,),
        in_specs=[pl.BlockSpec((tm, d), lambda i: (i, 0)),
                  pl.BlockSpec((1, d), lambda i: (0, 0)),
                  pl.BlockSpec((1, d), lambda i: (0, 0))],
        out_specs=pl.BlockSpec((tm, d), lambda i: (i, 0)),
        out_shape=jax.ShapeDtypeStruct((n, d), F32),
        compiler_params=_cparams(("parallel",)),
        name="ln_in",
    )(x2d, g.reshape(1, d), b.reshape(1, d))


def _fold_kernel(uq_ref, uk_ref, uv_ref, wo_ref, qlat_ref, fold_ref):
    a = uq_ref[0, 0]
    b = uk_ref[0, 0]
    ql = lax.dot_general(a, b, (((1,), (1,)), ((), ())), precision=HIGHEST,
                         preferred_element_type=F32)
    qlat_ref[0] = (ql * QUERY_SCALE).astype(BF16)
    fd = jnp.dot(uv_ref[0, 0], wo_ref[0], precision=HIGHEST, preferred_element_type=F32)
    fold_ref[0] = fd.astype(BF16)


def _fold_weights(mla_w_uq, mla_w_uk, mla_w_uv, w_out):
    depth = mla_w_uq.shape[0]
    uq_n = jnp.transpose(mla_w_uq[..., :MLA_NOPE], (0, 2, 1, 3))
    uk_t = jnp.transpose(mla_w_uk, (0, 2, 1, 3))
    uv_t = jnp.transpose(mla_w_uv, (0, 2, 1, 3))
    mla_row0 = (GLA_WIDTH + LRU_WIDTH) // MLA_V
    return pl.pallas_call(
        _fold_kernel,
        grid=(depth, MLA_HEADS),
        in_specs=[pl.BlockSpec((1, 1, MLA_Q_LORA, MLA_NOPE), lambda l, h: (l, h, 0, 0)),
                  pl.BlockSpec((1, 1, MLA_KV_LORA, MLA_NOPE), lambda l, h: (l, h, 0, 0)),
                  pl.BlockSpec((1, 1, MLA_KV_LORA, MLA_V), lambda l, h: (l, h, 0, 0)),
                  pl.BlockSpec((1, MLA_V, D_MODEL), lambda l, h: (l, mla_row0 + h, 0))],
        out_specs=[pl.BlockSpec((1, MLA_Q_LORA, MLA_KV_LORA), lambda l, h: (l, 0, h)),
                   pl.BlockSpec((1, MLA_KV_LORA, D_MODEL), lambda l, h: (l, h, 0))],
        out_shape=[jax.ShapeDtypeStruct((depth, MLA_Q_LORA, MLA_HEADS * MLA_KV_LORA), BF16),
                   jax.ShapeDtypeStruct((depth, MLA_HEADS * MLA_KV_LORA, D_MODEL), BF16)],
        compiler_params=_cparams(("parallel", "parallel")),
        name="fold_weights",
    )(uq_n, uk_t, uv_t, w_out)


def _inproj_kernel(x_ref, ada_ref, w_ref, wg_ref, bg_ref, qn_ref, kvn_ref, wq2_ref, cs_ref,
                   proj_ref, qcat_ref, ckv_ref, kr_ref, kcat_ref):
    bb, tt, d = x_ref.shape
    m = bb * tt
    x = x_ref[...]
    shift = ada_ref[:, 0:1, :]
    scale = ada_ref[:, 1:2, :]
    h = (x * (1.0 + scale) + shift).reshape(m, d).astype(BF16)
    p = jnp.dot(h, w_ref[...], preferred_element_type=F32)
    proj_ref[:, 0:C_DQ] = p[:, 0:C_DQ]

    tail = p[:, C_TAIL:C_TAIL + LANES]
    z = jnp.dot(tail.astype(BF16), wg_ref[...], preferred_element_type=F32) + bg_ref[...]
    proj_ref[:, C_LOGF:C_LOGF + LANES] = jax.nn.log_sigmoid(z) / GLA_GATE_TAU

    cs = cs_ref[...]
    lane = lax.broadcasted_iota(jnp.int32, (m, LANES), 1)
    rope_lanes = lane < MLA_ROPE

    def rope(block):
        r = block * cs
        r = r + pltpu.roll(r, LANES - MLA_ROPE, 1)
        return jnp.where(rope_lanes, r, 0.0)

    kr = rope(tail)
    kr_ref[...] = kr[:, 0:MLA_ROPE]
    ckv = _rms_rows(p[:, C_DKV:C_DKV + MLA_KV_LORA], kvn_ref[...])
    ckv_ref[...] = ckv
    kcat_ref[...] = jnp.concatenate([ckv.astype(BF16), kr.astype(BF16)], axis=1)

    dqn = _rms_rows(p[:, C_DQ:C_DQ + MLA_Q_LORA], qn_ref[...]).astype(BF16)
    q2 = jnp.dot(dqn, wq2_ref[...], preferred_element_type=F32)
    for hh in range(MLA_HEADS):
        lat = q2[:, hh * LANES:(hh + 1) * LANES]
        rp = rope(q2[:, (MLA_HEADS + hh) * LANES:(MLA_HEADS + hh + 1) * LANES])
        qc = jnp.concatenate([lat, rp], axis=1)
        qcat_ref[:, hh] = qc.reshape(bb, tt, QK_WIDTH).astype(qcat_ref.dtype)


def _inproj(x, ada, w_in_p, wg_p, bg, qn, kvn, wq2, cs, bb, tt, layer):
    b, t, d = x.shape
    n = b * t
    m = bb * tt
    grid = (b // bb, t // tt)
    nt = t // tt
    const = lambda i, j: (0, 0)
    row = lambda i, j: (i * nt + j, 0)
    return pl.pallas_call(
        _inproj_kernel,
        grid=grid,
        in_specs=[pl.BlockSpec((bb, tt, d), lambda i, j: (i, j, 0)),
                  pl.BlockSpec((bb, 6, d), lambda i, j: (i, 0, 0)),
                  pl.BlockSpec((None, d, W_IN_COLS), lambda i, j: (layer, 0, 0)),
                  pl.BlockSpec((LANES, LANES), const),
                  pl.BlockSpec((1, LANES), const),
                  pl.BlockSpec((1, MLA_Q_LORA), const),
                  pl.BlockSpec((1, MLA_KV_LORA), const),
                  pl.BlockSpec((None, MLA_Q_LORA, 2 * MLA_HEADS * LANES), lambda i, j: (layer, 0, 0)),
                  pl.BlockSpec((m, LANES), lambda i, j: (j, 0))],
        out_specs=[pl.BlockSpec((m, PROJ_COLS), row),
                   pl.BlockSpec((bb, MLA_HEADS, tt, QK_WIDTH), lambda i, j: (i, 0, j, 0)),
                   pl.BlockSpec((m, MLA_KV_LORA), row),
                   pl.BlockSpec((m, MLA_ROPE), row),
                   pl.BlockSpec((m, QK_WIDTH), row)],
        out_shape=[jax.ShapeDtypeStruct((n, PROJ_COLS), F32),
                   jax.ShapeDtypeStruct((b, MLA_HEADS, t, QK_WIDTH), BF16 if tt % 16 == 0 else F32),
                   jax.ShapeDtypeStruct((n, MLA_KV_LORA), F32),
                   jax.ShapeDtypeStruct((n, MLA_ROPE), F32),
                   jax.ShapeDtypeStruct((n, QK_WIDTH), BF16)],
        compiler_params=_cparams(("parallel", "parallel")),
        name="in_proj",
    )(x, ada, w_in_p, wg_p, bg, qn, kvn, wq2, cs)


def _gla_kernel(q_ref, k_ref, v_ref, gg_ref, g_ref, s0_ref, ng_ref, ltri_ref, ind_ref,
                msk_ref, seg_ref, o_ref, sT_ref, st_sc, b_sc, o_sc, *, nb, tt, c):
    tj = pl.program_id(1)
    n_chunks = tt // c
    groups = c // SUBLANES
    ltri = ltri_ref[...]
    ind = ind_ref[...]
    msk = msk_ref[...]
    row_iota = lax.broadcasted_iota(jnp.int32, (c, LANES), 0)

    on_diag = ind.astype(F32) > 0.5

    @pl.when(tj == 0)
    def _():
        for j in range(nb):
            s_cat = jnp.concatenate([s0_ref[j, hh] for hh in range(GLA_HEADS)], axis=1)
            s_bd = jnp.where(on_diag, jnp.concatenate([s_cat] * GLA_HEADS, axis=0), 0.0)
            st_sc[j] = s_bd.T

    def chunk(ci, carry):
        for j in range(nb):
            r0 = pl.multiple_of(ci * c, c)
            g = g_ref[j, pl.ds(r0, c), :]
            b = jnp.dot(ltri, g, precision=HIGHEST, preferred_element_type=F32)
            b_sc[j] = b
            q = q_ref[j, pl.ds(r0, c), :]
            k = k_ref[j, pl.ds(r0, c), :]
            v = v_ref[j, pl.ds(r0, c), :]
            blast = b_sc[j, pl.ds(c - 1, 1), :]
            qe = q * jnp.exp(b)
            ke = k * jnp.exp(blast - b)
            st = st_sc[j]
            o_inter = lax.dot_general(qe.astype(BF16), st.astype(BF16), (((1,), (1,)), ((), ())),
                                      preferred_element_type=F32)
            ut = lax.dot_general(v.astype(BF16), ke.astype(BF16), (((0,), (0,)), ((), ())),
                                 preferred_element_type=F32)
            st_sc[j] = st * jnp.exp(blast) + ut * msk

            o_blk = [None] * groups
            for g0 in range(groups):
                lo = g0 * SUBLANES
                pieces = []
                for s in range(lo, lo + SUBLANES):
                    bs = b_sc[j, pl.ds(s, 1), :]
                    ks = k_ref[j, pl.ds(r0 + s, 1), :]
                    e = jnp.exp(b[lo:, :] - bs)
                    head = jnp.where(row_iota[lo:lo + SUBLANES, :] >= s, e[0:SUBLANES, :], 0.0)
                    e = head if c - lo == SUBLANES else jnp.concatenate([head, e[SUBLANES:, :]], axis=0)
                    pieces.append(e * q[lo:, :] * ks)
                w = jnp.concatenate(pieces, axis=0).astype(BF16)
                a = jnp.dot(w, ind, preferred_element_type=F32)
                rows = c - lo
                for idx in range(SUBLANES):
                    vs = v_ref[j, pl.ds(r0 + lo + idx, 1), :]
                    for rb in range(g0, groups):
                        piece = a[idx * rows + (rb - g0) * SUBLANES:
                                  idx * rows + (rb - g0 + 1) * SUBLANES, :] * vs
                        o_blk[rb] = piece if o_blk[rb] is None else o_blk[rb] + piece
            o_intra = jnp.concatenate(o_blk, axis=0) if groups > 1 else o_blk[0]
            o_sc[j, pl.ds(r0, c), :] = o_inter + o_intra
        return carry

    lax.fori_loop(0, n_chunks, chunk, 0)

    @pl.when(tj == pl.num_programs(1) - 1)
    def _():
        for j in range(nb):
            s_bd = st_sc[j].T
            for hh in range(GLA_HEADS):
                sT_ref[j, hh] = s_bd[hh * GLA_DK:(hh + 1) * GLA_DK, hh * GLA_DV:(hh + 1) * GLA_DV]

    ng = ng_ref[...]
    seg = seg_ref[...]

    def epi(j, carry):
        o = o_sc[j]
        ms = jnp.dot(o * o, seg, precision=HIGHEST, preferred_element_type=F32)
        gg = gg_ref[j]
        o_ref[j] = o * lax.rsqrt(ms + RMS_EPS) * ng * (gg * jax.nn.sigmoid(gg))
        return carry

    lax.fori_loop(0, nb, epi, 0)


def _gla(proj, s0T, ng, b, t, nb, tt):
    n = b * t
    c = min(GLA_CHUNK, t)
    proj3 = proj.reshape(b, t, PROJ_COLS)
    ltri = jnp.asarray(np.tril(np.ones((c, c), np.float32)))
    hk = np.arange(GLA_HEADS * GLA_DK) // GLA_DK
    hv = np.arange(GLA_WIDTH) // GLA_DV
    ind = jnp.asarray((hk[:, None] == hv[None, :]).astype(np.float32)).astype(BF16)
    msk = jnp.asarray((hv[:, None] == hk[None, :]).astype(np.float32))
    seg = jnp.asarray((hv[:, None] == hv[None, :]).astype(np.float32) / GLA_DV)
    const = lambda i, j: (0, 0)
    kern = functools.partial(_gla_kernel, nb=nb, tt=tt, c=c)

    def cols(width, col0):
        return pl.BlockSpec((nb, tt, width), lambda i, j: (i, j, col0 // width))

    o_gla, sT = pl.pallas_call(
        kern,
        grid=(b // nb, t // tt),
        in_specs=[cols(LANES, C_Q), cols(LANES, C_K), cols(GLA_WIDTH, C_V), cols(GLA_WIDTH, C_GG),
                  cols(LANES, C_LOGF),
                  pl.BlockSpec((nb, GLA_HEADS, GLA_DK, GLA_DV), lambda i, j: (i, 0, 0, 0)),
                  pl.BlockSpec((1, GLA_WIDTH), const),
                  pl.BlockSpec((c, c), const),
                  pl.BlockSpec((LANES, GLA_WIDTH), const),
                  pl.BlockSpec((GLA_WIDTH, LANES), const),
                  pl.BlockSpec((GLA_WIDTH, GLA_WIDTH), const)],
        out_specs=[pl.BlockSpec((nb, tt, GLA_WIDTH), lambda i, j: (i, j, 0)),
                   pl.BlockSpec((nb, GLA_HEADS, GLA_DK, GLA_DV), lambda i, j: (i, 0, 0, 0))],
        out_shape=[jax.ShapeDtypeStruct((b, t, GLA_WIDTH), F32),
                   jax.ShapeDtypeStruct((b, GLA_HEADS, GLA_DK, GLA_DV), F32)],
        scratch_shapes=[pltpu.VMEM((nb, GLA_WIDTH, LANES), F32),
                        pltpu.VMEM((nb, c, LANES), F32),
                        pltpu.VMEM((nb, tt, GLA_WIDTH), F32)],
        compiler_params=_cparams(("parallel", "arbitrary")),
        name="gla",
    )(proj3, proj3, proj3, proj3, proj3, s0T, ng, ltri, ind, msk, seg)
    return o_gla.reshape(n, GLA_WIDTH), sT


def _lru_kernel(lx_ref, lg_ref, cbuf_ref, h0_ref, cw_ref, cb_ref, wax_ref, bax_ref, sp_ref,
                o_ref, hn_ref, cn_ref, xp_sc, a_sc, u_sc, *, nb, t):
    pad = SUBLANES
    hist = CONV_WIDTH - 1
    row8 = lax.broadcasted_iota(jnp.int32, (SUBLANES, LRU_WIDTH), 0)
    rows_per_iter = min(32, t)
    sub = rows_per_iter // SUBLANES

    def scan_block(a, u):
        for dd in (1, 2, 4):
            a_s = jnp.where(row8 >= dd, pltpu.roll(a, dd, 0), 1.0)
            u_s = jnp.where(row8 >= dd, pltpu.roll(u, dd, 0), 0.0)
            u = a * u_s + u
            a = a * a_s
        return a, u

    for j in range(nb):
        base = j * t
        xp_sc[pl.ds(0, pad), :] = jnp.zeros((pad, LRU_WIDTH), F32)
        xp_sc[pl.ds(pad - hist, hist), :] = cbuf_ref[j]
        xp_sc[pl.ds(pad, t), :] = lx_ref[pl.ds(base, t), :]
        xc = cb_ref[...]
        for kk in range(CONV_WIDTH):
            xc = xc + xp_sc[pl.ds(pad - hist + kk, t), :] * cw_ref[pl.ds(kk, 1), :]
        cn_ref[j] = xp_sc[pl.ds(pad + t - hist, hist), :]

        ax = jnp.dot(xc.astype(BF16), wax_ref[...], preferred_element_type=F32) + bax_ref[...]
        r = jax.nn.sigmoid(ax[:, 0:LRU_WIDTH])
        ig = jax.nn.sigmoid(ax[:, LRU_WIDTH:2 * LRU_WIDTH])
        log_a = -LRU_C * r * sp_ref[...]
        a = jnp.exp(log_a)
        u = jnp.sqrt((a * a + 1.0) * jnp.tanh(-log_a)) * (ig * xc)
        a_sc[...] = a
        u_sc[...] = u
        u_sc[pl.ds(0, 1), :] = u[0:1, :] + a[0:1, :] * h0_ref[j]

        def step(i, hprev):
            r0 = pl.multiple_of(i * rows_per_iter, rows_per_iter)
            scans = []
            for sb in range(sub):
                ab = a_sc[pl.ds(r0 + sb * SUBLANES, SUBLANES), :]
                ub = u_sc[pl.ds(r0 + sb * SUBLANES, SUBLANES), :]
                scans.append(scan_block(ab, ub))
            for sb in range(sub):
                ac, uc = scans[sb]
                hb = ac * hprev + uc
                u_sc[pl.ds(r0 + sb * SUBLANES, SUBLANES), :] = hb
                hprev = hb[SUBLANES - 1:SUBLANES, :]
            return hprev

        hlast = lax.fori_loop(0, t // rows_per_iter, step, jnp.zeros((1, LRU_WIDTH), F32))
        hn_ref[j] = hlast
        lg = lg_ref[pl.ds(base, t), :]
        o_ref[pl.ds(base, t), :] = jax.nn.gelu(lg, approximate=True) * u_sc[...]


def _lru(proj, cbuf, h0, cw, cb, wax, bax, sp, b, t, nb):
    n = b * t
    rows = nb * t
    const = lambda i: (0, 0)
    kern = functools.partial(_lru_kernel, nb=nb, t=t)
    return pl.pallas_call(
        kern,
        grid=(b // nb,),
        in_specs=[pl.BlockSpec((rows, LRU_WIDTH), lambda i: (i, C_LX // LRU_WIDTH)),
                  pl.BlockSpec((rows, LRU_WIDTH), lambda i: (i, C_LG // LRU_WIDTH)),
                  pl.BlockSpec((nb, CONV_WIDTH - 1, LRU_WIDTH), lambda i: (i, 0, 0)),
                  pl.BlockSpec((nb, 1, LRU_WIDTH), lambda i: (i, 0, 0)),
                  pl.BlockSpec((CONV_WIDTH, LRU_WIDTH), const),
                  pl.BlockSpec((1, LRU_WIDTH), const),
                  pl.BlockSpec((LRU_WIDTH, 2 * LRU_WIDTH), const),
                  pl.BlockSpec((1, 2 * LRU_WIDTH), const),
                  pl.BlockSpec((1, LRU_WIDTH), const)],
        out_specs=[pl.BlockSpec((rows, LRU_WIDTH), lambda i: (i, 0)),
                   pl.BlockSpec((nb, 1, LRU_WIDTH), lambda i: (i, 0, 0)),
                   pl.BlockSpec((nb, CONV_WIDTH - 1, LRU_WIDTH), lambda i: (i, 0, 0))],
        out_shape=[jax.ShapeDtypeStruct((n, LRU_WIDTH), F32),
                   jax.ShapeDtypeStruct((b, 1, LRU_WIDTH), F32),
                   jax.ShapeDtypeStruct((b, CONV_WIDTH - 1, LRU_WIDTH), F32)],
        scratch_shapes=[pltpu.VMEM((t + SUBLANES, LRU_WIDTH), F32),
                        pltpu.VMEM((t, LRU_WIDTH), F32),
                        pltpu.VMEM((t, LRU_WIDTH), F32)],
        compiler_params=_cparams(("parallel",)),
        name="rg_lru",
    )(proj, proj, cbuf, h0, cw, cb, wax, bax, sp)


def _softmax_step(s, m_ref, l_ref, acc_ref, v):
    shape3 = m_ref.shape
    rows = shape3[0] * shape3[1]
    reps = s.shape[1] // LANES
    m_prev = m_ref[...].reshape(rows, LANES)
    m_next = jnp.maximum(m_prev, jnp.max(s, axis=1, keepdims=True))
    alpha = jnp.exp2(m_prev - m_next)
    m_wide = m_next if reps == 1 else jnp.concatenate([m_next] * reps, axis=1)
    p = jnp.exp2(s - m_wide)
    l_next = alpha * l_ref[...].reshape(rows, LANES) + jnp.sum(p, axis=1, keepdims=True)
    acc_next = alpha * acc_ref[...].reshape(rows, LANES) + jnp.dot(p.astype(BF16), v,
                                                                    preferred_element_type=F32)
    l_ref[...] = l_next.reshape(shape3)
    acc_ref[...] = acc_next.reshape(shape3)
    m_ref[...] = m_next.reshape(shape3)


def _causal_mask(s, n):
    tpos = lax.broadcasted_iota(jnp.int32, s.shape, 0) & (n - 1)
    kpos = lax.broadcasted_iota(jnp.int32, s.shape, 1)
    return jnp.where(kpos <= tpos, s, -jnp.inf)


def _attn_kernel(q_ref, k_ref, o_ref, m_sc, l_sc, acc_sc, *, tq):
    i = pl.program_id(1)
    half = tq // 2
    m_sc[...] = jnp.full(m_sc.shape, -jnp.inf, F32)
    l_sc[...] = jnp.zeros(l_sc.shape, F32)
    acc_sc[...] = jnp.zeros(acc_sc.shape, F32)
    nt = (((1,), (1,)), ((), ()))

    def body(j, carry):
        kb = k_ref[0, pl.ds(pl.multiple_of(j * tq, tq), tq), :]
        q = q_ref[0].reshape(MLA_HEADS * tq, QK_WIDTH)
        s = lax.dot_general(q, kb, nt, preferred_element_type=F32)
        _softmax_step(s, m_sc, l_sc, acc_sc, kb[:, 0:MLA_KV_LORA])
        return carry

    lax.fori_loop(0, i, body, 0)

    k0 = pl.multiple_of(i * tq, tq)
    kb_a = k_ref[0, pl.ds(k0, half), :]
    q = q_ref[0].reshape(MLA_HEADS * tq, QK_WIDTH)
    s_a = lax.dot_general(q, kb_a, nt, preferred_element_type=F32)
    tpos = lax.broadcasted_iota(jnp.int32, s_a.shape, 0) & (tq - 1)
    kpos = lax.broadcasted_iota(jnp.int32, s_a.shape, 1)
    s_a = jnp.where(kpos <= tpos, s_a, -jnp.inf)
    _softmax_step(s_a, m_sc, l_sc, acc_sc, kb_a[:, 0:MLA_KV_LORA])

    kb_b = k_ref[0, pl.ds(k0 + half, half), :]
    late = pl.ds(half, half)
    q_b = q_ref[0, :, late, :].reshape(MLA_HEADS * half, QK_WIDTH)
    s_b = _causal_mask(lax.dot_general(q_b, kb_b, nt, preferred_element_type=F32), half)
    _softmax_step(s_b, m_sc.at[:, late, :], l_sc.at[:, late, :], acc_sc.at[:, late, :],
                  kb_b[:, 0:MLA_KV_LORA])

    for hh in range(MLA_HEADS):
        o_ref[0, :, hh * MLA_KV_LORA:(hh + 1) * MLA_KV_LORA] = (acc_sc[hh] / l_sc[hh]).astype(BF16)


def _attn_prompt(qcat, kcat, b, t):
    tq = min(ATTN_BLOCK, t)
    kern = functools.partial(_attn_kernel, tq=tq)
    stat = pltpu.VMEM((MLA_HEADS, tq, LANES), F32)
    return pl.pallas_call(
        kern,
        grid=(b, t // tq),
        in_specs=[pl.BlockSpec((1, MLA_HEADS, tq, QK_WIDTH), lambda bi, i: (bi, 0, i, 0)),
                  pl.BlockSpec((1, t, QK_WIDTH), lambda bi, i: (bi, 0, 0))],
        out_specs=pl.BlockSpec((1, tq, MLA_HEADS * MLA_KV_LORA), lambda bi, i: (bi, i, 0)),
        out_shape=jax.ShapeDtypeStruct((b, t, MLA_HEADS * MLA_KV_LORA), BF16),
        scratch_shapes=[stat, stat, stat],
        compiler_params=_cparams(("parallel", "parallel")),
        name="attn_prompt",
    )(qcat, kcat.reshape(b, t, QK_WIDTH))


PAGE_SLOTS = 3
PAGED_SUB_KEYS = 1024
PAGED_LAG = 2


def _attn_paged_kernel(pt_ref, q_ref, cnew_ref, rnew_ref, ckv_hbm, krt_hbm, o_ref,
                       kc_buf, kr_buf, sem, m_sc, l_sc, acc_sc, *, pages, ts, layer, steps, n_steps):
    j = pl.program_id(1)
    step = pl.program_id(0) * steps + j
    r = MLA_HEADS * ts

    def page_copy(src_page, slot, i):
        return (pltpu.make_async_copy(ckv_hbm.at[layer, src_page],
                                      kc_buf.at[slot, pl.ds(i * PAGE_SIZE, PAGE_SIZE), :],
                                      sem.at[slot, 0]),
                pltpu.make_async_copy(krt_hbm.at[layer, src_page],
                                      kr_buf.at[slot, :, pl.ds(i * PAGE_SIZE, PAGE_SIZE)],
                                      sem.at[slot, 1]))

    def start_step(step_id, slot):
        seq = step_id // steps
        first = (step_id - seq * steps) * pages
        for i in range(pages):
            for cp in page_copy(pt_ref[seq, first + i], slot, i):
                cp.start(priority=i % 2)

    @pl.when(step == 0)
    def _():
        for ahead in range(min(PAGE_SLOTS - 1, n_steps)):
            start_step(ahead, ahead)

    @pl.when(j == 0)
    def _():
        m_sc[...] = jnp.full(m_sc.shape, -jnp.inf, F32)
        l_sc[...] = jnp.zeros(l_sc.shape, F32)
        acc_sc[...] = jnp.zeros(acc_sc.shape, F32)

    slot = lax.rem(step, PAGE_SLOTS)
    for i in range(pages):
        for cp in page_copy(0, slot, i):
            cp.wait()

    q = q_ref[0].reshape(r, QK_WIDTH).astype(BF16)
    q_lat = q[:, 0:MLA_KV_LORA]
    q_rope = q[:, MLA_KV_LORA:MLA_KV_LORA + MLA_ROPE]
    sub = min(PAGED_SUB_KEYS, pages * PAGE_SIZE)
    n_sub = (pages * PAGE_SIZE) // sub

    def scores(ci):
        kc = kc_buf[slot, pl.ds(ci * sub, sub), :].astype(BF16)
        krt = kr_buf[slot, :, pl.ds(ci * sub, sub)].astype(BF16)
        s = (lax.dot_general(q_lat, kc, (((1,), (1,)), ((), ())), preferred_element_type=F32)
             + jnp.dot(q_rope, krt, preferred_element_type=F32))
        m_c = jnp.broadcast_to(jnp.max(s, axis=1, keepdims=True), (r, LANES))
        p = jnp.exp2(s - jnp.concatenate([m_c] * (sub // LANES), axis=1))
        l_c = jnp.broadcast_to(jnp.sum(p, axis=1, keepdims=True), (r, LANES))
        return m_c, l_c, p.astype(BF16), kc

    staged, parts = [], []
    for ci in range(n_sub + PAGED_LAG):
        if ci < n_sub:
            staged.append(scores(ci))
        if ci >= PAGED_LAG:
            m_c, l_c, p_c, kc_c = staged[ci - PAGED_LAG]
            parts.append((m_c, l_c, jnp.dot(p_c, kc_c, preferred_element_type=F32)))
    stat3 = m_sc.shape
    m_prev = m_sc[...].reshape(r, LANES)
    m_next = m_prev
    for m_c, _, _ in parts:
        m_next = jnp.maximum(m_next, m_c)
    alpha = jnp.exp2(m_prev - m_next)
    l_next = alpha * l_sc[...].reshape(r, LANES)
    acc_next = alpha * acc_sc[...].reshape(r, LANES)
    for m_c, l_c, acc_c in parts:
        w_c = jnp.exp2(m_c - m_next)
        l_next = l_next + w_c * l_c
        acc_next = acc_next + w_c * acc_c
    m_sc[...] = m_next.reshape(stat3)
    l_sc[...] = l_next.reshape(stat3)
    acc_sc[...] = acc_next.reshape(stat3)

    nxt = step + (PAGE_SLOTS - 1)

    @pl.when(nxt < n_steps)
    def _():
        start_step(nxt, lax.rem(nxt, PAGE_SLOTS))

    @pl.when(j == steps - 1)
    def _():
        pad = jnp.zeros((LANES - ts, MLA_KV_LORA), F32)
        kc_new = jnp.concatenate([cnew_ref[...], pad], axis=0).astype(BF16)
        kr_new = jnp.concatenate([rnew_ref[...], pad[:, 0:MLA_ROPE]], axis=0).astype(BF16)
        s_new = (lax.dot_general(q_lat, kc_new, (((1,), (1,)), ((), ())), preferred_element_type=F32)
                 + lax.dot_general(q_rope, kr_new, (((1,), (1,)), ((), ())),
                                   preferred_element_type=F32))
        _softmax_step(_causal_mask(s_new, ts), m_sc, l_sc, acc_sc, kc_new)
        for hh in range(MLA_HEADS):
            o_ref[0, :, hh * MLA_KV_LORA:(hh + 1) * MLA_KV_LORA] = acc_sc[hh] / l_sc[hh]


def _attn_paged(qcat, ckv_new, kr_new, cache_ckv, cache_krope_t, page_table, layer, pages):
    b, _, ts, _ = qcat.shape
    n_pages = page_table.shape[1]
    steps = n_pages // pages
    kern = functools.partial(_attn_paged_kernel, pages=pages, ts=ts, layer=layer, steps=steps,
                             n_steps=b * steps)
    stat = pltpu.VMEM((MLA_HEADS, ts, LANES), F32)
    grid_spec = pltpu.PrefetchScalarGridSpec(
        num_scalar_prefetch=1,
        grid=(b, steps),
        in_specs=[pl.BlockSpec((1, MLA_HEADS, ts, QK_WIDTH), lambda bi, j, pt: (bi, 0, 0, 0)),
                  pl.BlockSpec((ts, MLA_KV_LORA), lambda bi, j, pt: (bi, 0)),
                  pl.BlockSpec((ts, MLA_ROPE), lambda bi, j, pt: (bi, 0)),
                  pl.BlockSpec(memory_space=pl.ANY),
                  pl.BlockSpec(memory_space=pl.ANY)],
        out_specs=pl.BlockSpec((1, ts, MLA_HEADS * MLA_KV_LORA), lambda bi, j, pt: (bi, 0, 0)),
        scratch_shapes=[pltpu.VMEM((PAGE_SLOTS, pages * PAGE_SIZE, MLA_KV_LORA), F32),
                        pltpu.VMEM((PAGE_SLOTS, MLA_ROPE, pages * PAGE_SIZE), F32),
                        pltpu.SemaphoreType.DMA((PAGE_SLOTS, 2)),
                        stat, stat, stat])
    return pl.pallas_call(
        kern,
        grid_spec=grid_spec,
        out_shape=jax.ShapeDtypeStruct((b, ts, MLA_HEADS * MLA_KV_LORA), F32),
        compiler_params=_cparams(("arbitrary", "arbitrary")),
        name="attn_paged",
    )(page_table, qcat, ckv_new, kr_new, cache_ckv, cache_krope_t)


FFN_CHUNK = 256


def _mix_ffn_kernel(x_ref, ada_ref, og_ref, ol_ref, oa_ref, wgl_ref, wf_ref, g1_ref, b1_ref,
                    wgu_ref, wd_ref, g2_ref, b2_ref, o_ref, *, alpha):
    bb, tt, d = x_ref.shape
    m = bb * tt
    gate1 = ada_ref[:, 2:3, :]
    shift2 = ada_ref[:, 3:4, :]
    scale2 = ada_ref[:, 4:5, :]
    gate2 = ada_ref[:, 5:6, :]
    ogl = jnp.concatenate([og_ref[...], ol_ref[...]], axis=1).astype(BF16)
    mix = (jnp.dot(ogl, wgl_ref[...], preferred_element_type=F32)
           + jnp.dot(oa_ref[...].astype(BF16), wf_ref[...], preferred_element_type=F32))
    x1 = _layer_norm_rows(alpha * x_ref[...] + gate1 * mix.reshape(bb, tt, d), g1_ref[...], b1_ref[...])

    h2 = (x1 * (1.0 + scale2) + shift2).reshape(m, d).astype(BF16)
    acc = jnp.zeros((m, d), F32)
    for ci in range(D_FF // FFN_CHUNK):
        lo = ci * FFN_CHUNK
        gf = jnp.dot(h2, wgu_ref[:, lo:lo + FFN_CHUNK], preferred_element_type=F32)
        uf = jnp.dot(h2, wgu_ref[:, D_FF + lo:D_FF + lo + FFN_CHUNK], preferred_element_type=F32)
        act = (gf * jax.nn.sigmoid(gf) * uf).astype(BF16)
        acc = acc + jnp.dot(act, wd_ref[lo:lo + FFN_CHUNK, :], preferred_element_type=F32)
    y = alpha * x1 + gate2 * acc.reshape(bb, tt, d)
    o_ref[...] = _layer_norm_rows(y, g2_ref[...], b2_ref[...])


def _mix_ffn(x, ada, o_gla, o_lru, o_lat, wgl, wfold, ln1, wgu, wd, ln2, bb, tt, alpha, layer):
    b, t, d = x.shape
    nt = t // tt
    m = bb * tt
    row = lambda i, j: (i * nt + j, 0)

    def resident(shape):
        return pl.BlockSpec(shape, lambda i, j: (0, 0), pipeline_mode=pl.Buffered(1))

    def layer_slab(rows, cols):
        return pl.BlockSpec((None, rows, cols), lambda i, j: (layer, 0, 0), pipeline_mode=pl.Buffered(1))

    kern = functools.partial(_mix_ffn_kernel, alpha=alpha)
    return pl.pallas_call(
        kern,
        grid=(b // bb, nt),
        in_specs=[pl.BlockSpec((bb, tt, d), lambda i, j: (i, j, 0)),
                  pl.BlockSpec((bb, 6, d), lambda i, j: (i, 0, 0)),
                  pl.BlockSpec((m, GLA_WIDTH), row),
                  pl.BlockSpec((m, LRU_WIDTH), row),
                  pl.BlockSpec((m, MLA_HEADS * MLA_KV_LORA), row),
                  layer_slab(GLA_WIDTH + LRU_WIDTH, d),
                  layer_slab(MLA_HEADS * MLA_KV_LORA, d),
                  resident((1, d)), resident((1, d)),
                  layer_slab(d, 2 * D_FF),
                  layer_slab(D_FF, d),
                  resident((1, d)), resident((1, d))],
        out_specs=pl.BlockSpec((bb, tt, d), lambda i, j: (i, j, 0)),
        out_shape=jax.ShapeDtypeStruct((b, t, d), F32),
        compiler_params=_cparams(("parallel", "parallel")),
        name="mix_ffn",
    )(x, ada, o_gla, o_lru, o_lat, wgl, wfold, ln1[0], ln1[1], wgu, wd, ln2[0], ln2[1])


def _rotate_half_cols(w):
    half = MLA_ROPE // 2
    return jnp.concatenate([-w[..., half:], w[..., :half]], axis=-1)


def _prep_stacked_weights(w_in, mla_w_uq, w_qlat):
    depth, d, _ = w_in.shape
    o = np.cumsum([0, 128, 128, 256, 256, 16, 256, 256, 256, 128, 32])
    gq, gk, gv, gg, glr, lx, lgt, dq, dkv, kr = [w_in[:, :, o[i]:o[i + 1]] for i in range(10)]
    tail = jnp.concatenate([kr, _rotate_half_cols(kr), glr,
                            jnp.zeros((depth, d, LANES - 2 * MLA_ROPE - GLA_LOWRANK), F32)], axis=2)
    w_in_p = jnp.concatenate([gq * (GLA_DK ** -0.5), gk, gv, gg, lx, lgt, dq, dkv, tail],
                             axis=2).astype(BF16)
    rope_w = mla_w_uq[:, :, :, MLA_NOPE:] * QUERY_SCALE
    rope_blk = jnp.concatenate(
        [rope_w, _rotate_half_cols(rope_w),
         jnp.zeros((depth, MLA_Q_LORA, MLA_HEADS, LANES - 2 * MLA_ROPE), F32)], axis=-1)
    wq2 = jnp.concatenate(
        [w_qlat, rope_blk.reshape(depth, MLA_Q_LORA, MLA_HEADS * LANES).astype(BF16)], axis=2)
    return w_in_p, wq2


def _prep_layer_weights(l, gla_w_gate, gla_b_gate, gla_norm_g, lru_conv_w, lru_conv_b, lru_w_a,
                        lru_b_a, lru_w_x, lru_b_x, lru_lambda, mla_q_norm_g, mla_kv_norm_g):
    wg_p = jnp.pad(gla_w_gate[l], ((2 * MLA_ROPE, LANES - 2 * MLA_ROPE - GLA_LOWRANK),
                                   (0, 0))).astype(BF16)
    bg = gla_b_gate[l].reshape(1, LANES)
    ng = jnp.tile(gla_norm_g[l], GLA_HEADS).reshape(1, GLA_WIDTH)

    def block_diag(wb):
        on_diag = jnp.eye(LRU_BLOCKS, dtype=bool)[:, None, :, None]
        return jnp.where(on_diag, wb[:, :, None, :], 0.0).reshape(LRU_WIDTH, LRU_WIDTH)

    wax = jnp.concatenate([block_diag(lru_w_a[l]), block_diag(lru_w_x[l])], axis=1).astype(BF16)
    bax = jnp.concatenate([lru_b_a[l], lru_b_x[l]]).reshape(1, 2 * LRU_WIDTH)
    sp = jax.nn.softplus(-lru_lambda[l].astype(F32)).reshape(1, LRU_WIDTH)
    return dict(
        wg_p=wg_p, bg=bg, ng=ng,
        qn=mla_q_norm_g[l].reshape(1, MLA_Q_LORA), kvn=mla_kv_norm_g[l].reshape(1, MLA_KV_LORA),
        cw=lru_conv_w[l], cb=lru_conv_b[l].reshape(1, LRU_WIDTH), wax=wax, bax=bax, sp=sp)


def _rope_table(pos):
    half = MLA_ROPE // 2
    inv_freq = ROPE_THETA ** (-jnp.arange(half, dtype=F32) / half)
    ang = pos.astype(F32)[:, None] * inv_freq[None, :]
    cos, sin = jnp.cos(ang), jnp.sin(ang)
    return jnp.concatenate([cos, cos, sin, sin,
                            jnp.zeros((pos.shape[0], LANES - 2 * MLA_ROPE), F32)], axis=1)


def _group_layer(l, x, ada, lw, big, cs, s0, h0, cbuf, tiles, alpha, ln1, ln2, attend):
    b, t, _ = x.shape
    bb, tt, nb, gla_nb, gla_tt = tiles
    proj, qcat, ckv_new, kr_new, kcat = _inproj(x, ada, big["w_in_p"], lw["wg_p"], lw["bg"], lw["qn"],
                                                lw["kvn"], big["wq2"], cs, bb, tt, l)
    o_gla, s_new = _gla(proj, s0, lw["ng"], b, t, gla_nb, gla_tt)
    o_lru, h_new, conv_new = _lru(proj, cbuf, h0, lw["cw"], lw["cb"], lw["wax"], lw["bax"], lw["sp"],
                                  b, t, nb)
    o_lat = attend(qcat, kcat, ckv_new, kr_new)
    x2 = _mix_ffn(x, ada, o_gla, o_lru, o_lat.reshape(b * t, -1), big["w_out"], big["w_fold"], ln1,
                  big["wgu"], big["wd"], ln2, bb, tt, alpha, l)
    states = (s_new, h_new.reshape(b, LRU_WIDTH), conv_new,
              ckv_new.reshape(b, t, MLA_KV_LORA), kr_new.reshape(b, t, MLA_ROPE))
    return x2, states


def kernel(x_prompt, x_sample, c_prompt, c_sample, state_gla, state_lru, state_conv, cache_ckv, cache_krope, page_table, ln_in_g, ln_in_b, w_ada, b_ada, w_in, gla_w_gate, gla_b_gate, gla_norm_g, lru_conv_w, lru_conv_b, lru_w_a, lru_b_a, lru_w_x, lru_b_x, lru_lambda, mla_q_norm_g, mla_w_uq, mla_kv_norm_g, mla_w_uk, mla_w_uv, w_out, ln1_g, ln1_b, ffn_w_gu, ffn_w_down, ln2_g, ln2_b):
    bp, tp, d = x_prompt.shape
    bs, ts, _ = x_sample.shape
    depth = w_in.shape[0]
    n_pages = page_table.shape[1]
    past_len = n_pages * PAGE_SIZE
    alpha = (2.0 * depth) ** 0.25

    tiles_p = (1, min(512, tp), 2 if bp % 2 == 0 else 1, 8 if bp % 8 == 0 else 1, min(256, tp))
    tiles_s = (bs, ts, 4 if bs % 4 == 0 else 1, 8 if bs % 8 == 0 else 1, ts)
    pages_per_step = 64 if n_pages % 64 == 0 else n_pages

    ada = _ada_all(jnp.concatenate([c_prompt, c_sample], axis=0), w_ada, b_ada)
    ada = ada.reshape(depth, bp + bs, 6, d)
    xp = _ln_in(x_prompt.reshape(bp * tp, d), ln_in_g, ln_in_b).reshape(bp, tp, d)
    xs = _ln_in(x_sample.reshape(bs * ts, d), ln_in_g, ln_in_b).reshape(bs, ts, d)
    w_qlat, w_fold = _fold_weights(mla_w_uq, mla_w_uk, mla_w_uv, w_out)

    cache_krope_t = jnp.swapaxes(cache_krope, 2, 3)
    cs_p = _rope_table(jnp.arange(tp, dtype=jnp.int32))
    cs_s = jnp.tile(_rope_table(past_len + jnp.arange(ts, dtype=jnp.int32)), (bs, 1))
    zero_s = jnp.zeros((bp, GLA_HEADS, GLA_DK, GLA_DV), F32)
    zero_h = jnp.zeros((bp, 1, LRU_WIDTH), F32)
    zero_conv = jnp.zeros((bp, CONV_WIDTH - 1, LRU_WIDTH), F32)
    w_in_p, wq2 = _prep_stacked_weights(w_in, mla_w_uq, w_qlat)
    big = dict(w_in_p=w_in_p, wq2=wq2, w_out=w_out.astype(BF16), w_fold=w_fold,
               wgu=ffn_w_gu.astype(BF16), wd=ffn_w_down.astype(BF16))

    st_p, st_s = [], []
    for l in range(depth):
        lw = _prep_layer_weights(l, gla_w_gate, gla_b_gate, gla_norm_g, lru_conv_w, lru_conv_b,
                                 lru_w_a, lru_b_a, lru_w_x, lru_b_x, lru_lambda, mla_q_norm_g,
                                 mla_kv_norm_g)
        ln1 = (ln1_g[l].reshape(1, d), ln1_b[l].reshape(1, d))
        ln2 = (ln2_g[l].reshape(1, d), ln2_b[l].reshape(1, d))

        def attend_p(qcat, kcat, ckv_new, kr_new):
            return _attn_prompt(qcat, kcat, bp, tp)

        def attend_s(qcat, kcat, ckv_new, kr_new, l=l):
            return _attn_paged(qcat, ckv_new, kr_new, cache_ckv, cache_krope_t, page_table, l,
                               pages_per_step)

        xp, sp = _group_layer(l, xp, ada[l, :bp], lw, big, cs_p, zero_s, zero_h, zero_conv,
                              tiles_p, alpha, ln1, ln2, attend_p)
        xs, ss = _group_layer(l, xs, ada[l, bp:], lw, big, cs_s, state_gla[l],
                              state_lru[l].reshape(bs, 1, LRU_WIDTH), state_conv[l],
                              tiles_s, alpha, ln1, ln2, attend_s)
        st_p.append(sp)
        st_s.append(ss)

    def stk(outs, j):
        return jnp.stack([o[j] for o in outs])

    return (xp, xs, stk(st_p, 0), stk(st_s, 0), stk(st_p, 1), stk(st_s, 1), stk(st_p, 2), stk(st_s, 2),
            stk(st_p, 3), stk(st_s, 3), stk(st_p, 4), stk(st_s, 4))
```

```python
import functools
import math

import numpy as np
import jax
import jax.numpy as jnp
from jax import lax
from jax.experimental import pallas as pl
from jax.experimental.pallas import tpu as pltpu

F32 = jnp.float32
BF16 = jnp.bfloat16
HIGHEST = lax.Precision.HIGHEST

D_MODEL = 1024
PAGE_SIZE = 128
GLA_HEADS = 4
GLA_DK = 32
GLA_DV = 64
GLA_WIDTH = GLA_HEADS * GLA_DV
GLA_LOWRANK = 16
GLA_GATE_TAU = 16.0
GLA_CHUNK = 32
LRU_WIDTH = 256
LRU_BLOCKS = 4
LRU_BLOCK_W = LRU_WIDTH // LRU_BLOCKS
CONV_WIDTH = 4
LRU_C = 8.0
MLA_HEADS = 8
MLA_NOPE = 64
MLA_ROPE = 32
MLA_V = 64
MLA_Q_LORA = 256
MLA_KV_LORA = 128
ROPE_THETA = 10000.0
D_FF = 2816
LN_EPS = 1e-5
RMS_EPS = 1e-6
ATTN_SCALE = (MLA_NOPE + MLA_ROPE) ** -0.5
QUERY_SCALE = ATTN_SCALE * math.log2(math.e)
INPROJ_PARTS = 2
ATTN_BLOCK = 512

LANES = 128
SUBLANES = 8
VMEM_LIMIT_BYTES = 56 * 1024 * 1024

C_Q, C_K, C_V, C_GG, C_LX, C_LG = 0, 128, 256, 512, 768, 1024
C_DQ, C_DKV, C_TAIL = 1280, 1536, 1664
W_IN_COLS = 1792
C_LOGF = 1280
PROJ_COLS = 1408
QK_WIDTH = 256


def _cparams(sem):
    return pltpu.CompilerParams(dimension_semantics=sem, vmem_limit_bytes=VMEM_LIMIT_BYTES)


def _layer_norm_rows(y, g, b):
    mu = jnp.mean(y, axis=-1, keepdims=True)
    yc = y - mu
    var = jnp.mean(yc * yc, axis=-1, keepdims=True)
    return yc * lax.rsqrt(var + LN_EPS) * g + b


def _rms_rows(y, g):
    return y * lax.rsqrt(jnp.mean(y * y, axis=-1, keepdims=True) + RMS_EPS) * g


def _ada_kernel(c_ref, w_ref, b_ref, o_ref):
    c = c_ref[...]
    s = (c * jax.nn.sigmoid(c)).astype(BF16)
    o_ref[0] = jnp.dot(s, w_ref[0].astype(BF16), preferred_element_type=F32) + b_ref[0]


def _ada_all(c_all, w_ada, b_ada):
    depth, d, n = w_ada.shape
    bt = c_all.shape[0]
    tn = 1536
    return pl.pallas_call(
        _ada_kernel,
        grid=(depth, n // tn),
        in_specs=[pl.BlockSpec((bt, d), lambda l, j: (0, 0)),
                  pl.BlockSpec((1, d, tn), lambda l, j: (l, 0, j)),
                  pl.BlockSpec((1, 1, tn), lambda l, j: (l, 0, j))],
        out_specs=pl.BlockSpec((1, bt, tn), lambda l, j: (l, 0, j)),
        out_shape=jax.ShapeDtypeStruct((depth, bt, n), F32),
        compiler_params=_cparams(("parallel", "parallel")),
        name="ada_mod",
    )(c_all, w_ada, b_ada.reshape(depth, 1, n))


def _ln_kernel(x_ref, g_ref, b_ref, o_ref):
    o_ref[...] = _layer_norm_rows(x_ref[...], g_ref[...], b_ref[...])


def _ln_in(x2d, g, b):
    n, d = x2d.shape
    tm = min(1024, n)
    return pl.pallas_call(
        _ln_kernel,
        grid=(n // tm,),
        in_specs=[pl.BlockSpec((tm, d), lambda i: (i, 0)),
                  pl.BlockSpec((1, d), lambda i: (0, 0)),
                  pl.BlockSpec((1, d), lambda i: (0, 0))],
        out_specs=pl.BlockSpec((tm, d), lambda i: (i, 0)),
        out_shape=jax.ShapeDtypeStruct((n, d), F32),
        compiler_params=_cparams(("parallel",)),
        name="ln_in",
    )(x2d, g.reshape(1, d), b.reshape(1, d))


def _fold_kernel(uq_ref, uk_ref, uv_ref, wo_ref, qlat_ref, fold_ref):
    for hh in range(MLA_HEADS):
        a = uq_ref[0, hh]
        b = uk_ref[0, hh]
        ql = lax.dot_general(a, b, (((1,), (1,)), ((), ())), precision=HIGHEST,
                             preferred_element_type=F32)
        qlat_ref[0, :, hh * MLA_KV_LORA:(hh + 1) * MLA_KV_LORA] = (ql * QUERY_SCALE).astype(BF16)
        fd = jnp.dot(uv_ref[0, hh], wo_ref[0, hh * MLA_V:(hh + 1) * MLA_V, :], precision=HIGHEST,
                     preferred_element_type=F32)
        fold_ref[0, hh * MLA_KV_LORA:(hh + 1) * MLA_KV_LORA, :] = fd.astype(BF16)


def _fold_weights(mla_w_uq, mla_w_uk, mla_w_uv, w_out):
    depth = mla_w_uq.shape[0]
    uq_n = jnp.transpose(mla_w_uq[..., :MLA_NOPE], (0, 2, 1, 3))
    uk_t = jnp.transpose(mla_w_uk, (0, 2, 1, 3))
    uv_t = jnp.transpose(mla_w_uv, (0, 2, 1, 3))
    mla_rows = MLA_HEADS * MLA_V
    mla_blk = (GLA_WIDTH + LRU_WIDTH) // mla_rows
    return pl.pallas_call(
        _fold_kernel,
        grid=(depth,),
        in_specs=[pl.BlockSpec((1, MLA_HEADS, MLA_Q_LORA, MLA_NOPE), lambda l: (l, 0, 0, 0)),
                  pl.BlockSpec((1, MLA_HEADS, MLA_KV_LORA, MLA_NOPE), lambda l: (l, 0, 0, 0)),
                  pl.BlockSpec((1, MLA_HEADS, MLA_KV_LORA, MLA_V), lambda l: (l, 0, 0, 0)),
                  pl.BlockSpec((1, mla_rows, D_MODEL), lambda l: (l, mla_blk, 0))],
        out_specs=[pl.BlockSpec((1, MLA_Q_LORA, MLA_HEADS * MLA_KV_LORA), lambda l: (l, 0, 0)),
                   pl.BlockSpec((1, MLA_HEADS * MLA_KV_LORA, D_MODEL), lambda l: (l, 0, 0))],
        out_shape=[jax.ShapeDtypeStruct((depth, MLA_Q_LORA, MLA_HEADS * MLA_KV_LORA), BF16),
                   jax.ShapeDtypeStruct((depth, MLA_HEADS * MLA_KV_LORA, D_MODEL), BF16)],
        compiler_params=_cparams(("parallel",)),
        name="fold_weights",
    )(uq_n, uk_t, uv_t, w_out)


def _inproj_kernel(x_ref, ada_ref, w_ref, wg_ref, bg_ref, qn_ref, kvn_ref, wq2_ref, cs_ref,
                   proj_ref, qcat_ref, ckv_ref, kr_ref, kcat_ref):
    bb, tt, d = x_ref.shape
    m = bb * tt
    parts = INPROJ_PARTS if (bb == 1 and tt % (INPROJ_PARTS * 16) == 0) else 1
    mp = m // parts
    x = x_ref[...]
    shift = ada_ref[:, 0:1, :]
    scale = ada_ref[:, 1:2, :]
    h = (x * (1.0 + scale) + shift).reshape(m, d).astype(BF16)
    lane = lax.broadcasted_iota(jnp.int32, (mp, LANES), 1)
    rope_lanes = lane < MLA_ROPE

    def project(r):
        rows = pl.ds(r * mp, mp)
        p = jnp.dot(h[r * mp:(r + 1) * mp], w_ref[...], preferred_element_type=F32)
        proj_ref[rows, 0:C_DQ] = p[:, 0:C_DQ]
        tail = p[:, C_TAIL:C_TAIL + LANES]
        z = jnp.dot(tail.astype(BF16), wg_ref[...], preferred_element_type=F32) + bg_ref[...]
        proj_ref[rows, C_LOGF:C_LOGF + LANES] = jax.nn.log_sigmoid(z) / GLA_GATE_TAU
        return p

    def finish(r, p):
        rows = pl.ds(r * mp, mp)
        cs = cs_ref[rows, :]

        def rope(block):
            rot = block * cs
            rot = rot + pltpu.roll(rot, LANES - MLA_ROPE, 1)
            return jnp.where(rope_lanes, rot, 0.0)

        kr = rope(p[:, C_TAIL:C_TAIL + LANES])
        kr_ref[rows, :] = kr[:, 0:MLA_ROPE]
        ckv = _rms_rows(p[:, C_DKV:C_DKV + MLA_KV_LORA], kvn_ref[...])
        ckv_ref[rows, :] = ckv
        kcat_ref[rows, :] = jnp.concatenate([ckv.astype(BF16), kr.astype(BF16)], axis=1)

        dqn = _rms_rows(p[:, C_DQ:C_DQ + MLA_Q_LORA], qn_ref[...]).astype(BF16)
        q2 = jnp.dot(dqn, wq2_ref[...], preferred_element_type=F32)
        for hh in range(MLA_HEADS):
            lat = q2[:, hh * LANES:(hh + 1) * LANES]
            rp = rope(q2[:, (MLA_HEADS + hh) * LANES:(MLA_HEADS + hh + 1) * LANES])
            qc = jnp.concatenate([lat, rp], axis=1).astype(qcat_ref.dtype)
            if parts == 1:
                qcat_ref[:, hh] = qc.reshape(bb, tt, QK_WIDTH)
            else:
                qcat_ref[0, hh, rows, :] = qc

    projected = [project(r) for r in range(parts)]
    for r in range(parts):
        finish(r, projected[r])


def _inproj(x, ada, w_in_p, wg_p, bg, qn, kvn, wq2, cs, bb, tt, layer):
    b, t, d = x.shape
    n = b * t
    m = bb * tt
    grid = (b // bb, t // tt)
    nt = t // tt
    const = lambda i, j: (0, 0)
    row = lambda i, j: (i * nt + j, 0)
    return pl.pallas_call(
        _inproj_kernel,
        grid=grid,
        in_specs=[pl.BlockSpec((bb, tt, d), lambda i, j: (i, j, 0)),
                  pl.BlockSpec((bb, 6, d), lambda i, j: (i, 0, 0)),
                  pl.BlockSpec((None, d, W_IN_COLS), lambda i, j: (layer, 0, 0)),
                  pl.BlockSpec((LANES, LANES), const),
                  pl.BlockSpec((1, LANES), const),
                  pl.BlockSpec((1, MLA_Q_LORA), const),
                  pl.BlockSpec((1, MLA_KV_LORA), const),
                  pl.BlockSpec((None, MLA_Q_LORA, 2 * MLA_HEADS * LANES), lambda i, j: (layer, 0, 0)),
                  pl.BlockSpec((m, LANES), lambda i, j: (j, 0))],
        out_specs=[pl.BlockSpec((m, PROJ_COLS), row),
                   pl.BlockSpec((bb, MLA_HEADS, tt, QK_WIDTH), lambda i, j: (i, 0, j, 0)),
                   pl.BlockSpec((m, MLA_KV_LORA), row),
                   pl.BlockSpec((m, MLA_ROPE), row),
                   pl.BlockSpec((m, QK_WIDTH), row)],
        out_shape=[jax.ShapeDtypeStruct((n, PROJ_COLS), F32),
                   jax.ShapeDtypeStruct((b, MLA_HEADS, t, QK_WIDTH), BF16 if tt % 16 == 0 else F32),
                   jax.ShapeDtypeStruct((n, MLA_KV_LORA), F32),
                   jax.ShapeDtypeStruct((n, MLA_ROPE), F32),
                   jax.ShapeDtypeStruct((n, QK_WIDTH), BF16)],
        compiler_params=_cparams(("parallel", "parallel")),
        name="in_proj",
    )(x, ada, w_in_p, wg_p, bg, qn, kvn, wq2, cs)


def _gla_kernel(q_ref, k_ref, v_ref, gg_ref, g_ref, s0_ref, ng_ref, ltri_ref, ind_ref,
                msk_ref, seg_ref, o_ref, sT_ref, st_sc, b_sc, o_sc, *, nb, tt, c):
    tj = pl.program_id(1)
    n_chunks = tt // c
    groups = c // SUBLANES
    ltri = ltri_ref[...]
    ind = ind_ref[...]
    msk = msk_ref[...]
    row_iota = lax.broadcasted_iota(jnp.int32, (c, LANES), 0)

    on_diag = ind.astype(F32) > 0.5

    @pl.when(tj == 0)
    def _():
        for j in range(nb):
            s_cat = jnp.concatenate([s0_ref[j, hh] for hh in range(GLA_HEADS)], axis=1)
            s_bd = jnp.where(on_diag, jnp.concatenate([s_cat] * GLA_HEADS, axis=0), 0.0)
            st_sc[j] = s_bd.T

    def chunk(ci, carry):
        for j in range(nb):
            r0 = pl.multiple_of(ci * c, c)
            g = g_ref[j, pl.ds(r0, c), :]
            b = jnp.dot(ltri, g, precision=HIGHEST, preferred_element_type=F32)
            b_sc[j] = b
            q = q_ref[j, pl.ds(r0, c), :]
            k = k_ref[j, pl.ds(r0, c), :]
            v = v_ref[j, pl.ds(r0, c), :]
            blast = b_sc[j, pl.ds(c - 1, 1), :]
            qe = q * jnp.exp(b)
            ke = k * jnp.exp(blast - b)
            st = st_sc[j]
            o_inter = lax.dot_general(qe.astype(BF16), st.astype(BF16), (((1,), (1,)), ((), ())),
                                      preferred_element_type=F32)
            ut = lax.dot_general(v.astype(BF16), ke.astype(BF16), (((0,), (0,)), ((), ())),
                                 preferred_element_type=F32)
            st_sc[j] = st * jnp.exp(blast) + ut * msk

            o_blk = [None] * groups
            for g0 in range(groups):
                lo = g0 * SUBLANES
                pieces = []
                for s in range(lo, lo + SUBLANES):
                    bs = b_sc[j, pl.ds(s, 1), :]
                    ks = k_ref[j, pl.ds(r0 + s, 1), :]
                    e = jnp.exp(b[lo:, :] - bs)
                    head = jnp.where(row_iota[lo:lo + SUBLANES, :] >= s, e[0:SUBLANES, :], 0.0)
                    e = head if c - lo == SUBLANES else jnp.concatenate([head, e[SUBLANES:, :]], axis=0)
                    pieces.append(e * q[lo:, :] * ks)
                w = jnp.concatenate(pieces, axis=0).astype(BF16)
                a = jnp.dot(w, ind, preferred_element_type=F32)
                rows = c - lo
                for idx in range(SUBLANES):
                    vs = v_ref[j, pl.ds(r0 + lo + idx, 1), :]
                    for rb in range(g0, groups):
                        piece = a[idx * rows + (rb - g0) * SUBLANES:
                                  idx * rows + (rb - g0 + 1) * SUBLANES, :] * vs
                        o_blk[rb] = piece if o_blk[rb] is None else o_blk[rb] + piece
            o_intra = jnp.concatenate(o_blk, axis=0) if groups > 1 else o_blk[0]
            o_sc[j, pl.ds(r0, c), :] = o_inter + o_intra
        return carry

    lax.fori_loop(0, n_chunks, chunk, 0, unroll=2 if n_chunks % 2 == 0 else 1)

    @pl.when(tj == pl.num_programs(1) - 1)
    def _():
        for j in range(nb):
            s_bd = st_sc[j].T
            for hh in range(GLA_HEADS):
                sT_ref[j, hh] = s_bd[hh * GLA_DK:(hh + 1) * GLA_DK, hh * GLA_DV:(hh + 1) * GLA_DV]

    ng = ng_ref[...]
    seg = seg_ref[...]

    def epi(j, carry):
        o = o_sc[j]
        ms = jnp.dot(o * o, seg, precision=HIGHEST, preferred_element_type=F32)
        gg = gg_ref[j]
        o_ref[j] = o * lax.rsqrt(ms + RMS_EPS) * ng * (gg * jax.nn.sigmoid(gg))
        return carry

    lax.fori_loop(0, nb, epi, 0)


def _gla(proj, s0T, ng, b, t, nb, tt):
    n = b * t
    c = min(GLA_CHUNK, t)
    proj3 = proj.reshape(b, t, PROJ_COLS)
    ltri = jnp.asarray(np.tril(np.ones((c, c), np.float32)))
    hk = np.arange(GLA_HEADS * GLA_DK) // GLA_DK
    hv = np.arange(GLA_WIDTH) // GLA_DV
    ind = jnp.asarray((hk[:, None] == hv[None, :]).astype(np.float32)).astype(BF16)
    msk = jnp.asarray((hv[:, None] == hk[None, :]).astype(np.float32))
    seg = jnp.asarray((hv[:, None] == hv[None, :]).astype(np.float32) / GLA_DV)
    const = lambda i, j: (0, 0)
    kern = functools.partial(_gla_kernel, nb=nb, tt=tt, c=c)

    def cols(width, col0):
        return pl.BlockSpec((nb, tt, width), lambda i, j: (i, j, col0 // width))

    o_gla, sT = pl.pallas_call(
        kern,
        grid=(b // nb, t // tt),
        in_specs=[cols(LANES, C_Q), cols(LANES, C_K), cols(GLA_WIDTH, C_V), cols(GLA_WIDTH, C_GG),
                  cols(LANES, C_LOGF),
                  pl.BlockSpec((nb, GLA_HEADS, GLA_DK, GLA_DV), lambda i, j: (i, 0, 0, 0)),
                  pl.BlockSpec((1, GLA_WIDTH), const),
                  pl.BlockSpec((c, c), const),
                  pl.BlockSpec((LANES, GLA_WIDTH), const),
                  pl.BlockSpec((GLA_WIDTH, LANES), const),
                  pl.BlockSpec((GLA_WIDTH, GLA_WIDTH), const)],
        out_specs=[pl.BlockSpec((nb, tt, GLA_WIDTH), lambda i, j: (i, j, 0)),
                   pl.BlockSpec((nb, GLA_HEADS, GLA_DK, GLA_DV), lambda i, j: (i, 0, 0, 0))],
        out_shape=[jax.ShapeDtypeStruct((b, t, GLA_WIDTH), F32),
                   jax.ShapeDtypeStruct((b, GLA_HEADS, GLA_DK, GLA_DV), F32)],
        scratch_shapes=[pltpu.VMEM((nb, GLA_WIDTH, LANES), F32),
                        pltpu.VMEM((nb, c, LANES), F32),
                        pltpu.VMEM((nb, tt, GLA_WIDTH), F32)],
        compiler_params=_cparams(("parallel", "arbitrary")),
        name="gla",
    )(proj3, proj3, proj3, proj3, proj3, s0T, ng, ltri, ind, msk, seg)
    return o_gla.reshape(n, GLA_WIDTH), sT


def _lru_kernel(lx_ref, lg_ref, cbuf_ref, h0_ref, cw_ref, cb_ref, wax_ref, bax_ref, sp_ref,
                o_ref, hn_ref, cn_ref, xp_sc, a_sc, u_sc, *, nb, t):
    pad = SUBLANES
    hist = CONV_WIDTH - 1
    row8 = lax.broadcasted_iota(jnp.int32, (SUBLANES, LRU_WIDTH), 0)
    rows_per_iter = min(32, t)
    sub = rows_per_iter // SUBLANES

    def scan_block(a, u):
        for dd in (1, 2, 4):
            a_s = jnp.where(row8 >= dd, pltpu.roll(a, dd, 0), 1.0)
            u_s = jnp.where(row8 >= dd, pltpu.roll(u, dd, 0), 0.0)
            u = a * u_s + u
            a = a * a_s
        return a, u

    for j in range(nb):
        base = j * t
        x = lx_ref[pl.ds(base, t), :]
        xp_sc[pl.ds(0, pad), :] = jnp.zeros((pad, LRU_WIDTH), F32)
        xp_sc[pl.ds(pad - hist, hist), :] = cbuf_ref[j]
        xp_sc[pl.ds(pad, pad), :] = x[0:pad, :]
        xc = cb_ref[...]
        for kk in range(CONV_WIDTH):
            xc = xc + xp_sc[pl.ds(pad - hist + kk, pad), :] * cw_ref[pl.ds(kk, 1), :]
        if t > pad:
            body = cb_ref[...] + x * cw_ref[pl.ds(hist, 1), :]
            for kk in range(hist):
                body = body + pltpu.roll(x, hist - kk, 0) * cw_ref[pl.ds(kk, 1), :]
            xc = jnp.concatenate([xc, body[pad:, :]], axis=0)
        cn_ref[j] = lx_ref[pl.ds(base + t - hist, hist), :]

        ax = jnp.dot(xc.astype(BF16), wax_ref[...], preferred_element_type=F32) + bax_ref[...]
        r = jax.nn.sigmoid(ax[:, 0:LRU_WIDTH])
        ig = jax.nn.sigmoid(ax[:, LRU_WIDTH:2 * LRU_WIDTH])
        log_a = -LRU_C * r * sp_ref[...]
        a = jnp.exp(log_a)
        u = jnp.sqrt((a * a + 1.0) * jnp.tanh(-log_a)) * (ig * xc)
        a_sc[...] = a
        u_sc[...] = u
        u_sc[pl.ds(0, 1), :] = u[0:1, :] + a[0:1, :] * h0_ref[j]

        def step(i, hprev):
            r0 = pl.multiple_of(i * rows_per_iter, rows_per_iter)
            scans = []
            for sb in range(sub):
                ab = a_sc[pl.ds(r0 + sb * SUBLANES, SUBLANES), :]
                ub = u_sc[pl.ds(r0 + sb * SUBLANES, SUBLANES), :]
                scans.append(scan_block(ab, ub))
            for sb in range(sub):
                ac, uc = scans[sb]
                hb = ac * hprev + uc
                u_sc[pl.ds(r0 + sb * SUBLANES, SUBLANES), :] = hb
                hprev = hb[SUBLANES - 1:SUBLANES, :]
            return hprev

        hlast = lax.fori_loop(0, t // rows_per_iter, step, jnp.zeros((1, LRU_WIDTH), F32))
        hn_ref[j] = hlast
        lg = lg_ref[pl.ds(base, t), :]
        o_ref[pl.ds(base, t), :] = jax.nn.gelu(lg, approximate=True) * u_sc[...]


def _lru(proj, cbuf, h0, cw, cb, wax, bax, sp, b, t, nb):
    n = b * t
    rows = nb * t
    const = lambda i: (0, 0)
    kern = functools.partial(_lru_kernel, nb=nb, t=t)
    return pl.pallas_call(
        kern,
        grid=(b // nb,),
        in_specs=[pl.BlockSpec((rows, LRU_WIDTH), lambda i: (i, C_LX // LRU_WIDTH)),
                  pl.BlockSpec((rows, LRU_WIDTH), lambda i: (i, C_LG // LRU_WIDTH)),
                  pl.BlockSpec((nb, CONV_WIDTH - 1, LRU_WIDTH), lambda i: (i, 0, 0)),
                  pl.BlockSpec((nb, 1, LRU_WIDTH), lambda i: (i, 0, 0)),
                  pl.BlockSpec((CONV_WIDTH, LRU_WIDTH), const),
                  pl.BlockSpec((1, LRU_WIDTH), const),
                  pl.BlockSpec((LRU_WIDTH, 2 * LRU_WIDTH), const),
                  pl.BlockSpec((1, 2 * LRU_WIDTH), const),
                  pl.BlockSpec((1, LRU_WIDTH), const)],
        out_specs=[pl.BlockSpec((rows, LRU_WIDTH), lambda i: (i, 0)),
                   pl.BlockSpec((nb, 1, LRU_WIDTH), lambda i: (i, 0, 0)),
                   pl.BlockSpec((nb, CONV_WIDTH - 1, LRU_WIDTH), lambda i: (i, 0, 0))],
        out_shape=[jax.ShapeDtypeStruct((n, LRU_WIDTH), F32),
                   jax.ShapeDtypeStruct((b, 1, LRU_WIDTH), F32),
                   jax.ShapeDtypeStruct((b, CONV_WIDTH - 1, LRU_WIDTH), F32)],
        scratch_shapes=[pltpu.VMEM((2 * SUBLANES, LRU_WIDTH), F32),
                        pltpu.VMEM((t, LRU_WIDTH), F32),
                        pltpu.VMEM((t, LRU_WIDTH), F32)],
        compiler_params=_cparams(("parallel",)),
        name="rg_lru",
    )(proj, proj, cbuf, h0, cw, cb, wax, bax, sp)


def _softmax_step(s, m_ref, l_ref, acc_ref, v):
    shape3 = m_ref.shape
    rows = shape3[0] * shape3[1]
    reps = s.shape[1] // LANES
    m_prev = m_ref[...].reshape(rows, LANES)
    m_next = jnp.maximum(m_prev, jnp.max(s, axis=1, keepdims=True))
    alpha = jnp.exp2(m_prev - m_next)
    m_wide = m_next if reps == 1 else jnp.concatenate([m_next] * reps, axis=1)
    p = jnp.exp2(s - m_wide)
    l_next = alpha * l_ref[...].reshape(rows, LANES) + jnp.sum(p, axis=1, keepdims=True)
    acc_next = alpha * acc_ref[...].reshape(rows, LANES) + jnp.dot(p.astype(BF16), v,
                                                                    preferred_element_type=F32)
    l_ref[...] = l_next.reshape(shape3)
    acc_ref[...] = acc_next.reshape(shape3)
    m_ref[...] = m_next.reshape(shape3)


def _causal_mask(s, n):
    tpos = lax.broadcasted_iota(jnp.int32, s.shape, 0) & (n - 1)
    kpos = lax.broadcasted_iota(jnp.int32, s.shape, 1)
    return jnp.where(kpos <= tpos, s, -jnp.inf)


def _attn_kernel(q_ref, k_ref, o_ref, m_sc, l_sc, acc_sc, *, tq):
    i = pl.program_id(1)
    half = tq // 2
    m_sc[...] = jnp.full(m_sc.shape, -jnp.inf, F32)
    l_sc[...] = jnp.zeros(l_sc.shape, F32)
    acc_sc[...] = jnp.zeros(acc_sc.shape, F32)
    nt = (((1,), (1,)), ((), ()))

    def body(j, carry):
        kb = k_ref[0, pl.ds(pl.multiple_of(j * tq, tq), tq), :]
        q = q_ref[0].reshape(MLA_HEADS * tq, QK_WIDTH)
        s = lax.dot_general(q, kb, nt, preferred_element_type=F32)
        _softmax_step(s, m_sc, l_sc, acc_sc, kb[:, 0:MLA_KV_LORA])
        return carry

    lax.fori_loop(0, i, body, 0)

    k0 = pl.multiple_of(i * tq, tq)
    kb_a = k_ref[0, pl.ds(k0, half), :]
    q = q_ref[0].reshape(MLA_HEADS * tq, QK_WIDTH)
    s_a = lax.dot_general(q, kb_a, nt, preferred_element_type=F32)
    tpos = lax.broadcasted_iota(jnp.int32, s_a.shape, 0) & (tq - 1)
    kpos = lax.broadcasted_iota(jnp.int32, s_a.shape, 1)
    s_a = jnp.where(kpos <= tpos, s_a, -jnp.inf)
    _softmax_step(s_a, m_sc, l_sc, acc_sc, kb_a[:, 0:MLA_KV_LORA])

    kb_b = k_ref[0, pl.ds(k0 + half, half), :]
    late = pl.ds(half, half)
    q_b = q_ref[0, :, late, :].reshape(MLA_HEADS * half, QK_WIDTH)
    s_b = _causal_mask(lax.dot_general(q_b, kb_b, nt, preferred_element_type=F32), half)
    _softmax_step(s_b, m_sc.at[:, late, :], l_sc.at[:, late, :], acc_sc.at[:, late, :],
                  kb_b[:, 0:MLA_KV_LORA])

    for hh in range(MLA_HEADS):
        o_ref[0, :, hh * MLA_KV_LORA:(hh + 1) * MLA_KV_LORA] = (acc_sc[hh] / l_sc[hh]).astype(BF16)


def _attn_prompt(qcat, kcat, b, t):
    tq = min(ATTN_BLOCK, t)
    kern = functools.partial(_attn_kernel, tq=tq)
    stat = pltpu.VMEM((MLA_HEADS, tq, LANES), F32)
    return pl.pallas_call(
        kern,
        grid=(b, t // tq),
        in_specs=[pl.BlockSpec((1, MLA_HEADS, tq, QK_WIDTH), lambda bi, i: (bi, 0, i, 0)),
                  pl.BlockSpec((1, t, QK_WIDTH), lambda bi, i: (bi, 0, 0))],
        out_specs=pl.BlockSpec((1, tq, MLA_HEADS * MLA_KV_LORA), lambda bi, i: (bi, i, 0)),
        out_shape=jax.ShapeDtypeStruct((b, t, MLA_HEADS * MLA_KV_LORA), BF16),
        scratch_shapes=[stat, stat, stat],
        compiler_params=_cparams(("parallel", "parallel")),
        name="attn_prompt",
    )(qcat, kcat.reshape(b, t, QK_WIDTH))


PAGE_SLOTS = 3
PAGES_PER_STEP = 128
PAGED_SUB_KEYS = 1024
PAGED_LAG = 2


def _attn_paged_kernel(pt_ref, q_ref, cnew_ref, rnew_ref, ckv_hbm, krt_hbm, o_ref,
                       kc_buf, kr_buf, sem, m_sc, l_sc, acc_sc, *, pages, ts, layer, steps, n_steps):
    j = pl.program_id(1)
    step = pl.program_id(0) * steps + j
    r = MLA_HEADS * ts

    def page_copy(src_page, slot, i):
        return (pltpu.make_async_copy(ckv_hbm.at[layer, src_page],
                                      kc_buf.at[slot, pl.ds(i * PAGE_SIZE, PAGE_SIZE), :],
                                      sem.at[slot, 0]),
                pltpu.make_async_copy(krt_hbm.at[layer, src_page],
                                      kr_buf.at[slot, :, pl.ds(i * PAGE_SIZE, PAGE_SIZE)],
                                      sem.at[slot, 1]))

    def start_step(step_id, slot):
        seq = step_id // steps
        first = (step_id - seq * steps) * pages
        for i in range(pages):
            for cp in page_copy(pt_ref[seq, first + i], slot, i):
                cp.start(priority=i % 2)

    @pl.when(step == 0)
    def _():
        for ahead in range(min(PAGE_SLOTS - 1, n_steps)):
            start_step(ahead, ahead)

    @pl.when(j == 0)
    def _():
        m_sc[...] = jnp.full(m_sc.shape, -jnp.inf, F32)
        l_sc[...] = jnp.zeros(l_sc.shape, F32)
        acc_sc[...] = jnp.zeros(acc_sc.shape, F32)

    slot = lax.rem(step, PAGE_SLOTS)
    for i in range(pages):
        for cp in page_copy(0, slot, i):
            cp.wait()

    q = q_ref[0].reshape(r, QK_WIDTH).astype(BF16)
    q_lat = q[:, 0:MLA_KV_LORA]
    q_rope = q[:, MLA_KV_LORA:MLA_KV_LORA + MLA_ROPE]
    sub = min(PAGED_SUB_KEYS, pages * PAGE_SIZE)
    n_sub = (pages * PAGE_SIZE) // sub

    def scores(ci):
        kc = kc_buf[slot, pl.ds(ci * sub, sub), :].astype(BF16)
        krt = kr_buf[slot, :, pl.ds(ci * sub, sub)].astype(BF16)
        s = (lax.dot_general(q_lat, kc, (((1,), (1,)), ((), ())), preferred_element_type=F32)
             + jnp.dot(q_rope, krt, preferred_element_type=F32))
        m_c = jnp.broadcast_to(jnp.max(s, axis=1, keepdims=True), (r, LANES))
        p = jnp.exp2(s - jnp.concatenate([m_c] * (sub // LANES), axis=1))
        l_c = jnp.broadcast_to(jnp.sum(p, axis=1, keepdims=True), (r, LANES))
        return m_c, l_c, p.astype(BF16), kc

    staged, parts = [], []
    for ci in range(n_sub + PAGED_LAG):
        if ci < n_sub:
            staged.append(scores(ci))
        if ci >= PAGED_LAG:
            m_c, l_c, p_c, kc_c = staged[ci - PAGED_LAG]
            parts.append((m_c, l_c, jnp.dot(p_c, kc_c, preferred_element_type=F32)))
    stat3 = m_sc.shape
    m_prev = m_sc[...].reshape(r, LANES)
    m_next = m_prev
    for m_c, _, _ in parts:
        m_next = jnp.maximum(m_next, m_c)
    alpha = jnp.exp2(m_prev - m_next)
    l_next = alpha * l_sc[...].reshape(r, LANES)
    acc_next = alpha * acc_sc[...].reshape(r, LANES)
    for m_c, l_c, acc_c in parts:
        w_c = jnp.exp2(m_c - m_next)
        l_next = l_next + w_c * l_c
        acc_next = acc_next + w_c * acc_c
    m_sc[...] = m_next.reshape(stat3)
    l_sc[...] = l_next.reshape(stat3)
    acc_sc[...] = acc_next.reshape(stat3)

    nxt = step + (PAGE_SLOTS - 1)

    @pl.when(nxt < n_steps)
    def _():
        start_step(nxt, lax.rem(nxt, PAGE_SLOTS))

    @pl.when(j == steps - 1)
    def _():
        pad = jnp.zeros((LANES - ts, MLA_KV_LORA), F32)
        kc_new = jnp.concatenate([cnew_ref[...], pad], axis=0).astype(BF16)
        kr_new = jnp.concatenate([rnew_ref[...], pad[:, 0:MLA_ROPE]], axis=0).astype(BF16)
        s_new = (lax.dot_general(q_lat, kc_new, (((1,), (1,)), ((), ())), preferred_element_type=F32)
                 + lax.dot_general(q_rope, kr_new, (((1,), (1,)), ((), ())),
                                   preferred_element_type=F32))
        _softmax_step(_causal_mask(s_new, ts), m_sc, l_sc, acc_sc, kc_new)
        for hh in range(MLA_HEADS):
            o_ref[0, :, hh * MLA_KV_LORA:(hh + 1) * MLA_KV_LORA] = acc_sc[hh] / l_sc[hh]


def _attn_paged(qcat, ckv_new, kr_new, cache_ckv, cache_krope_t, page_table, layer, pages):
    b, _, ts, _ = qcat.shape
    n_pages = page_table.shape[1]
    steps = n_pages // pages
    kern = functools.partial(_attn_paged_kernel, pages=pages, ts=ts, layer=layer, steps=steps,
                             n_steps=b * steps)
    stat = pltpu.VMEM((MLA_HEADS, ts, LANES), F32)
    grid_spec = pltpu.PrefetchScalarGridSpec(
        num_scalar_prefetch=1,
        grid=(b, steps),
        in_specs=[pl.BlockSpec((1, MLA_HEADS, ts, QK_WIDTH), lambda bi, j, pt: (bi, 0, 0, 0)),
                  pl.BlockSpec((ts, MLA_KV_LORA), lambda bi, j, pt: (bi, 0)),
                  pl.BlockSpec((ts, MLA_ROPE), lambda bi, j, pt: (bi, 0)),
                  pl.BlockSpec(memory_space=pl.ANY),
                  pl.BlockSpec(memory_space=pl.ANY)],
        out_specs=pl.BlockSpec((1, ts, MLA_HEADS * MLA_KV_LORA), lambda bi, j, pt: (bi, 0, 0)),
        scratch_shapes=[pltpu.VMEM((PAGE_SLOTS, pages * PAGE_SIZE, MLA_KV_LORA), F32),
                        pltpu.VMEM((PAGE_SLOTS, MLA_ROPE, pages * PAGE_SIZE), F32),
                        pltpu.SemaphoreType.DMA((PAGE_SLOTS, 2)),
                        stat, stat, stat])
    return pl.pallas_call(
        kern,
        grid_spec=grid_spec,
        out_shape=jax.ShapeDtypeStruct((b, ts, MLA_HEADS * MLA_KV_LORA), F32),
        compiler_params=_cparams(("arbitrary", "arbitrary")),
        name="attn_paged",
    )(page_table, qcat, ckv_new, kr_new, cache_ckv, cache_krope_t)


FFN_CHUNK = 256


def _mix_ffn_kernel(x_ref, ada_ref, og_ref, ol_ref, oa_ref, wgl_ref, wf_ref, g1_ref, b1_ref,
                    wgu_ref, wd_ref, g2_ref, b2_ref, o_ref, *, alpha):
    bb, tt, d = x_ref.shape
    m = bb * tt
    gate1 = ada_ref[:, 2:3, :]
    shift2 = ada_ref[:, 3:4, :]
    scale2 = ada_ref[:, 4:5, :]
    gate2 = ada_ref[:, 5:6, :]
    ogl = jnp.concatenate([og_ref[...], ol_ref[...]], axis=1).astype(BF16)
    mix = (jnp.dot(ogl, wgl_ref[...], preferred_element_type=F32)
           + jnp.dot(oa_ref[...].astype(BF16), wf_ref[...], preferred_element_type=F32))
    x1 = _layer_norm_rows(alpha * x_ref[...] + gate1 * mix.reshape(bb, tt, d), g1_ref[...], b1_ref[...])

    h2 = (x1 * (1.0 + scale2) + shift2).reshape(m, d).astype(BF16)
    acc = jnp.zeros((m, d), F32)
    for ci in range(D_FF // FFN_CHUNK):
        lo = ci * FFN_CHUNK
        gf = jnp.dot(h2, wgu_ref[:, lo:lo + FFN_CHUNK], preferred_element_type=F32)
        uf = jnp.dot(h2, wgu_ref[:, D_FF + lo:D_FF + lo + FFN_CHUNK], preferred_element_type=F32)
        act = (gf * jax.nn.sigmoid(gf) * uf).astype(BF16)
        acc = acc + jnp.dot(act, wd_ref[lo:lo + FFN_CHUNK, :], preferred_element_type=F32)
    y = alpha * x1 + gate2 * acc.reshape(bb, tt, d)
    o_ref[...] = _layer_norm_rows(y, g2_ref[...], b2_ref[...])


def _mix_ffn(x, ada, o_gla, o_lru, o_lat, wgl, wfold, ln1, wgu, wd, ln2, bb, tt, alpha, layer):
    b, t, d = x.shape
    nt = t // tt
    m = bb * tt
    row = lambda i, j: (i * nt + j, 0)

    def resident(shape):
        return pl.BlockSpec(shape, lambda i, j: (0, 0), pipeline_mode=pl.Buffered(1))

    def layer_slab(rows, cols):
        return pl.BlockSpec((None, rows, cols), lambda i, j: (layer, 0, 0), pipeline_mode=pl.Buffered(1))

    kern = functools.partial(_mix_ffn_kernel, alpha=alpha)
    return pl.pallas_call(
        kern,
        grid=(b // bb, nt),
        in_specs=[pl.BlockSpec((bb, tt, d), lambda i, j: (i, j, 0)),
                  pl.BlockSpec((bb, 6, d), lambda i, j: (i, 0, 0)),
                  pl.BlockSpec((m, GLA_WIDTH), row),
                  pl.BlockSpec((m, LRU_WIDTH), row),
                  pl.BlockSpec((m, MLA_HEADS * MLA_KV_LORA), row),
                  layer_slab(GLA_WIDTH + LRU_WIDTH, d),
                  layer_slab(MLA_HEADS * MLA_KV_LORA, d),
                  resident((1, d)), resident((1, d)),
                  layer_slab(d, 2 * D_FF),
                  layer_slab(D_FF, d),
                  resident((1, d)), resident((1, d))],
        out_specs=pl.BlockSpec((bb, tt, d), lambda i, j: (i, j, 0)),
        out_shape=jax.ShapeDtypeStruct((b, t, d), F32),
        compiler_params=_cparams(("parallel", "parallel")),
        name="mix_ffn",
    )(x, ada, o_gla, o_lru, o_lat, wgl, wfold, ln1[0], ln1[1], wgu, wd, ln2[0], ln2[1])


def _rotate_half_cols(w):
    half = MLA_ROPE // 2
    return jnp.concatenate([-w[..., half:], w[..., :half]], axis=-1)


def _prep_stacked_weights(w_in, mla_w_uq, w_qlat):
    depth, d, _ = w_in.shape
    o = np.cumsum([0, 128, 128, 256, 256, 16, 256, 256, 256, 128, 32])
    gq, gk, gv, gg, glr, lx, lgt, dq, dkv, kr = [w_in[:, :, o[i]:o[i + 1]] for i in range(10)]
    tail = jnp.concatenate([kr, _rotate_half_cols(kr), glr,
                            jnp.zeros((depth, d, LANES - 2 * MLA_ROPE - GLA_LOWRANK), F32)], axis=2)
    w_in_p = jnp.concatenate([gq * (GLA_DK ** -0.5), gk, gv, gg, lx, lgt, dq, dkv, tail],
                             axis=2).astype(BF16)
    rope_w = mla_w_uq[:, :, :, MLA_NOPE:] * QUERY_SCALE
    rope_blk = jnp.concatenate(
        [rope_w, _rotate_half_cols(rope_w),
         jnp.zeros((depth, MLA_Q_LORA, MLA_HEADS, LANES - 2 * MLA_ROPE), F32)], axis=-1)
    wq2 = jnp.concatenate(
        [w_qlat, rope_blk.reshape(depth, MLA_Q_LORA, MLA_HEADS * LANES).astype(BF16)], axis=2)
    return w_in_p, wq2


def _prep_layer_weights(l, gla_w_gate, gla_b_gate, gla_norm_g, lru_conv_w, lru_conv_b, lru_w_a,
                        lru_b_a, lru_w_x, lru_b_x, lru_lambda, mla_q_norm_g, mla_kv_norm_g):
    wg_p = jnp.pad(gla_w_gate[l], ((2 * MLA_ROPE, LANES - 2 * MLA_ROPE - GLA_LOWRANK),
                                   (0, 0))).astype(BF16)
    bg = gla_b_gate[l].reshape(1, LANES)
    ng = jnp.tile(gla_norm_g[l], GLA_HEADS).reshape(1, GLA_WIDTH)

    def block_diag(wb):
        on_diag = jnp.eye(LRU_BLOCKS, dtype=bool)[:, None, :, None]
        return jnp.where(on_diag, wb[:, :, None, :], 0.0).reshape(LRU_WIDTH, LRU_WIDTH)

    wax = jnp.concatenate([block_diag(lru_w_a[l]), block_diag(lru_w_x[l])], axis=1).astype(BF16)
    bax = jnp.concatenate([lru_b_a[l], lru_b_x[l]]).reshape(1, 2 * LRU_WIDTH)
    sp = jax.nn.softplus(-lru_lambda[l].astype(F32)).reshape(1, LRU_WIDTH)
    return dict(
        wg_p=wg_p, bg=bg, ng=ng,
        qn=mla_q_norm_g[l].reshape(1, MLA_Q_LORA), kvn=mla_kv_norm_g[l].reshape(1, MLA_KV_LORA),
        cw=lru_conv_w[l], cb=lru_conv_b[l].reshape(1, LRU_WIDTH), wax=wax, bax=bax, sp=sp)


def _rope_table(pos):
    half = MLA_ROPE // 2
    inv_freq = ROPE_THETA ** (-jnp.arange(half, dtype=F32) / half)
    ang = pos.astype(F32)[:, None] * inv_freq[None, :]
    cos, sin = jnp.cos(ang), jnp.sin(ang)
    return jnp.concatenate([cos, cos, sin, sin,
                            jnp.zeros((pos.shape[0], LANES - 2 * MLA_ROPE), F32)], axis=1)


def _group_layer(l, x, ada, lw, big, cs, s0, h0, cbuf, tiles, alpha, ln1, ln2, attend):
    b, t, _ = x.shape
    bb, tt, nb, gla_nb, gla_tt = tiles
    proj, qcat, ckv_new, kr_new, kcat = _inproj(x, ada, big["w_in_p"], lw["wg_p"], lw["bg"], lw["qn"],
                                                lw["kvn"], big["wq2"], cs, bb, tt, l)
    o_gla, s_new = _gla(proj, s0, lw["ng"], b, t, gla_nb, gla_tt)
    o_lru, h_new, conv_new = _lru(proj, cbuf, h0, lw["cw"], lw["cb"], lw["wax"], lw["bax"], lw["sp"],
                                  b, t, nb)
    o_lat = attend(qcat, kcat, ckv_new, kr_new)
    x2 = _mix_ffn(x, ada, o_gla, o_lru, o_lat.reshape(b * t, -1), big["w_out"], big["w_fold"], ln1,
                  big["wgu"], big["wd"], ln2, bb, tt, alpha, l)
    states = (s_new, h_new.reshape(b, LRU_WIDTH), conv_new,
              ckv_new.reshape(b, t, MLA_KV_LORA), kr_new.reshape(b, t, MLA_ROPE))
    return x2, states


def kernel(x_prompt, x_sample, c_prompt, c_sample, state_gla, state_lru, state_conv, cache_ckv, cache_krope, page_table, ln_in_g, ln_in_b, w_ada, b_ada, w_in, gla_w_gate, gla_b_gate, gla_norm_g, lru_conv_w, lru_conv_b, lru_w_a, lru_b_a, lru_w_x, lru_b_x, lru_lambda, mla_q_norm_g, mla_w_uq, mla_kv_norm_g, mla_w_uk, mla_w_uv, w_out, ln1_g, ln1_b, ffn_w_gu, ffn_w_down, ln2_g, ln2_b):
    bp, tp, d = x_prompt.shape
    bs, ts, _ = x_sample.shape
    depth = w_in.shape[0]
    n_pages = page_table.shape[1]
    past_len = n_pages * PAGE_SIZE
    alpha = (2.0 * depth) ** 0.25

    tiles_p = (1, min(512, tp), 2 if bp % 2 == 0 else 1, 8 if bp % 8 == 0 else 1, min(256, tp))
    tiles_s = (bs, ts, 4 if bs % 4 == 0 else 1, 8 if bs % 8 == 0 else 1, ts)
    pages_per_step = PAGES_PER_STEP if n_pages % PAGES_PER_STEP == 0 else n_pages

    ada = _ada_all(jnp.concatenate([c_prompt, c_sample], axis=0), w_ada, b_ada)
    ada = ada.reshape(depth, bp + bs, 6, d)
    xp = _ln_in(x_prompt.reshape(bp * tp, d), ln_in_g, ln_in_b).reshape(bp, tp, d)
    xs = _ln_in(x_sample.reshape(bs * ts, d), ln_in_g, ln_in_b).reshape(bs, ts, d)
    w_qlat, w_fold = _fold_weights(mla_w_uq, mla_w_uk, mla_w_uv, w_out)

    cache_krope_t = jnp.swapaxes(cache_krope, 2, 3)
    cs_p = _rope_table(jnp.arange(tp, dtype=jnp.int32))
    cs_s = jnp.tile(_rope_table(past_len + jnp.arange(ts, dtype=jnp.int32)), (bs, 1))
    zero_s = jnp.zeros((bp, GLA_HEADS, GLA_DK, GLA_DV), F32)
    zero_h = jnp.zeros((bp, 1, LRU_WIDTH), F32)
    zero_conv = jnp.zeros((bp, CONV_WIDTH - 1, LRU_WIDTH), F32)
    w_in_p, wq2 = _prep_stacked_weights(w_in, mla_w_uq, w_qlat)
    big = dict(w_in_p=w_in_p, wq2=wq2, w_out=w_out.astype(BF16), w_fold=w_fold,
               wgu=ffn_w_gu.astype(BF16), wd=ffn_w_down.astype(BF16))

    st_p, st_s = [], []
    for l in range(depth):
        lw = _prep_layer_weights(l, gla_w_gate, gla_b_gate, gla_norm_g, lru_conv_w, lru_conv_b,
                                 lru_w_a, lru_b_a, lru_w_x, lru_b_x, lru_lambda, mla_q_norm_g,
                                 mla_kv_norm_g)
        ln1 = (ln1_g[l].reshape(1, d), ln1_b[l].reshape(1, d))
        ln2 = (ln2_g[l].reshape(1, d), ln2_b[l].reshape(1, d))

        def attend_p(qcat, kcat, ckv_new, kr_new):
            return _attn_prompt(qcat, kcat, bp, tp)

        def attend_s(qcat, kcat, ckv_new, kr_new, l=l):
            return _attn_paged(qcat, ckv_new, kr_new, cache_ckv, cache_krope_t, page_table, l,
                               pages_per_step)

        xp, sp = _group_layer(l, xp, ada[l, :bp], lw, big, cs_p, zero_s, zero_h, zero_conv,
                              tiles_p, alpha, ln1, ln2, attend_p)
        xs, ss = _group_layer(l, xs, ada[l, bp:], lw, big, cs_s, state_gla[l],
                              state_lru[l].reshape(bs, 1, LRU_WIDTH), state_conv[l],
                              tiles_s, alpha, ln1, ln2, attend_s)
        st_p.append(sp)
        st_s.append(ss)

    def stk(outs, j):
        return jnp.stack([o[j] for o in outs])

    return (xp, xs, stk(st_p, 0), stk(st_s, 0), stk(st_p, 1), stk(st_s, 1), stk(st_p, 2), stk(st_s, 2),
            stk(st_p, 3), stk(st_s, 3), stk(st_p, 4), stk(st_s, 4))
```

```python
import functools
import math

import numpy as np
import jax
import jax.numpy as jnp
from jax import lax
from jax.experimental import pallas as pl
from jax.experimental.pallas import tpu as pltpu

F32 = jnp.float32
BF16 = jnp.bfloat16
HIGHEST = lax.Precision.HIGHEST

D_MODEL = 1024
PAGE_SIZE = 128
GLA_HEADS = 4
GLA_DK = 32
GLA_DV = 64
GLA_WIDTH = GLA_HEADS * GLA_DV
GLA_LOWRANK = 16
GLA_GATE_TAU = 16.0
GLA_CHUNK = 32
LRU_WIDTH = 256
LRU_BLOCKS = 4
LRU_BLOCK_W = LRU_WIDTH // LRU_BLOCKS
CONV_WIDTH = 4
LRU_C = 8.0
MLA_HEADS = 8
MLA_NOPE = 64
MLA_ROPE = 32
MLA_V = 64
MLA_Q_LORA = 256
MLA_KV_LORA = 128
ROPE_THETA = 10000.0
D_FF = 2816
LN_EPS = 1e-5
RMS_EPS = 1e-6
ATTN_SCALE = (MLA_NOPE + MLA_ROPE) ** -0.5
QUERY_SCALE = ATTN_SCALE * math.log2(math.e)
INPROJ_PARTS = 2
ATTN_BLOCK = 512

LANES = 128
SUBLANES = 8
VMEM_LIMIT_BYTES = 56 * 1024 * 1024

C_Q, C_K, C_V, C_GG, C_LX, C_LG = 0, 128, 256, 512, 768, 1024
C_DQ, C_DKV, C_TAIL = 1280, 1536, 1664
W_IN_COLS = 1792
C_LOGF = 1280
PROJ_COLS = 1408
QK_WIDTH = 256


def _cparams(sem):
    return pltpu.CompilerParams(dimension_semantics=sem, vmem_limit_bytes=VMEM_LIMIT_BYTES)


def _layer_norm_rows(y, g, b):
    mu = jnp.mean(y, axis=-1, keepdims=True)
    yc = y - mu
    var = jnp.mean(yc * yc, axis=-1, keepdims=True)
    return yc * lax.rsqrt(var + LN_EPS) * g + b


def _rms_rows(y, g):
    return y * lax.rsqrt(jnp.mean(y * y, axis=-1, keepdims=True) + RMS_EPS) * g


def _ada_kernel(c_ref, w_ref, b_ref, o_ref):
    c = c_ref[...]
    s = (c * jax.nn.sigmoid(c)).astype(BF16)
    o_ref[0] = jnp.dot(s, w_ref[0].astype(BF16), preferred_element_type=F32) + b_ref[0]


def _ada_all(c_all, w_ada, b_ada):
    depth, d, n = w_ada.shape
    bt = c_all.shape[0]
    tn = 1536
    return pl.pallas_call(
        _ada_kernel,
        grid=(depth, n // tn),
        in_specs=[pl.BlockSpec((bt, d), lambda l, j: (0, 0)),
                  pl.BlockSpec((1, d, tn), lambda l, j: (l, 0, j)),
                  pl.BlockSpec((1, 1, tn), lambda l, j: (l, 0, j))],
        out_specs=pl.BlockSpec((1, bt, tn), lambda l, j: (l, 0, j)),
        out_shape=jax.ShapeDtypeStruct((depth, bt, n), F32),
        compiler_params=_cparams(("parallel", "parallel")),
        name="ada_mod",
    )(c_all, w_ada, b_ada.reshape(depth, 1, n))


def _fold_kernel(uq_ref, uk_ref, uv_ref, wo_ref, qlat_ref, fold_ref):
    for hh in range(MLA_HEADS):
        a = uq_ref[0, hh]
        b = uk_ref[0, hh]
        ql = lax.dot_general(a, b, (((1,), (1,)), ((), ())), precision=HIGHEST,
                             preferred_element_type=F32)
        qlat_ref[0, :, hh * MLA_KV_LORA:(hh + 1) * MLA_KV_LORA] = (ql * QUERY_SCALE).astype(BF16)
        fd = jnp.dot(uv_ref[0, hh], wo_ref[0, hh * MLA_V:(hh + 1) * MLA_V, :], precision=HIGHEST,
                     preferred_element_type=F32)
        fold_ref[0, hh * MLA_KV_LORA:(hh + 1) * MLA_KV_LORA, :] = fd.astype(BF16)


def _fold_weights(mla_w_uq, mla_w_uk, mla_w_uv, w_out):
    depth = mla_w_uq.shape[0]
    uq_n = jnp.transpose(mla_w_uq[..., :MLA_NOPE], (0, 2, 1, 3))
    uk_t = jnp.transpose(mla_w_uk, (0, 2, 1, 3))
    uv_t = jnp.transpose(mla_w_uv, (0, 2, 1, 3))
    mla_rows = MLA_HEADS * MLA_V
    mla_blk = (GLA_WIDTH + LRU_WIDTH) // mla_rows
    return pl.pallas_call(
        _fold_kernel,
        grid=(depth,),
        in_specs=[pl.BlockSpec((1, MLA_HEADS, MLA_Q_LORA, MLA_NOPE), lambda l: (l, 0, 0, 0)),
                  pl.BlockSpec((1, MLA_HEADS, MLA_KV_LORA, MLA_NOPE), lambda l: (l, 0, 0, 0)),
                  pl.BlockSpec((1, MLA_HEADS, MLA_KV_LORA, MLA_V), lambda l: (l, 0, 0, 0)),
                  pl.BlockSpec((1, mla_rows, D_MODEL), lambda l: (l, mla_blk, 0))],
        out_specs=[pl.BlockSpec((1, MLA_Q_LORA, MLA_HEADS * MLA_KV_LORA), lambda l: (l, 0, 0)),
                   pl.BlockSpec((1, MLA_HEADS * MLA_KV_LORA, D_MODEL), lambda l: (l, 0, 0))],
        out_shape=[jax.ShapeDtypeStruct((depth, MLA_Q_LORA, MLA_HEADS * MLA_KV_LORA), BF16),
                   jax.ShapeDtypeStruct((depth, MLA_HEADS * MLA_KV_LORA, D_MODEL), BF16)],
        compiler_params=_cparams(("parallel",)),
        name="fold_weights",
    )(uq_n, uk_t, uv_t, w_out)


def _inproj_kernel(x_ref, lng_ref, lnb_ref, ada_ref, w_ref, wg_ref, bg_ref, qn_ref, kvn_ref, wq2_ref,
                   cs_ref, proj_ref, qcat_ref, ckv_ref, kr_ref, kcat_ref, *, ln_in):
    bb, tt, d = x_ref.shape
    m = bb * tt
    parts = INPROJ_PARTS if (bb == 1 and tt % (INPROJ_PARTS * 16) == 0) else 1
    mp = m // parts
    x = x_ref[...]
    if ln_in:
        x = _layer_norm_rows(x, lng_ref[...], lnb_ref[...])
    shift = ada_ref[:, 0:1, :]
    scale = ada_ref[:, 1:2, :]
    h = (x * (1.0 + scale) + shift).reshape(m, d).astype(BF16)
    lane = lax.broadcasted_iota(jnp.int32, (mp, LANES), 1)
    rope_lanes = lane < MLA_ROPE

    def project(r):
        rows = pl.ds(r * mp, mp)
        p = jnp.dot(h[r * mp:(r + 1) * mp], w_ref[...], preferred_element_type=F32)
        proj_ref[rows, 0:C_DQ] = p[:, 0:C_DQ]
        tail = p[:, C_TAIL:C_TAIL + LANES]
        z = jnp.dot(tail.astype(BF16), wg_ref[...], preferred_element_type=F32) + bg_ref[...]
        proj_ref[rows, C_LOGF:C_LOGF + LANES] = jax.nn.log_sigmoid(z) / GLA_GATE_TAU
        return p

    def finish(r, p):
        rows = pl.ds(r * mp, mp)
        cs = cs_ref[rows, :]

        def rope(block):
            rot = block * cs
            rot = rot + pltpu.roll(rot, LANES - MLA_ROPE, 1)
            return jnp.where(rope_lanes, rot, 0.0)

        kr = rope(p[:, C_TAIL:C_TAIL + LANES])
        kr_ref[rows, :] = kr[:, 0:MLA_ROPE]
        ckv = _rms_rows(p[:, C_DKV:C_DKV + MLA_KV_LORA], kvn_ref[...])
        ckv_ref[rows, :] = ckv
        kcat_ref[rows, :] = jnp.concatenate([ckv.astype(BF16), kr.astype(BF16)], axis=1)

        dqn = _rms_rows(p[:, C_DQ:C_DQ + MLA_Q_LORA], qn_ref[...]).astype(BF16)
        q2 = jnp.dot(dqn, wq2_ref[...], preferred_element_type=F32)
        for hh in range(MLA_HEADS):
            lat = q2[:, hh * LANES:(hh + 1) * LANES]
            rp = rope(q2[:, (MLA_HEADS + hh) * LANES:(MLA_HEADS + hh + 1) * LANES])
            qc = jnp.concatenate([lat, rp], axis=1).astype(qcat_ref.dtype)
            if parts == 1:
                qcat_ref[:, hh] = qc.reshape(bb, tt, QK_WIDTH)
            else:
                qcat_ref[0, hh, rows, :] = qc

    projected = [project(r) for r in range(parts)]
    for r in range(parts):
        finish(r, projected[r])


def _inproj(x, ln_in_gb, ada, w_in_p, wg_p, bg, qn, kvn, wq2, cs, bb, tt, layer):
    b, t, d = x.shape
    n = b * t
    m = bb * tt
    grid = (b // bb, t // tt)
    nt = t // tt
    const = lambda i, j: (0, 0)
    row = lambda i, j: (i * nt + j, 0)
    return pl.pallas_call(
        functools.partial(_inproj_kernel, ln_in=(layer == 0)),
        grid=grid,
        in_specs=[pl.BlockSpec((bb, tt, d), lambda i, j: (i, j, 0)),
                  pl.BlockSpec((1, d), const),
                  pl.BlockSpec((1, d), const),
                  pl.BlockSpec((bb, 6, d), lambda i, j: (i, 0, 0)),
                  pl.BlockSpec((None, d, W_IN_COLS), lambda i, j: (layer, 0, 0)),
                  pl.BlockSpec((LANES, LANES), const),
                  pl.BlockSpec((1, LANES), const),
                  pl.BlockSpec((1, MLA_Q_LORA), const),
                  pl.BlockSpec((1, MLA_KV_LORA), const),
                  pl.BlockSpec((None, MLA_Q_LORA, 2 * MLA_HEADS * LANES), lambda i, j: (layer, 0, 0)),
                  pl.BlockSpec((m, LANES), lambda i, j: (j, 0))],
        out_specs=[pl.BlockSpec((m, PROJ_COLS), row),
                   pl.BlockSpec((bb, MLA_HEADS, tt, QK_WIDTH), lambda i, j: (i, 0, j, 0)),
                   pl.BlockSpec((m, MLA_KV_LORA), row),
                   pl.BlockSpec((m, MLA_ROPE), row),
                   pl.BlockSpec((m, QK_WIDTH), row)],
        out_shape=[jax.ShapeDtypeStruct((n, PROJ_COLS), F32),
                   jax.ShapeDtypeStruct((b, MLA_HEADS, t, QK_WIDTH), BF16 if tt % 16 == 0 else F32),
                   jax.ShapeDtypeStruct((n, MLA_KV_LORA), F32),
                   jax.ShapeDtypeStruct((n, MLA_ROPE), F32),
                   jax.ShapeDtypeStruct((n, QK_WIDTH), BF16)],
        compiler_params=_cparams(("parallel", "parallel")),
        name="in_proj",
    )(x, ln_in_gb[0], ln_in_gb[1], ada, w_in_p, wg_p, bg, qn, kvn, wq2, cs)


def _gla_kernel(q_ref, k_ref, v_ref, gg_ref, g_ref, s0_ref, ng_ref, ltri_ref, ind_ref,
                msk_ref, seg_ref, o_ref, sT_ref, st_sc, b_sc, o_sc, *, nb, tt, c):
    tj = pl.program_id(1)
    n_chunks = tt // c
    groups = c // SUBLANES
    ltri = ltri_ref[...]
    ind = ind_ref[...]
    msk = msk_ref[...]
    row_iota = lax.broadcasted_iota(jnp.int32, (c, LANES), 0)

    on_diag = ind.astype(F32) > 0.5

    @pl.when(tj == 0)
    def _():
        for j in range(nb):
            s_cat = jnp.concatenate([s0_ref[j, hh] for hh in range(GLA_HEADS)], axis=1)
            s_bd = jnp.where(on_diag, jnp.concatenate([s_cat] * GLA_HEADS, axis=0), 0.0)
            st_sc[j] = s_bd.T

    def chunk(ci, carry):
        for j in range(nb):
            r0 = pl.multiple_of(ci * c, c)
            g = g_ref[j, pl.ds(r0, c), :]
            b = jnp.dot(ltri, g, precision=HIGHEST, preferred_element_type=F32)
            b_sc[j] = b
            q = q_ref[j, pl.ds(r0, c), :]
            k = k_ref[j, pl.ds(r0, c), :]
            v = v_ref[j, pl.ds(r0, c), :]
            blast = b_sc[j, pl.ds(c - 1, 1), :]
            qe = q * jnp.exp(b)
            ke = k * jnp.exp(blast - b)
            st = st_sc[j]
            o_inter = lax.dot_general(qe.astype(BF16), st.astype(BF16), (((1,), (1,)), ((), ())),
                                      preferred_element_type=F32)
            ut = lax.dot_general(v.astype(BF16), ke.astype(BF16), (((0,), (0,)), ((), ())),
                                 preferred_element_type=F32)
            st_sc[j] = st * jnp.exp(blast) + ut * msk

            o_blk = [None] * groups
            for g0 in range(groups):
                lo = g0 * SUBLANES
                pieces = []
                for s in range(lo, lo + SUBLANES):
                    bs = b_sc[j, pl.ds(s, 1), :]
                    ks = k_ref[j, pl.ds(r0 + s, 1), :]
                    e = jnp.exp(b[lo:, :] - bs)
                    head = jnp.where(row_iota[lo:lo + SUBLANES, :] >= s, e[0:SUBLANES, :], 0.0)
                    e = head if c - lo == SUBLANES else jnp.concatenate([head, e[SUBLANES:, :]], axis=0)
                    pieces.append(e * q[lo:, :] * ks)
                w = jnp.concatenate(pieces, axis=0).astype(BF16)
                a = jnp.dot(w, ind, preferred_element_type=F32)
                rows = c - lo
                for idx in range(SUBLANES):
                    vs = v_ref[j, pl.ds(r0 + lo + idx, 1), :]
                    for rb in range(g0, groups):
                        piece = a[idx * rows + (rb - g0) * SUBLANES:
                                  idx * rows + (rb - g0 + 1) * SUBLANES, :] * vs
                        o_blk[rb] = piece if o_blk[rb] is None else o_blk[rb] + piece
            o_intra = jnp.concatenate(o_blk, axis=0) if groups > 1 else o_blk[0]
            o_sc[j, pl.ds(r0, c), :] = o_inter + o_intra
        return carry

    lax.fori_loop(0, n_chunks, chunk, 0, unroll=2 if n_chunks % 2 == 0 else 1)

    @pl.when(tj == pl.num_programs(1) - 1)
    def _():
        for j in range(nb):
            s_bd = st_sc[j].T
            for hh in range(GLA_HEADS):
                sT_ref[j, hh] = s_bd[hh * GLA_DK:(hh + 1) * GLA_DK, hh * GLA_DV:(hh + 1) * GLA_DV]

    ng = ng_ref[...]
    seg = seg_ref[...]

    def epi(j, carry):
        o = o_sc[j]
        ms = jnp.dot(o * o, seg, precision=HIGHEST, preferred_element_type=F32)
        gg = gg_ref[j]
        o_ref[j] = o * lax.rsqrt(ms + RMS_EPS) * ng * (gg * jax.nn.sigmoid(gg))
        return carry

    lax.fori_loop(0, nb, epi, 0)


def _gla(proj, s0T, ng, b, t, nb, tt):
    n = b * t
    c = min(GLA_CHUNK, t)
    proj3 = proj.reshape(b, t, PROJ_COLS)
    ltri = jnp.asarray(np.tril(np.ones((c, c), np.float32)))
    hk = np.arange(GLA_HEADS * GLA_DK) // GLA_DK
    hv = np.arange(GLA_WIDTH) // GLA_DV
    ind = jnp.asarray((hk[:, None] == hv[None, :]).astype(np.float32)).astype(BF16)
    msk = jnp.asarray((hv[:, None] == hk[None, :]).astype(np.float32))
    seg = jnp.asarray((hv[:, None] == hv[None, :]).astype(np.float32) / GLA_DV)
    const = lambda i, j: (0, 0)
    kern = functools.partial(_gla_kernel, nb=nb, tt=tt, c=c)

    def cols(width, col0):
        return pl.BlockSpec((nb, tt, width), lambda i, j: (i, j, col0 // width))

    o_gla, sT = pl.pallas_call(
        kern,
        grid=(b // nb, t // tt),
        in_specs=[cols(LANES, C_Q), cols(LANES, C_K), cols(GLA_WIDTH, C_V), cols(GLA_WIDTH, C_GG),
                  cols(LANES, C_LOGF),
                  pl.BlockSpec((nb, GLA_HEADS, GLA_DK, GLA_DV), lambda i, j: (i, 0, 0, 0)),
                  pl.BlockSpec((1, GLA_WIDTH), const),
                  pl.BlockSpec((c, c), const),
                  pl.BlockSpec((LANES, GLA_WIDTH), const),
                  pl.BlockSpec((GLA_WIDTH, LANES), const),
                  pl.BlockSpec((GLA_WIDTH, GLA_WIDTH), const)],
        out_specs=[pl.BlockSpec((nb, tt, GLA_WIDTH), lambda i, j: (i, j, 0)),
                   pl.BlockSpec((nb, GLA_HEADS, GLA_DK, GLA_DV), lambda i, j: (i, 0, 0, 0))],
        out_shape=[jax.ShapeDtypeStruct((b, t, GLA_WIDTH), F32),
                   jax.ShapeDtypeStruct((b, GLA_HEADS, GLA_DK, GLA_DV), F32)],
        scratch_shapes=[pltpu.VMEM((nb, GLA_WIDTH, LANES), F32),
                        pltpu.VMEM((nb, c, LANES), F32),
                        pltpu.VMEM((nb, tt, GLA_WIDTH), F32)],
        compiler_params=_cparams(("parallel", "arbitrary")),
        name="gla",
    )(proj3, proj3, proj3, proj3, proj3, s0T, ng, ltri, ind, msk, seg)
    return o_gla.reshape(n, GLA_WIDTH), sT


def _lru_kernel(lx_ref, lg_ref, cbuf_ref, h0_ref, cw_ref, cb_ref, wax_ref, bax_ref, sp_ref,
                o_ref, hn_ref, cn_ref, xp_sc, a_sc, u_sc, *, nb, t):
    pad = SUBLANES
    hist = CONV_WIDTH - 1
    row8 = lax.broadcasted_iota(jnp.int32, (SUBLANES, LRU_WIDTH), 0)
    rows_per_iter = min(32, t)
    sub = rows_per_iter // SUBLANES

    def scan_block(a, u):
        for dd in (1, 2, 4):
            a_s = jnp.where(row8 >= dd, pltpu.roll(a, dd, 0), 1.0)
            u_s = jnp.where(row8 >= dd, pltpu.roll(u, dd, 0), 0.0)
            u = a * u_s + u
            a = a * a_s
        return a, u

    for j in range(nb):
        base = j * t
        x = lx_ref[pl.ds(base, t), :]
        xp_sc[pl.ds(0, pad), :] = jnp.zeros((pad, LRU_WIDTH), F32)
        xp_sc[pl.ds(pad - hist, hist), :] = cbuf_ref[j]
        xp_sc[pl.ds(pad, pad), :] = x[0:pad, :]
        xc = cb_ref[...]
        for kk in range(CONV_WIDTH):
            xc = xc + xp_sc[pl.ds(pad - hist + kk, pad), :] * cw_ref[pl.ds(kk, 1), :]
        if t > pad:
            body = cb_ref[...] + x * cw_ref[pl.ds(hist, 1), :]
            for kk in range(hist):
                body = body + pltpu.roll(x, hist - kk, 0) * cw_ref[pl.ds(kk, 1), :]
            xc = jnp.concatenate([xc, body[pad:, :]], axis=0)
        cn_ref[j] = lx_ref[pl.ds(base + t - hist, hist), :]

        ax = jnp.dot(xc.astype(BF16), wax_ref[...], preferred_element_type=F32) + bax_ref[...]
        r = jax.nn.sigmoid(ax[:, 0:LRU_WIDTH])
        ig = jax.nn.sigmoid(ax[:, LRU_WIDTH:2 * LRU_WIDTH])
        log_a = -LRU_C * r * sp_ref[...]
        a = jnp.exp(log_a)
        u = jnp.sqrt((a * a + 1.0) * jnp.tanh(-log_a)) * (ig * xc)
        a_sc[...] = a
        u_sc[...] = u
        u_sc[pl.ds(0, 1), :] = u[0:1, :] + a[0:1, :] * h0_ref[j]

        def step(i, hprev):
            r0 = pl.multiple_of(i * rows_per_iter, rows_per_iter)
            scans = []
            for sb in range(sub):
                ab = a_sc[pl.ds(r0 + sb * SUBLANES, SUBLANES), :]
                ub = u_sc[pl.ds(r0 + sb * SUBLANES, SUBLANES), :]
                scans.append(scan_block(ab, ub))
            for sb in range(sub):
                ac, uc = scans[sb]
                hb = ac * hprev + uc
                u_sc[pl.ds(r0 + sb * SUBLANES, SUBLANES), :] = hb
                hprev = hb[SUBLANES - 1:SUBLANES, :]
            return hprev

        hlast = lax.fori_loop(0, t // rows_per_iter, step, jnp.zeros((1, LRU_WIDTH), F32))
        hn_ref[j] = hlast
        lg = lg_ref[pl.ds(base, t), :]
        o_ref[pl.ds(base, t), :] = jax.nn.gelu(lg, approximate=True) * u_sc[...]


def _lru(proj, cbuf, h0, cw, cb, wax, bax, sp, b, t, nb):
    n = b * t
    rows = nb * t
    const = lambda i: (0, 0)
    kern = functools.partial(_lru_kernel, nb=nb, t=t)
    return pl.pallas_call(
        kern,
        grid=(b // nb,),
        in_specs=[pl.BlockSpec((rows, LRU_WIDTH), lambda i: (i, C_LX // LRU_WIDTH)),
                  pl.BlockSpec((rows, LRU_WIDTH), lambda i: (i, C_LG // LRU_WIDTH)),
                  pl.BlockSpec((nb, CONV_WIDTH - 1, LRU_WIDTH), lambda i: (i, 0, 0)),
                  pl.BlockSpec((nb, 1, LRU_WIDTH), lambda i: (i, 0, 0)),
                  pl.BlockSpec((CONV_WIDTH, LRU_WIDTH), const),
                  pl.BlockSpec((1, LRU_WIDTH), const),
                  pl.BlockSpec((LRU_WIDTH, 2 * LRU_WIDTH), const),
                  pl.BlockSpec((1, 2 * LRU_WIDTH), const),
                  pl.BlockSpec((1, LRU_WIDTH), const)],
        out_specs=[pl.BlockSpec((rows, LRU_WIDTH), lambda i: (i, 0)),
                   pl.BlockSpec((nb, 1, LRU_WIDTH), lambda i: (i, 0, 0)),
                   pl.BlockSpec((nb, CONV_WIDTH - 1, LRU_WIDTH), lambda i: (i, 0, 0))],
        out_shape=[jax.ShapeDtypeStruct((n, LRU_WIDTH), F32),
                   jax.ShapeDtypeStruct((b, 1, LRU_WIDTH), F32),
                   jax.ShapeDtypeStruct((b, CONV_WIDTH - 1, LRU_WIDTH), F32)],
        scratch_shapes=[pltpu.VMEM((2 * SUBLANES, LRU_WIDTH), F32),
                        pltpu.VMEM((t, LRU_WIDTH), F32),
                        pltpu.VMEM((t, LRU_WIDTH), F32)],
        compiler_params=_cparams(("parallel",)),
        name="rg_lru",
    )(proj, proj, cbuf, h0, cw, cb, wax, bax, sp)


def _softmax_step(s, m_ref, l_ref, acc_ref, v, first=False):
    shape3 = m_ref.shape
    rows = shape3[0] * shape3[1]
    reps = s.shape[1] // LANES
    m_cur = jnp.max(s, axis=1, keepdims=True)
    if first:
        m_next = jnp.broadcast_to(m_cur, (rows, LANES))
    else:
        m_prev = m_ref[...].reshape(rows, LANES)
        m_next = jnp.maximum(m_prev, m_cur)
        alpha = jnp.exp2(m_prev - m_next)
    m_wide = m_next if reps == 1 else jnp.concatenate([m_next] * reps, axis=1)
    p = jnp.exp2(s - m_wide)
    l_next = jnp.broadcast_to(jnp.sum(p, axis=1, keepdims=True), (rows, LANES))
    acc_next = jnp.dot(p.astype(BF16), v, preferred_element_type=F32)
    if not first:
        l_next = alpha * l_ref[...].reshape(rows, LANES) + l_next
        acc_next = alpha * acc_ref[...].reshape(rows, LANES) + acc_next
    l_ref[...] = l_next.reshape(shape3)
    acc_ref[...] = acc_next.reshape(shape3)
    m_ref[...] = m_next.reshape(shape3)


def _causal_mask(s, n):
    tpos = lax.broadcasted_iota(jnp.int32, s.shape, 0) & (n - 1)
    kpos = lax.broadcasted_iota(jnp.int32, s.shape, 1)
    return jnp.where(kpos <= tpos, s, -jnp.inf)


def _attn_kernel(q_ref, k_ref, o_ref, m_sc, l_sc, acc_sc, *, tq):
    i = pl.program_id(1)
    half = tq // 2
    nt = (((1,), (1,)), ((), ()))

    k0 = pl.multiple_of(i * tq, tq)
    kb_a = k_ref[0, pl.ds(k0, half), :]
    q = q_ref[0].reshape(MLA_HEADS * tq, QK_WIDTH)
    s_a = lax.dot_general(q, kb_a, nt, preferred_element_type=F32)
    tpos = lax.broadcasted_iota(jnp.int32, s_a.shape, 0) & (tq - 1)
    kpos = lax.broadcasted_iota(jnp.int32, s_a.shape, 1)
    s_a = jnp.where(kpos <= tpos, s_a, -jnp.inf)
    _softmax_step(s_a, m_sc, l_sc, acc_sc, kb_a[:, 0:MLA_KV_LORA], first=True)

    kb_b = k_ref[0, pl.ds(k0 + half, half), :]
    late = pl.ds(half, half)
    q_b = q_ref[0, :, late, :].reshape(MLA_HEADS * half, QK_WIDTH)
    s_b = _causal_mask(lax.dot_general(q_b, kb_b, nt, preferred_element_type=F32), half)
    _softmax_step(s_b, m_sc.at[:, late, :], l_sc.at[:, late, :], acc_sc.at[:, late, :],
                  kb_b[:, 0:MLA_KV_LORA])

    def body(j, carry):
        kb = k_ref[0, pl.ds(pl.multiple_of(j * tq, tq), tq), :]
        qf = q_ref[0].reshape(MLA_HEADS * tq, QK_WIDTH)
        s = lax.dot_general(qf, kb, nt, preferred_element_type=F32)
        _softmax_step(s, m_sc, l_sc, acc_sc, kb[:, 0:MLA_KV_LORA])
        return carry

    lax.fori_loop(0, i, body, 0)

    for hh in range(MLA_HEADS):
        o_ref[0, :, hh * MLA_KV_LORA:(hh + 1) * MLA_KV_LORA] = (acc_sc[hh] / l_sc[hh]).astype(BF16)


def _attn_prompt(qcat, kcat, b, t):
    tq = min(ATTN_BLOCK, t)
    kern = functools.partial(_attn_kernel, tq=tq)
    stat = pltpu.VMEM((MLA_HEADS, tq, LANES), F32)
    return pl.pallas_call(
        kern,
        grid=(b, t // tq),
        in_specs=[pl.BlockSpec((1, MLA_HEADS, tq, QK_WIDTH), lambda bi, i: (bi, 0, i, 0)),
                  pl.BlockSpec((1, t, QK_WIDTH), lambda bi, i: (bi, 0, 0))],
        out_specs=pl.BlockSpec((1, tq, MLA_HEADS * MLA_KV_LORA), lambda bi, i: (bi, i, 0)),
        out_shape=jax.ShapeDtypeStruct((b, t, MLA_HEADS * MLA_KV_LORA), BF16),
        scratch_shapes=[stat, stat, stat],
        compiler_params=_cparams(("parallel", "parallel")),
        name="attn_prompt",
    )(qcat, kcat.reshape(b, t, QK_WIDTH))


PAGE_SLOTS = 3
PAGES_PER_STEP = 128
PAGED_SUB_KEYS = 1024
PAGED_LAG = 2


def _attn_paged_kernel(pt_ref, q_ref, cnew_ref, rnew_ref, ckv_hbm, krt_hbm, o_ref,
                       kc_buf, kr_buf, sem, m_sc, l_sc, acc_sc, *, pages, ts, layer, steps, n_steps):
    j = pl.program_id(1)
    step = pl.program_id(0) * steps + j
    r = MLA_HEADS * ts

    def page_copy(src_page, slot, i):
        return (pltpu.make_async_copy(ckv_hbm.at[layer, src_page],
                                      kc_buf.at[slot, pl.ds(i * PAGE_SIZE, PAGE_SIZE), :],
                                      sem.at[slot, 0]),
                pltpu.make_async_copy(krt_hbm.at[layer, src_page],
                                      kr_buf.at[slot, :, pl.ds(i * PAGE_SIZE, PAGE_SIZE)],
                                      sem.at[slot, 1]))

    def start_step(step_id, slot):
        seq = step_id // steps
        first = (step_id - seq * steps) * pages
        for i in range(pages):
            for cp in page_copy(pt_ref[seq, first + i], slot, i):
                cp.start(priority=i % 2)

    @pl.when(step == 0)
    def _():
        for ahead in range(min(PAGE_SLOTS - 1, n_steps)):
            start_step(ahead, ahead)

    @pl.when(j == 0)
    def _():
        m_sc[...] = jnp.full(m_sc.shape, -jnp.inf, F32)
        l_sc[...] = jnp.zeros(l_sc.shape, F32)
        acc_sc[...] = jnp.zeros(acc_sc.shape, F32)

    slot = lax.rem(step, PAGE_SLOTS)
    for i in range(pages):
        for cp in page_copy(0, slot, i):
            cp.wait()

    q = q_ref[0].reshape(r, QK_WIDTH).astype(BF16)
    q_lat = q[:, 0:MLA_KV_LORA]
    q_rope = q[:, MLA_KV_LORA:MLA_KV_LORA + MLA_ROPE]
    sub = min(PAGED_SUB_KEYS, pages * PAGE_SIZE)
    n_sub = (pages * PAGE_SIZE) // sub

    def scores(ci):
        kc = kc_buf[slot, pl.ds(ci * sub, sub), :].astype(BF16)
        krt = kr_buf[slot, :, pl.ds(ci * sub, sub)].astype(BF16)
        s = (lax.dot_general(q_lat, kc, (((1,), (1,)), ((), ())), preferred_element_type=F32)
             + jnp.dot(q_rope, krt, preferred_element_type=F32))
        m_c = jnp.broadcast_to(jnp.max(s, axis=1, keepdims=True), (r, LANES))
        p = jnp.exp2(s - jnp.concatenate([m_c] * (sub // LANES), axis=1))
        l_c = jnp.broadcast_to(jnp.sum(p, axis=1, keepdims=True), (r, LANES))
        return m_c, l_c, p.astype(BF16), kc

    staged, parts = [], []
    for ci in range(n_sub + PAGED_LAG):
        if ci < n_sub:
            staged.append(scores(ci))
        if ci >= PAGED_LAG:
            m_c, l_c, p_c, kc_c = staged[ci - PAGED_LAG]
            parts.append((m_c, l_c, jnp.dot(p_c, kc_c, preferred_element_type=F32)))
    stat3 = m_sc.shape
    m_prev = m_sc[...].reshape(r, LANES)
    m_next = m_prev
    for m_c, _, _ in parts:
        m_next = jnp.maximum(m_next, m_c)
    alpha = jnp.exp2(m_prev - m_next)
    l_next = alpha * l_sc[...].reshape(r, LANES)
    acc_next = alpha * acc_sc[...].reshape(r, LANES)
    for m_c, l_c, acc_c in parts:
        w_c = jnp.exp2(m_c - m_next)
        l_next = l_next + w_c * l_c
        acc_next = acc_next + w_c * acc_c
    m_sc[...] = m_next.reshape(stat3)
    l_sc[...] = l_next.reshape(stat3)
    acc_sc[...] = acc_next.reshape(stat3)

    nxt = step + (PAGE_SLOTS - 1)

    @pl.when(nxt < n_steps)
    def _():
        start_step(nxt, lax.rem(nxt, PAGE_SLOTS))

    @pl.when(j == steps - 1)
    def _():
        pad = jnp.zeros((LANES - ts, MLA_KV_LORA), F32)
        kc_new = jnp.concatenate([cnew_ref[...], pad], axis=0).astype(BF16)
        kr_new = jnp.concatenate([rnew_ref[...], pad[:, 0:MLA_ROPE]], axis=0).astype(BF16)
        s_new = (lax.dot_general(q_lat, kc_new, (((1,), (1,)), ((), ())), preferred_element_type=F32)
                 + lax.dot_general(q_rope, kr_new, (((1,), (1,)), ((), ())),
                                   preferred_element_type=F32))
        _softmax_step(_causal_mask(s_new, ts), m_sc, l_sc, acc_sc, kc_new)
        for hh in range(MLA_HEADS):
            o_ref[0, :, hh * MLA_KV_LORA:(hh + 1) * MLA_KV_LORA] = acc_sc[hh] / l_sc[hh]


def _attn_paged(qcat, ckv_new, kr_new, cache_ckv, cache_krope_t, page_table, layer, pages):
    b, _, ts, _ = qcat.shape
    n_pages = page_table.shape[1]
    steps = n_pages // pages
    kern = functools.partial(_attn_paged_kernel, pages=pages, ts=ts, layer=layer, steps=steps,
                             n_steps=b * steps)
    stat = pltpu.VMEM((MLA_HEADS, ts, LANES), F32)
    grid_spec = pltpu.PrefetchScalarGridSpec(
        num_scalar_prefetch=1,
        grid=(b, steps),
        in_specs=[pl.BlockSpec((1, MLA_HEADS, ts, QK_WIDTH), lambda bi, j, pt: (bi, 0, 0, 0)),
                  pl.BlockSpec((ts, MLA_KV_LORA), lambda bi, j, pt: (bi, 0)),
                  pl.BlockSpec((ts, MLA_ROPE), lambda bi, j, pt: (bi, 0)),
                  pl.BlockSpec(memory_space=pl.ANY),
                  pl.BlockSpec(memory_space=pl.ANY)],
        out_specs=pl.BlockSpec((1, ts, MLA_HEADS * MLA_KV_LORA), lambda bi, j, pt: (bi, 0, 0)),
        scratch_shapes=[pltpu.VMEM((PAGE_SLOTS, pages * PAGE_SIZE, MLA_KV_LORA), F32),
                        pltpu.VMEM((PAGE_SLOTS, MLA_ROPE, pages * PAGE_SIZE), F32),
                        pltpu.SemaphoreType.DMA((PAGE_SLOTS, 2)),
                        stat, stat, stat])
    return pl.pallas_call(
        kern,
        grid_spec=grid_spec,
        out_shape=jax.ShapeDtypeStruct((b, ts, MLA_HEADS * MLA_KV_LORA), F32),
        compiler_params=_cparams(("arbitrary", "arbitrary")),
        name="attn_paged",
    )(page_table, qcat, ckv_new, kr_new, cache_ckv, cache_krope_t)


FFN_CHUNK = 256


def _mix_ffn_kernel(x_ref, lng_ref, lnb_ref, ada_ref, og_ref, ol_ref, oa_ref, wgl_ref, wf_ref, g1_ref,
                    b1_ref, wgu_ref, wd_ref, g2_ref, b2_ref, o_ref, *, alpha, ln_in):
    bb, tt, d = x_ref.shape
    m = bb * tt
    x = x_ref[...]
    if ln_in:
        x = _layer_norm_rows(x, lng_ref[...], lnb_ref[...])
    gate1 = ada_ref[:, 2:3, :]
    shift2 = ada_ref[:, 3:4, :]
    scale2 = ada_ref[:, 4:5, :]
    gate2 = ada_ref[:, 5:6, :]
    ogl = jnp.concatenate([og_ref[...], ol_ref[...]], axis=1).astype(BF16)
    mix = (jnp.dot(ogl, wgl_ref[...], preferred_element_type=F32)
           + jnp.dot(oa_ref[...].astype(BF16), wf_ref[...], preferred_element_type=F32))
    x1 = _layer_norm_rows(alpha * x + gate1 * mix.reshape(bb, tt, d), g1_ref[...], b1_ref[...])

    h2 = (x1 * (1.0 + scale2) + shift2).reshape(m, d).astype(BF16)
    acc = jnp.zeros((m, d), F32)
    for ci in range(D_FF // FFN_CHUNK):
        lo = ci * FFN_CHUNK
        gf = jnp.dot(h2, wgu_ref[:, lo:lo + FFN_CHUNK], preferred_element_type=F32)
        uf = jnp.dot(h2, wgu_ref[:, D_FF + lo:D_FF + lo + FFN_CHUNK], preferred_element_type=F32)
        act = (gf * jax.nn.sigmoid(gf) * uf).astype(BF16)
        acc = acc + jnp.dot(act, wd_ref[lo:lo + FFN_CHUNK, :], preferred_element_type=F32)
    y = alpha * x1 + gate2 * acc.reshape(bb, tt, d)
    o_ref[...] = _layer_norm_rows(y, g2_ref[...], b2_ref[...])


def _mix_ffn(x, ln_in_gb, ada, o_gla, o_lru, o_lat, wgl, wfold, ln1, wgu, wd, ln2, bb, tt, alpha, layer):
    b, t, d = x.shape
    nt = t // tt
    m = bb * tt
    row = lambda i, j: (i * nt + j, 0)

    def resident(shape):
        return pl.BlockSpec(shape, lambda i, j: (0, 0), pipeline_mode=pl.Buffered(1))

    def layer_slab(rows, cols):
        return pl.BlockSpec((None, rows, cols), lambda i, j: (layer, 0, 0), pipeline_mode=pl.Buffered(1))

    kern = functools.partial(_mix_ffn_kernel, alpha=alpha, ln_in=(layer == 0))
    return pl.pallas_call(
        kern,
        grid=(b // bb, nt),
        in_specs=[pl.BlockSpec((bb, tt, d), lambda i, j: (i, j, 0)),
                  resident((1, d)), resident((1, d)),
                  pl.BlockSpec((bb, 6, d), lambda i, j: (i, 0, 0)),
                  pl.BlockSpec((m, GLA_WIDTH), row),
                  pl.BlockSpec((m, LRU_WIDTH), row),
                  pl.BlockSpec((m, MLA_HEADS * MLA_KV_LORA), row),
                  layer_slab(GLA_WIDTH + LRU_WIDTH, d),
                  layer_slab(MLA_HEADS * MLA_KV_LORA, d),
                  resident((1, d)), resident((1, d)),
                  layer_slab(d, 2 * D_FF),
                  layer_slab(D_FF, d),
                  resident((1, d)), resident((1, d))],
        out_specs=pl.BlockSpec((bb, tt, d), lambda i, j: (i, j, 0)),
        out_shape=jax.ShapeDtypeStruct((b, t, d), F32),
        compiler_params=_cparams(("parallel", "parallel")),
        name="mix_ffn",
    )(x, ln_in_gb[0], ln_in_gb[1], ada, o_gla, o_lru, o_lat, wgl, wfold, ln1[0], ln1[1], wgu, wd,
      ln2[0], ln2[1])


def _rotate_half_cols(w):
    half = MLA_ROPE // 2
    return jnp.concatenate([-w[..., half:], w[..., :half]], axis=-1)


def _prep_stacked_weights(w_in, mla_w_uq, w_qlat):
    depth, d, _ = w_in.shape
    o = np.cumsum([0, 128, 128, 256, 256, 16, 256, 256, 256, 128, 32])
    gq, gk, gv, gg, glr, lx, lgt, dq, dkv, kr = [w_in[:, :, o[i]:o[i + 1]] for i in range(10)]
    tail = jnp.concatenate([kr, _rotate_half_cols(kr), glr,
                            jnp.zeros((depth, d, LANES - 2 * MLA_ROPE - GLA_LOWRANK), F32)], axis=2)
    w_in_p = jnp.concatenate([gq * (GLA_DK ** -0.5), gk, gv, gg, lx, lgt, dq, dkv, tail],
                             axis=2).astype(BF16)
    rope_w = mla_w_uq[:, :, :, MLA_NOPE:] * QUERY_SCALE
    rope_blk = jnp.concatenate(
        [rope_w, _rotate_half_cols(rope_w),
         jnp.zeros((depth, MLA_Q_LORA, MLA_HEADS, LANES - 2 * MLA_ROPE), F32)], axis=-1)
    wq2 = jnp.concatenate(
        [w_qlat, rope_blk.reshape(depth, MLA_Q_LORA, MLA_HEADS * LANES).astype(BF16)], axis=2)
    return w_in_p, wq2


def _prep_layer_weights(l, gla_w_gate, gla_b_gate, gla_norm_g, lru_conv_w, lru_conv_b, lru_w_a,
                        lru_b_a, lru_w_x, lru_b_x, lru_lambda, mla_q_norm_g, mla_kv_norm_g):
    wg_p = jnp.pad(gla_w_gate[l], ((2 * MLA_ROPE, LANES - 2 * MLA_ROPE - GLA_LOWRANK),
                                   (0, 0))).astype(BF16)
    bg = gla_b_gate[l].reshape(1, LANES)
    ng = jnp.tile(gla_norm_g[l], GLA_HEADS).reshape(1, GLA_WIDTH)

    def block_diag(wb):
        on_diag = jnp.eye(LRU_BLOCKS, dtype=bool)[:, None, :, None]
        return jnp.where(on_diag, wb[:, :, None, :], 0.0).reshape(LRU_WIDTH, LRU_WIDTH)

    wax = jnp.concatenate([block_diag(lru_w_a[l]), block_diag(lru_w_x[l])], axis=1).astype(BF16)
    bax = jnp.concatenate([lru_b_a[l], lru_b_x[l]]).reshape(1, 2 * LRU_WIDTH)
    sp = jax.nn.softplus(-lru_lambda[l].astype(F32)).reshape(1, LRU_WIDTH)
    return dict(
        wg_p=wg_p, bg=bg, ng=ng,
        qn=mla_q_norm_g[l].reshape(1, MLA_Q_LORA), kvn=mla_kv_norm_g[l].reshape(1, MLA_KV_LORA),
        cw=lru_conv_w[l], cb=lru_conv_b[l].reshape(1, LRU_WIDTH), wax=wax, bax=bax, sp=sp)


def _rope_table(pos):
    half = MLA_ROPE // 2
    inv_freq = ROPE_THETA ** (-jnp.arange(half, dtype=F32) / half)
    ang = pos.astype(F32)[:, None] * inv_freq[None, :]
    cos, sin = jnp.cos(ang), jnp.sin(ang)
    return jnp.concatenate([cos, cos, sin, sin,
                            jnp.zeros((pos.shape[0], LANES - 2 * MLA_ROPE), F32)], axis=1)


def _group_layer(l, x, ln_in_gb, ada, lw, big, cs, s0, h0, cbuf, tiles, alpha, ln1, ln2, attend):
    b, t, _ = x.shape
    bb, tt, nb, gla_nb, gla_tt = tiles
    proj, qcat, ckv_new, kr_new, kcat = _inproj(x, ln_in_gb, ada, big["w_in_p"], lw["wg_p"], lw["bg"],
                                                lw["qn"], lw["kvn"], big["wq2"], cs, bb, tt, l)
    o_gla, s_new = _gla(proj, s0, lw["ng"], b, t, gla_nb, gla_tt)
    o_lru, h_new, conv_new = _lru(proj, cbuf, h0, lw["cw"], lw["cb"], lw["wax"], lw["bax"], lw["sp"],
                                  b, t, nb)
    o_lat = attend(qcat, kcat, ckv_new, kr_new)
    x2 = _mix_ffn(x, ln_in_gb, ada, o_gla, o_lru, o_lat.reshape(b * t, -1), big["w_out"], big["w_fold"],
                  ln1, big["wgu"], big["wd"], ln2, bb, tt, alpha, l)
    states = (s_new, h_new.reshape(b, LRU_WIDTH), conv_new,
              ckv_new.reshape(b, t, MLA_KV_LORA), kr_new.reshape(b, t, MLA_ROPE))
    return x2, states


def kernel(x_prompt, x_sample, c_prompt, c_sample, state_gla, state_lru, state_conv, cache_ckv, cache_krope, page_table, ln_in_g, ln_in_b, w_ada, b_ada, w_in, gla_w_gate, gla_b_gate, gla_norm_g, lru_conv_w, lru_conv_b, lru_w_a, lru_b_a, lru_w_x, lru_b_x, lru_lambda, mla_q_norm_g, mla_w_uq, mla_kv_norm_g, mla_w_uk, mla_w_uv, w_out, ln1_g, ln1_b, ffn_w_gu, ffn_w_down, ln2_g, ln2_b):
    bp, tp, d = x_prompt.shape
    bs, ts, _ = x_sample.shape
    depth = w_in.shape[0]
    n_pages = page_table.shape[1]
    past_len = n_pages * PAGE_SIZE
    alpha = (2.0 * depth) ** 0.25

    tiles_p = (1, min(512, tp), 2 if bp % 2 == 0 else 1, 8 if bp % 8 == 0 else 1, min(256, tp))
    tiles_s = (bs, ts, 4 if bs % 4 == 0 else 1, 16 if bs % 16 == 0 else 1, ts)
    pages_per_step = PAGES_PER_STEP if n_pages % PAGES_PER_STEP == 0 else n_pages

    ada = _ada_all(jnp.concatenate([c_prompt, c_sample], axis=0), w_ada, b_ada)
    ada = ada.reshape(depth, bp + bs, 6, d)
    xp, xs = x_prompt, x_sample
    ln_in_gb = (ln_in_g.reshape(1, d), ln_in_b.reshape(1, d))
    w_qlat, w_fold = _fold_weights(mla_w_uq, mla_w_uk, mla_w_uv, w_out)

    cache_krope_t = jnp.swapaxes(cache_krope, 2, 3)
    cs_p = _rope_table(jnp.arange(tp, dtype=jnp.int32))
    cs_s = jnp.tile(_rope_table(past_len + jnp.arange(ts, dtype=jnp.int32)), (bs, 1))
    zero_s = jnp.zeros((bp, GLA_HEADS, GLA_DK, GLA_DV), F32)
    zero_h = jnp.zeros((bp, 1, LRU_WIDTH), F32)
    zero_conv = jnp.zeros((bp, CONV_WIDTH - 1, LRU_WIDTH), F32)
    w_in_p, wq2 = _prep_stacked_weights(w_in, mla_w_uq, w_qlat)
    big = dict(w_in_p=w_in_p, wq2=wq2, w_out=w_out.astype(BF16), w_fold=w_fold,
               wgu=ffn_w_gu.astype(BF16), wd=ffn_w_down.astype(BF16))

    st_p, st_s = [], []
    for l in range(depth):
        lw = _prep_layer_weights(l, gla_w_gate, gla_b_gate, gla_norm_g, lru_conv_w, lru_conv_b,
                                 lru_w_a, lru_b_a, lru_w_x, lru_b_x, lru_lambda, mla_q_norm_g,
                                 mla_kv_norm_g)
        ln1 = (ln1_g[l].reshape(1, d), ln1_b[l].reshape(1, d))
        ln2 = (ln2_g[l].reshape(1, d), ln2_b[l].reshape(1, d))

        def attend_p(qcat, kcat, ckv_new, kr_new):
            return _attn_prompt(qcat, kcat, bp, tp)

        def attend_s(qcat, kcat, ckv_new, kr_new, l=l):
            return _attn_paged(qcat, ckv_new, kr_new, cache_ckv, cache_krope_t, page_table, l,
                               pages_per_step)

        xp, sp = _group_layer(l, xp, ln_in_gb, ada[l, :bp], lw, big, cs_p, zero_s, zero_h, zero_conv,
                              tiles_p, alpha, ln1, ln2, attend_p)
        xs, ss = _group_layer(l, xs, ln_in_gb, ada[l, bp:], lw, big, cs_s, state_gla[l],
                              state_lru[l].reshape(bs, 1, LRU_WIDTH), state_conv[l],
                              tiles_s, alpha, ln1, ln2, attend_s)
        st_p.append(sp)
        st_s.append(ss)

    def stk(outs, j):
        return jnp.stack([o[j] for o in outs])

    return (xp, xs, stk(st_p, 0), stk(st_s, 0), stk(st_p, 1), stk(st_s, 1), stk(st_p, 2), stk(st_s, 2),
            stk(st_p, 3), stk(st_s, 3), stk(st_p, 4), stk(st_s, 4))
```

```python
import functools
import math

import numpy as np
import jax
import jax.numpy as jnp
from jax import lax
from jax.experimental import pallas as pl
from jax.experimental.pallas import tpu as pltpu

F32 = jnp.float32
BF16 = jnp.bfloat16
HIGHEST = lax.Precision.HIGHEST

D_MODEL = 1024
PAGE_SIZE = 128
GLA_HEADS = 4
GLA_DK = 32
GLA_DV = 64
GLA_WIDTH = GLA_HEADS * GLA_DV
GLA_LOWRANK = 16
GLA_GATE_TAU = 16.0
GLA_CHUNK = 32
LRU_WIDTH = 256
LRU_BLOCKS = 4
LRU_BLOCK_W = LRU_WIDTH // LRU_BLOCKS
CONV_WIDTH = 4
LRU_C = 8.0
MLA_HEADS = 8
MLA_NOPE = 64
MLA_ROPE = 32
MLA_V = 64
MLA_Q_LORA = 256
MLA_KV_LORA = 128
ROPE_THETA = 10000.0
D_FF = 2816
LN_EPS = 1e-5
RMS_EPS = 1e-6
ATTN_SCALE = (MLA_NOPE + MLA_ROPE) ** -0.5
QUERY_SCALE = ATTN_SCALE * math.log2(math.e)
INPROJ_PARTS = 2
ATTN_BLOCK = 512

LANES = 128
SUBLANES = 8
VMEM_LIMIT_BYTES = 56 * 1024 * 1024

C_Q, C_K, C_V, C_GG, C_LX, C_LG = 0, 128, 256, 512, 768, 1024
C_DQ, C_DKV, C_TAIL = 1280, 1536, 1664
W_IN_COLS = 1792
C_LOGF = 1280
PROJ_COLS = 1408
QK_WIDTH = 256


def _cparams(sem):
    return pltpu.CompilerParams(dimension_semantics=sem, vmem_limit_bytes=VMEM_LIMIT_BYTES)


def _layer_norm_rows(y, g, b):
    mu = jnp.mean(y, axis=-1, keepdims=True)
    yc = y - mu
    var = jnp.mean(yc * yc, axis=-1, keepdims=True)
    return yc * lax.rsqrt(var + LN_EPS) * g + b


def _rms_rows(y, g):
    return y * lax.rsqrt(jnp.mean(y * y, axis=-1, keepdims=True) + RMS_EPS) * g


def _split3(x):
    hi = x.astype(BF16)
    r1 = x - hi.astype(F32)
    mid = r1.astype(BF16)
    lo = (r1 - mid.astype(F32)).astype(BF16)
    return hi, mid, lo


def _dot_exact_lhs(w, x):
    return sum(jnp.dot(w, t, preferred_element_type=F32) for t in _split3(x))


def _dot_exact_rhs(x, w):
    return sum(jnp.dot(t, w, preferred_element_type=F32) for t in _split3(x))


def _ada_kernel(c_ref, w_ref, b_ref, o_ref):
    c = c_ref[...]
    s = (c * jax.nn.sigmoid(c)).astype(BF16)
    o_ref[0] = jnp.dot(s, w_ref[0].astype(BF16), preferred_element_type=F32) + b_ref[0]


def _ada_all(c_all, w_ada, b_ada):
    depth, d, n = w_ada.shape
    bt = c_all.shape[0]
    tn = 1536
    return pl.pallas_call(
        _ada_kernel,
        grid=(depth, n // tn),
        in_specs=[pl.BlockSpec((bt, d), lambda l, j: (0, 0)),
                  pl.BlockSpec((1, d, tn), lambda l, j: (l, 0, j)),
                  pl.BlockSpec((1, 1, tn), lambda l, j: (l, 0, j))],
        out_specs=pl.BlockSpec((1, bt, tn), lambda l, j: (l, 0, j)),
        out_shape=jax.ShapeDtypeStruct((depth, bt, n), F32),
        compiler_params=_cparams(("parallel", "parallel")),
        name="ada_mod",
    )(c_all, w_ada, b_ada.reshape(depth, 1, n))


def _fold_kernel(uq_ref, uk_ref, uv_ref, wo_ref, qlat_ref, fold_ref):
    for hh in range(MLA_HEADS):
        a = uq_ref[0, hh]
        b = uk_ref[0, hh]
        ql = lax.dot_general(a, b, (((1,), (1,)), ((), ())), precision=HIGHEST,
                             preferred_element_type=F32)
        qlat_ref[0, :, hh * MLA_KV_LORA:(hh + 1) * MLA_KV_LORA] = (ql * QUERY_SCALE).astype(BF16)
        fd = jnp.dot(uv_ref[0, hh], wo_ref[0, hh * MLA_V:(hh + 1) * MLA_V, :], precision=HIGHEST,
                     preferred_element_type=F32)
        fold_ref[0, hh * MLA_KV_LORA:(hh + 1) * MLA_KV_LORA, :] = fd.astype(BF16)


def _fold_weights(mla_w_uq, mla_w_uk, mla_w_uv, w_out):
    depth = mla_w_uq.shape[0]
    uq_n = jnp.transpose(mla_w_uq[..., :MLA_NOPE], (0, 2, 1, 3))
    uk_t = jnp.transpose(mla_w_uk, (0, 2, 1, 3))
    uv_t = jnp.transpose(mla_w_uv, (0, 2, 1, 3))
    mla_rows = MLA_HEADS * MLA_V
    mla_blk = (GLA_WIDTH + LRU_WIDTH) // mla_rows
    return pl.pallas_call(
        _fold_kernel,
        grid=(depth,),
        in_specs=[pl.BlockSpec((1, MLA_HEADS, MLA_Q_LORA, MLA_NOPE), lambda l: (l, 0, 0, 0)),
                  pl.BlockSpec((1, MLA_HEADS, MLA_KV_LORA, MLA_NOPE), lambda l: (l, 0, 0, 0)),
                  pl.BlockSpec((1, MLA_HEADS, MLA_KV_LORA, MLA_V), lambda l: (l, 0, 0, 0)),
                  pl.BlockSpec((1, mla_rows, D_MODEL), lambda l: (l, mla_blk, 0))],
        out_specs=[pl.BlockSpec((1, MLA_Q_LORA, MLA_HEADS * MLA_KV_LORA), lambda l: (l, 0, 0)),
                   pl.BlockSpec((1, MLA_HEADS * MLA_KV_LORA, D_MODEL), lambda l: (l, 0, 0))],
        out_shape=[jax.ShapeDtypeStruct((depth, MLA_Q_LORA, MLA_HEADS * MLA_KV_LORA), BF16),
                   jax.ShapeDtypeStruct((depth, MLA_HEADS * MLA_KV_LORA, D_MODEL), BF16)],
        compiler_params=_cparams(("parallel",)),
        name="fold_weights",
    )(uq_n, uk_t, uv_t, w_out)


def _inproj_kernel(x_ref, lng_ref, lnb_ref, ada_ref, w_ref, wg_ref, bg_ref, qn_ref, kvn_ref, wq2_ref,
                   cs_ref, proj_ref, qcat_ref, ckv_ref, kr_ref, kcat_ref, *, ln_in):
    bb, tt, d = x_ref.shape
    m = bb * tt
    parts = INPROJ_PARTS if (bb == 1 and tt % (INPROJ_PARTS * 16) == 0) else 1
    mp = m // parts
    x = x_ref[...]
    if ln_in:
        x = _layer_norm_rows(x, lng_ref[...], lnb_ref[...])
    shift = ada_ref[:, 0:1, :]
    scale = ada_ref[:, 1:2, :]
    h = (x * (1.0 + scale) + shift).reshape(m, d).astype(BF16)
    lane = lax.broadcasted_iota(jnp.int32, (mp, LANES), 1)
    rope_lanes = lane < MLA_ROPE

    def project(r):
        rows = pl.ds(r * mp, mp)
        p = jnp.dot(h[r * mp:(r + 1) * mp], w_ref[...], preferred_element_type=F32)
        proj_ref[rows, 0:C_DQ] = p[:, 0:C_DQ]
        tail = p[:, C_TAIL:C_TAIL + LANES]
        z = jnp.dot(tail.astype(BF16), wg_ref[...], preferred_element_type=F32) + bg_ref[...]
        proj_ref[rows, C_LOGF:C_LOGF + LANES] = jax.nn.log_sigmoid(z) / GLA_GATE_TAU
        return p

    def finish(r, p):
        rows = pl.ds(r * mp, mp)
        cs = cs_ref[rows, :]

        def rope(block):
            rot = block * cs
            rot = rot + pltpu.roll(rot, LANES - MLA_ROPE, 1)
            return jnp.where(rope_lanes, rot, 0.0)

        kr = rope(p[:, C_TAIL:C_TAIL + LANES])
        kr_ref[rows, :] = kr[:, 0:MLA_ROPE]
        ckv = _rms_rows(p[:, C_DKV:C_DKV + MLA_KV_LORA], kvn_ref[...])
        ckv_ref[rows, :] = ckv
        kcat_ref[rows, :] = jnp.concatenate([ckv.astype(BF16), kr.astype(BF16)], axis=1)

        dqn = _rms_rows(p[:, C_DQ:C_DQ + MLA_Q_LORA], qn_ref[...]).astype(BF16)
        q2 = jnp.dot(dqn, wq2_ref[...], preferred_element_type=F32)
        for hh in range(MLA_HEADS):
            lat = q2[:, hh * LANES:(hh + 1) * LANES]
            rp = rope(q2[:, (MLA_HEADS + hh) * LANES:(MLA_HEADS + hh + 1) * LANES])
            qc = jnp.concatenate([lat, rp], axis=1).astype(qcat_ref.dtype)
            if parts == 1:
                qcat_ref[:, hh] = qc.reshape(bb, tt, QK_WIDTH)
            else:
                qcat_ref[0, hh, rows, :] = qc

    projected = [project(r) for r in range(parts)]
    for r in range(parts):
        finish(r, projected[r])


def _inproj(x, ln_in_gb, ada, w_in_p, wg_p, bg, qn, kvn, wq2, cs, bb, tt, layer):
    b, t, d = x.shape
    n = b * t
    m = bb * tt
    grid = (b // bb, t // tt)
    nt = t // tt
    const = lambda i, j: (0, 0)
    row = lambda i, j: (i * nt + j, 0)
    return pl.pallas_call(
        functools.partial(_inproj_kernel, ln_in=(layer == 0)),
        grid=grid,
        in_specs=[pl.BlockSpec((bb, tt, d), lambda i, j: (i, j, 0)),
                  pl.BlockSpec((1, d), const),
                  pl.BlockSpec((1, d), const),
                  pl.BlockSpec((bb, 6, d), lambda i, j: (i, 0, 0)),
                  pl.BlockSpec((None, d, W_IN_COLS), lambda i, j: (layer, 0, 0)),
                  pl.BlockSpec((LANES, LANES), const),
                  pl.BlockSpec((1, LANES), const),
                  pl.BlockSpec((1, MLA_Q_LORA), const),
                  pl.BlockSpec((1, MLA_KV_LORA), const),
                  pl.BlockSpec((None, MLA_Q_LORA, 2 * MLA_HEADS * LANES), lambda i, j: (layer, 0, 0)),
                  pl.BlockSpec((m, LANES), lambda i, j: (j, 0))],
        out_specs=[pl.BlockSpec((m, PROJ_COLS), row),
                   pl.BlockSpec((bb, MLA_HEADS, tt, QK_WIDTH), lambda i, j: (i, 0, j, 0)),
                   pl.BlockSpec((m, MLA_KV_LORA), row),
                   pl.BlockSpec((m, MLA_ROPE), row),
                   pl.BlockSpec((m, QK_WIDTH), row)],
        out_shape=[jax.ShapeDtypeStruct((n, PROJ_COLS), F32),
                   jax.ShapeDtypeStruct((b, MLA_HEADS, t, QK_WIDTH), BF16 if tt % 16 == 0 else F32),
                   jax.ShapeDtypeStruct((n, MLA_KV_LORA), F32),
                   jax.ShapeDtypeStruct((n, MLA_ROPE), F32),
                   jax.ShapeDtypeStruct((n, QK_WIDTH), BF16)],
        compiler_params=_cparams(("parallel", "parallel")),
        name="in_proj",
    )(x, ln_in_gb[0], ln_in_gb[1], ada, w_in_p, wg_p, bg, qn, kvn, wq2, cs)


def _gla_kernel(q_ref, k_ref, v_ref, gg_ref, g_ref, s0_ref, ng_ref, ltri_ref, ind_ref,
                msk_ref, seg_ref, o_ref, sT_ref, st_sc, b_sc, o_sc, *, nb, tt, c):
    tj = pl.program_id(1)
    n_chunks = tt // c
    groups = c // SUBLANES
    ltri = ltri_ref[...]
    ind = ind_ref[...]
    msk = msk_ref[...]
    row_iota = lax.broadcasted_iota(jnp.int32, (c, LANES), 0)

    on_diag = ind.astype(F32) > 0.5

    @pl.when(tj == 0)
    def _():
        for j in range(nb):
            s_cat = jnp.concatenate([s0_ref[j, hh] for hh in range(GLA_HEADS)], axis=1)
            s_bd = jnp.where(on_diag, jnp.concatenate([s_cat] * GLA_HEADS, axis=0), 0.0)
            st_sc[j] = s_bd.T

    def chunk(ci, carry):
        for j in range(nb):
            r0 = pl.multiple_of(ci * c, c)
            g = g_ref[j, pl.ds(r0, c), :]
            b = _dot_exact_lhs(ltri, g)
            b_sc[j] = b
            q = q_ref[j, pl.ds(r0, c), :]
            k = k_ref[j, pl.ds(r0, c), :]
            v = v_ref[j, pl.ds(r0, c), :]
            blast = b_sc[j, pl.ds(c - 1, 1), :]
            qe = q * jnp.exp(b)
            ke = k * jnp.exp(blast - b)
            st = st_sc[j]
            o_inter = lax.dot_general(qe.astype(BF16), st.astype(BF16), (((1,), (1,)), ((), ())),
                                      preferred_element_type=F32)
            ut = lax.dot_general(v.astype(BF16), ke.astype(BF16), (((0,), (0,)), ((), ())),
                                 preferred_element_type=F32)
            st_sc[j] = st * jnp.exp(blast) + ut * msk

            o_blk = [None] * groups
            for g0 in range(groups):
                lo = g0 * SUBLANES
                pieces = []
                for s in range(lo, lo + SUBLANES):
                    bs = b_sc[j, pl.ds(s, 1), :]
                    ks = k_ref[j, pl.ds(r0 + s, 1), :]
                    e = jnp.exp(b[lo:, :] - bs)
                    head = jnp.where(row_iota[lo:lo + SUBLANES, :] >= s, e[0:SUBLANES, :], 0.0)
                    e = head if c - lo == SUBLANES else jnp.concatenate([head, e[SUBLANES:, :]], axis=0)
                    pieces.append(e * q[lo:, :] * ks)
                w = jnp.concatenate(pieces, axis=0).astype(BF16)
                a = jnp.dot(w, ind, preferred_element_type=F32)
                rows = c - lo
                for idx in range(SUBLANES):
                    vs = v_ref[j, pl.ds(r0 + lo + idx, 1), :]
                    for rb in range(g0, groups):
                        piece = a[idx * rows + (rb - g0) * SUBLANES:
                                  idx * rows + (rb - g0 + 1) * SUBLANES, :] * vs
                        o_blk[rb] = piece if o_blk[rb] is None else o_blk[rb] + piece
            o_intra = jnp.concatenate(o_blk, axis=0) if groups > 1 else o_blk[0]
            o_sc[j, pl.ds(r0, c), :] = o_inter + o_intra
        return carry

    lax.fori_loop(0, n_chunks, chunk, 0, unroll=2 if n_chunks % 2 == 0 else 1)

    @pl.when(tj == pl.num_programs(1) - 1)
    def _():
        for j in range(nb):
            s_bd = st_sc[j].T
            for hh in range(GLA_HEADS):
                sT_ref[j, hh] = s_bd[hh * GLA_DK:(hh + 1) * GLA_DK, hh * GLA_DV:(hh + 1) * GLA_DV]

    ng = ng_ref[...]
    seg = seg_ref[...]

    def epi(j, carry):
        o = o_sc[j]
        ms = _dot_exact_rhs(o * o, seg)
        gg = gg_ref[j]
        o_ref[j] = o * lax.rsqrt(ms + RMS_EPS) * ng * (gg * jax.nn.sigmoid(gg))
        return carry

    lax.fori_loop(0, nb, epi, 0)


def _gla(proj, s0T, ng, b, t, nb, tt):
    n = b * t
    c = min(GLA_CHUNK, t)
    proj3 = proj.reshape(b, t, PROJ_COLS)
    ltri = jnp.asarray(np.tril(np.ones((c, c), np.float32))).astype(BF16)
    hk = np.arange(GLA_HEADS * GLA_DK) // GLA_DK
    hv = np.arange(GLA_WIDTH) // GLA_DV
    ind = jnp.asarray((hk[:, None] == hv[None, :]).astype(np.float32)).astype(BF16)
    msk = jnp.asarray((hv[:, None] == hk[None, :]).astype(np.float32))
    seg = jnp.asarray((hv[:, None] == hv[None, :]).astype(np.float32) / GLA_DV).astype(BF16)
    const = lambda i, j: (0, 0)
    kern = functools.partial(_gla_kernel, nb=nb, tt=tt, c=c)

    def cols(width, col0):
        return pl.BlockSpec((nb, tt, width), lambda i, j: (i, j, col0 // width))

    o_gla, sT = pl.pallas_call(
        kern,
        grid=(b // nb, t // tt),
        in_specs=[cols(LANES, C_Q), cols(LANES, C_K), cols(GLA_WIDTH, C_V), cols(GLA_WIDTH, C_GG),
                  cols(LANES, C_LOGF),
                  pl.BlockSpec((nb, GLA_HEADS, GLA_DK, GLA_DV), lambda i, j: (i, 0, 0, 0)),
                  pl.BlockSpec((1, GLA_WIDTH), const),
                  pl.BlockSpec((c, c), const),
                  pl.BlockSpec((LANES, GLA_WIDTH), const),
                  pl.BlockSpec((GLA_WIDTH, LANES), const),
                  pl.BlockSpec((GLA_WIDTH, GLA_WIDTH), const)],
        out_specs=[pl.BlockSpec((nb, tt, GLA_WIDTH), lambda i, j: (i, j, 0)),
                   pl.BlockSpec((nb, GLA_HEADS, GLA_DK, GLA_DV), lambda i, j: (i, 0, 0, 0))],
        out_shape=[jax.ShapeDtypeStruct((b, t, GLA_WIDTH), F32),
                   jax.ShapeDtypeStruct((b, GLA_HEADS, GLA_DK, GLA_DV), F32)],
        scratch_shapes=[pltpu.VMEM((nb, GLA_WIDTH, LANES), F32),
                        pltpu.VMEM((nb, c, LANES), F32),
                        pltpu.VMEM((nb, tt, GLA_WIDTH), F32)],
        compiler_params=_cparams(("parallel", "arbitrary")),
        name="gla",
    )(proj3, proj3, proj3, proj3, proj3, s0T, ng, ltri, ind, msk, seg)
    return o_gla.reshape(n, GLA_WIDTH), sT


def _lru_kernel(lx_ref, lg_ref, cbuf_ref, h0_ref, cw_ref, cb_ref, wax_ref, bax_ref, sp_ref,
                o_ref, hn_ref, cn_ref, xp_sc, a_sc, u_sc, *, nb, t):
    pad = SUBLANES
    hist = CONV_WIDTH - 1
    row8 = lax.broadcasted_iota(jnp.int32, (SUBLANES, LRU_WIDTH), 0)
    rows_per_iter = min(32, t)
    sub = rows_per_iter // SUBLANES

    def scan_block(a, u):
        for dd in (1, 2, 4):
            a_s = jnp.where(row8 >= dd, pltpu.roll(a, dd, 0), 1.0)
            u_s = jnp.where(row8 >= dd, pltpu.roll(u, dd, 0), 0.0)
            u = a * u_s + u
            a = a * a_s
        return a, u

    for j in range(nb):
        base = j * t
        x = lx_ref[pl.ds(base, t), :]
        xp_sc[pl.ds(0, pad), :] = jnp.zeros((pad, LRU_WIDTH), F32)
        xp_sc[pl.ds(pad - hist, hist), :] = cbuf_ref[j]
        xp_sc[pl.ds(pad, pad), :] = x[0:pad, :]
        xc = cb_ref[...]
        for kk in range(CONV_WIDTH):
            xc = xc + xp_sc[pl.ds(pad - hist + kk, pad), :] * cw_ref[pl.ds(kk, 1), :]
        if t > pad:
            body = cb_ref[...] + x * cw_ref[pl.ds(hist, 1), :]
            for kk in range(hist):
                body = body + pltpu.roll(x, hist - kk, 0) * cw_ref[pl.ds(kk, 1), :]
            xc = jnp.concatenate([xc, body[pad:, :]], axis=0)
        cn_ref[j] = lx_ref[pl.ds(base + t - hist, hist), :]

        ax = jnp.dot(xc.astype(BF16), wax_ref[...], preferred_element_type=F32) + bax_ref[...]
        r = jax.nn.sigmoid(ax[:, 0:LRU_WIDTH])
        ig = jax.nn.sigmoid(ax[:, LRU_WIDTH:2 * LRU_WIDTH])
        log_a = -LRU_C * r * sp_ref[...]
        a = jnp.exp(log_a)
        u = jnp.sqrt((a * a + 1.0) * jnp.tanh(-log_a)) * (ig * xc)
        a_sc[...] = a
        u_sc[...] = u
        u_sc[pl.ds(0, 1), :] = u[0:1, :] + a[0:1, :] * h0_ref[j]

        def step(i, hprev):
            r0 = pl.multiple_of(i * rows_per_iter, rows_per_iter)
            scans = []
            for sb in range(sub):
                ab = a_sc[pl.ds(r0 + sb * SUBLANES, SUBLANES), :]
                ub = u_sc[pl.ds(r0 + sb * SUBLANES, SUBLANES), :]
                scans.append(scan_block(ab, ub))
            for sb in range(sub):
                ac, uc = scans[sb]
                hb = ac * hprev + uc
                u_sc[pl.ds(r0 + sb * SUBLANES, SUBLANES), :] = hb
                hprev = hb[SUBLANES - 1:SUBLANES, :]
            return hprev

        hlast = lax.fori_loop(0, t // rows_per_iter, step, jnp.zeros((1, LRU_WIDTH), F32))
        hn_ref[j] = hlast
        lg = lg_ref[pl.ds(base, t), :]
        o_ref[pl.ds(base, t), :] = jax.nn.gelu(lg, approximate=True) * u_sc[...]


def _lru(proj, cbuf, h0, cw, cb, wax, bax, sp, b, t, nb):
    n = b * t
    rows = nb * t
    const = lambda i: (0, 0)
    kern = functools.partial(_lru_kernel, nb=nb, t=t)
    return pl.pallas_call(
        kern,
        grid=(b // nb,),
        in_specs=[pl.BlockSpec((rows, LRU_WIDTH), lambda i: (i, C_LX // LRU_WIDTH)),
                  pl.BlockSpec((rows, LRU_WIDTH), lambda i: (i, C_LG // LRU_WIDTH)),
                  pl.BlockSpec((nb, CONV_WIDTH - 1, LRU_WIDTH), lambda i: (i, 0, 0)),
                  pl.BlockSpec((nb, 1, LRU_WIDTH), lambda i: (i, 0, 0)),
                  pl.BlockSpec((CONV_WIDTH, LRU_WIDTH), const),
                  pl.BlockSpec((1, LRU_WIDTH), const),
                  pl.BlockSpec((LRU_WIDTH, 2 * LRU_WIDTH), const),
                  pl.BlockSpec((1, 2 * LRU_WIDTH), const),
                  pl.BlockSpec((1, LRU_WIDTH), const)],
        out_specs=[pl.BlockSpec((rows, LRU_WIDTH), lambda i: (i, 0)),
                   pl.BlockSpec((nb, 1, LRU_WIDTH), lambda i: (i, 0, 0)),
                   pl.BlockSpec((nb, CONV_WIDTH - 1, LRU_WIDTH), lambda i: (i, 0, 0))],
        out_shape=[jax.ShapeDtypeStruct((n, LRU_WIDTH), F32),
                   jax.ShapeDtypeStruct((b, 1, LRU_WIDTH), F32),
                   jax.ShapeDtypeStruct((b, CONV_WIDTH - 1, LRU_WIDTH), F32)],
        scratch_shapes=[pltpu.VMEM((2 * SUBLANES, LRU_WIDTH), F32),
                        pltpu.VMEM((t, LRU_WIDTH), F32),
                        pltpu.VMEM((t, LRU_WIDTH), F32)],
        compiler_params=_cparams(("parallel",)),
        name="rg_lru",
    )(proj, proj, cbuf, h0, cw, cb, wax, bax, sp)


def _softmax_step(s, m_ref, l_ref, acc_ref, v, first=False):
    shape3 = m_ref.shape
    rows = shape3[0] * shape3[1]
    reps = s.shape[1] // LANES
    m_cur = jnp.max(s, axis=1, keepdims=True)
    if first:
        m_next = jnp.broadcast_to(m_cur, (rows, LANES))
    else:
        m_prev = m_ref[...].reshape(rows, LANES)
        m_next = jnp.maximum(m_prev, m_cur)
        alpha = jnp.exp2(m_prev - m_next)
    m_wide = m_next if reps == 1 else jnp.concatenate([m_next] * reps, axis=1)
    p = jnp.exp2(s - m_wide)
    l_next = jnp.broadcast_to(jnp.sum(p, axis=1, keepdims=True), (rows, LANES))
    acc_next = jnp.dot(p.astype(BF16), v, preferred_element_type=F32)
    if not first:
        l_next = alpha * l_ref[...].reshape(rows, LANES) + l_next
        acc_next = alpha * acc_ref[...].reshape(rows, LANES) + acc_next
    l_ref[...] = l_next.reshape(shape3)
    acc_ref[...] = acc_next.reshape(shape3)
    m_ref[...] = m_next.reshape(shape3)


def _causal_mask(s, n):
    tpos = lax.broadcasted_iota(jnp.int32, s.shape, 0) & (n - 1)
    kpos = lax.broadcasted_iota(jnp.int32, s.shape, 1)
    return jnp.where(kpos <= tpos, s, -jnp.inf)


def _attn_kernel(q_ref, k_ref, o_ref, m_sc, l_sc, acc_sc, *, tq):
    i = pl.program_id(1)
    half = tq // 2
    nt = (((1,), (1,)), ((), ()))

    k0 = pl.multiple_of(i * tq, tq)
    kb_a = k_ref[0, pl.ds(k0, half), :]
    q = q_ref[0].reshape(MLA_HEADS * tq, QK_WIDTH)
    s_a = lax.dot_general(q, kb_a, nt, preferred_element_type=F32)
    tpos = lax.broadcasted_iota(jnp.int32, s_a.shape, 0) & (tq - 1)
    kpos = lax.broadcasted_iota(jnp.int32, s_a.shape, 1)
    s_a = jnp.where(kpos <= tpos, s_a, -jnp.inf)
    _softmax_step(s_a, m_sc, l_sc, acc_sc, kb_a[:, 0:MLA_KV_LORA], first=True)

    kb_b = k_ref[0, pl.ds(k0 + half, half), :]
    late = pl.ds(half, half)
    q_b = q_ref[0, :, late, :].reshape(MLA_HEADS * half, QK_WIDTH)
    s_b = _causal_mask(lax.dot_general(q_b, kb_b, nt, preferred_element_type=F32), half)
    _softmax_step(s_b, m_sc.at[:, late, :], l_sc.at[:, late, :], acc_sc.at[:, late, :],
                  kb_b[:, 0:MLA_KV_LORA])

    def body(j, carry):
        kb = k_ref[0, pl.ds(pl.multiple_of(j * tq, tq), tq), :]
        qf = q_ref[0].reshape(MLA_HEADS * tq, QK_WIDTH)
        s = lax.dot_general(qf, kb, nt, preferred_element_type=F32)
        _softmax_step(s, m_sc, l_sc, acc_sc, kb[:, 0:MLA_KV_LORA])
        return carry

    lax.fori_loop(0, i, body, 0)

    for hh in range(MLA_HEADS):
        o_ref[0, :, hh * MLA_KV_LORA:(hh + 1) * MLA_KV_LORA] = (acc_sc[hh] / l_sc[hh]).astype(BF16)


def _attn_prompt(qcat, kcat, b, t):
    tq = min(ATTN_BLOCK, t)
    kern = functools.partial(_attn_kernel, tq=tq)
    stat = pltpu.VMEM((MLA_HEADS, tq, LANES), F32)
    return pl.pallas_call(
        kern,
        grid=(b, t // tq),
        in_specs=[pl.BlockSpec((1, MLA_HEADS, tq, QK_WIDTH), lambda bi, i: (bi, 0, i, 0)),
                  pl.BlockSpec((1, t, QK_WIDTH), lambda bi, i: (bi, 0, 0))],
        out_specs=pl.BlockSpec((1, tq, MLA_HEADS * MLA_KV_LORA), lambda bi, i: (bi, i, 0)),
        out_shape=jax.ShapeDtypeStruct((b, t, MLA_HEADS * MLA_KV_LORA), BF16),
        scratch_shapes=[stat, stat, stat],
        compiler_params=_cparams(("parallel", "parallel")),
        name="attn_prompt",
    )(qcat, kcat.reshape(b, t, QK_WIDTH))


PAGE_SLOTS = 3
PAGES_PER_STEP = 128
PAGED_SUB_KEYS = 2048
PAGED_LAG = 2


def _attn_paged_kernel(pt_ref, q_ref, cnew_ref, rnew_ref, ckv_hbm, krt_hbm, o_ref,
                       kc_buf, kr_buf, sem, m_sc, l_sc, acc_sc, *, pages, ts, layer, steps, n_steps):
    j = pl.program_id(1)
    step = pl.program_id(0) * steps + j
    r = MLA_HEADS * ts

    def page_copy(src_page, slot, i):
        return (pltpu.make_async_copy(ckv_hbm.at[layer, src_page],
                                      kc_buf.at[slot, pl.ds(i * PAGE_SIZE, PAGE_SIZE), :],
                                      sem.at[slot, 0]),
                pltpu.make_async_copy(krt_hbm.at[layer, src_page],
                                      kr_buf.at[slot, :, pl.ds(i * PAGE_SIZE, PAGE_SIZE)],
                                      sem.at[slot, 1]))

    def start_step(step_id, slot):
        seq = step_id // steps
        first = (step_id - seq * steps) * pages
        for i in range(pages):
            for cp in page_copy(pt_ref[seq, first + i], slot, i):
                cp.start(priority=i % 2)

    @pl.when(step == 0)
    def _():
        for ahead in range(min(PAGE_SLOTS - 1, n_steps)):
            start_step(ahead, ahead)

    @pl.when(j == 0)
    def _():
        m_sc[...] = jnp.full(m_sc.shape, -jnp.inf, F32)
        l_sc[...] = jnp.zeros(l_sc.shape, F32)
        acc_sc[...] = jnp.zeros(acc_sc.shape, F32)

    slot = lax.rem(step, PAGE_SLOTS)
    for i in range(pages):
        for cp in page_copy(0, slot, i):
            cp.wait()

    q = q_ref[0].reshape(r, QK_WIDTH).astype(BF16)
    q_lat = q[:, 0:MLA_KV_LORA]
    q_rope = q[:, MLA_KV_LORA:MLA_KV_LORA + MLA_ROPE]
    sub = min(PAGED_SUB_KEYS, pages * PAGE_SIZE)
    n_sub = (pages * PAGE_SIZE) // sub

    def scores(ci):
        kc = kc_buf[slot, pl.ds(ci * sub, sub), :].astype(BF16)
        krt = kr_buf[slot, :, pl.ds(ci * sub, sub)].astype(BF16)
        s = (lax.dot_general(q_lat, kc, (((1,), (1,)), ((), ())), preferred_element_type=F32)
             + jnp.dot(q_rope, krt, preferred_element_type=F32))
        m_c = jnp.broadcast_to(jnp.max(s, axis=1, keepdims=True), (r, LANES))
        p = jnp.exp2(s - jnp.concatenate([m_c] * (sub // LANES), axis=1))
        l_c = jnp.broadcast_to(jnp.sum(p, axis=1, keepdims=True), (r, LANES))
        return m_c, l_c, p.astype(BF16), kc

    staged, parts = [], []
    for ci in range(n_sub + PAGED_LAG):
        if ci < n_sub:
            staged.append(scores(ci))
        if ci >= PAGED_LAG:
            m_c, l_c, p_c, kc_c = staged[ci - PAGED_LAG]
            parts.append((m_c, l_c, jnp.dot(p_c, kc_c, preferred_element_type=F32)))
    stat3 = m_sc.shape
    m_prev = m_sc[...].reshape(r, LANES)
    m_next = m_prev
    for m_c, _, _ in parts:
        m_next = jnp.maximum(m_next, m_c)
    alpha = jnp.exp2(m_prev - m_next)
    l_next = alpha * l_sc[...].reshape(r, LANES)
    acc_next = alpha * acc_sc[...].reshape(r, LANES)
    for m_c, l_c, acc_c in parts:
        w_c = jnp.exp2(m_c - m_next)
        l_next = l_next + w_c * l_c
        acc_next = acc_next + w_c * acc_c
    m_sc[...] = m_next.reshape(stat3)
    l_sc[...] = l_next.reshape(stat3)
    acc_sc[...] = acc_next.reshape(stat3)

    nxt = step + (PAGE_SLOTS - 1)

    @pl.when(nxt < n_steps)
    def _():
        start_step(nxt, lax.rem(nxt, PAGE_SLOTS))

    @pl.when(j == steps - 1)
    def _():
        pad = jnp.zeros((LANES - ts, MLA_KV_LORA), F32)
        kc_new = jnp.concatenate([cnew_ref[...], pad], axis=0).astype(BF16)
        kr_new = jnp.concatenate([rnew_ref[...], pad[:, 0:MLA_ROPE]], axis=0).astype(BF16)
        s_new = (lax.dot_general(q_lat, kc_new, (((1,), (1,)), ((), ())), preferred_element_type=F32)
                 + lax.dot_general(q_rope, kr_new, (((1,), (1,)), ((), ())),
                                   preferred_element_type=F32))
        _softmax_step(_causal_mask(s_new, ts), m_sc, l_sc, acc_sc, kc_new)
        for hh in range(MLA_HEADS):
            o_ref[0, :, hh * MLA_KV_LORA:(hh + 1) * MLA_KV_LORA] = acc_sc[hh] / l_sc[hh]


def _attn_paged(qcat, ckv_new, kr_new, cache_ckv, cache_krope_t, page_table, layer, pages):
    b, _, ts, _ = qcat.shape
    n_pages = page_table.shape[1]
    steps = n_pages // pages
    kern = functools.partial(_attn_paged_kernel, pages=pages, ts=ts, layer=layer, steps=steps,
                             n_steps=b * steps)
    stat = pltpu.VMEM((MLA_HEADS, ts, LANES), F32)
    grid_spec = pltpu.PrefetchScalarGridSpec(
        num_scalar_prefetch=1,
        grid=(b, steps),
        in_specs=[pl.BlockSpec((1, MLA_HEADS, ts, QK_WIDTH), lambda bi, j, pt: (bi, 0, 0, 0)),
                  pl.BlockSpec((ts, MLA_KV_LORA), lambda bi, j, pt: (bi, 0)),
                  pl.BlockSpec((ts, MLA_ROPE), lambda bi, j, pt: (bi, 0)),
                  pl.BlockSpec(memory_space=pl.ANY),
                  pl.BlockSpec(memory_space=pl.ANY)],
        out_specs=pl.BlockSpec((1, ts, MLA_HEADS * MLA_KV_LORA), lambda bi, j, pt: (bi, 0, 0)),
        scratch_shapes=[pltpu.VMEM((PAGE_SLOTS, pages * PAGE_SIZE, MLA_KV_LORA), F32),
                        pltpu.VMEM((PAGE_SLOTS, MLA_ROPE, pages * PAGE_SIZE), F32),
                        pltpu.SemaphoreType.DMA((PAGE_SLOTS, 2)),
                        stat, stat, stat])
    return pl.pallas_call(
        kern,
        grid_spec=grid_spec,
        out_shape=jax.ShapeDtypeStruct((b, ts, MLA_HEADS * MLA_KV_LORA), F32),
        compiler_params=_cparams(("arbitrary", "arbitrary")),
        name="attn_paged",
    )(page_table, qcat, ckv_new, kr_new, cache_ckv, cache_krope_t)


FFN_CHUNK = 256


def _mix_ffn_kernel(x_ref, lng_ref, lnb_ref, ada_ref, og_ref, ol_ref, oa_ref, wgl_ref, wf_ref, g1_ref,
                    b1_ref, wgu_ref, wd_ref, g2_ref, b2_ref, o_ref, *, alpha, ln_in):
    bb, tt, d = x_ref.shape
    m = bb * tt
    x = x_ref[...]
    if ln_in:
        x = _layer_norm_rows(x, lng_ref[...], lnb_ref[...])
    gate1 = ada_ref[:, 2:3, :]
    shift2 = ada_ref[:, 3:4, :]
    scale2 = ada_ref[:, 4:5, :]
    gate2 = ada_ref[:, 5:6, :]
    ogl = jnp.concatenate([og_ref[...], ol_ref[...]], axis=1).astype(BF16)
    mix = (jnp.dot(ogl, wgl_ref[...], preferred_element_type=F32)
           + jnp.dot(oa_ref[...].astype(BF16), wf_ref[...], preferred_element_type=F32))
    x1 = _layer_norm_rows(alpha * x + gate1 * mix.reshape(bb, tt, d), g1_ref[...], b1_ref[...])

    h2 = (x1 * (1.0 + scale2) + shift2).reshape(m, d).astype(BF16)
    acc = jnp.zeros((m, d), F32)
    for lo in range(0, D_FF, FFN_CHUNK):
        hi = min(lo + FFN_CHUNK, D_FF)
        gf = jnp.dot(h2, wgu_ref[:, lo:hi], preferred_element_type=F32)
        uf = jnp.dot(h2, wgu_ref[:, D_FF + lo:D_FF + hi], preferred_element_type=F32)
        act = (gf * jax.nn.sigmoid(gf) * uf).astype(BF16)
        acc = acc + jnp.dot(act, wd_ref[lo:hi, :], preferred_element_type=F32)
    y = alpha * x1 + gate2 * acc.reshape(bb, tt, d)
    o_ref[...] = _layer_norm_rows(y, g2_ref[...], b2_ref[...])


def _mix_ffn(x, ln_in_gb, ada, o_gla, o_lru, o_lat, wgl, wfold, ln1, wgu, wd, ln2, bb, tt, alpha, layer):
    b, t, d = x.shape
    nt = t // tt
    m = bb * tt
    row = lambda i, j: (i * nt + j, 0)

    def resident(shape):
        return pl.BlockSpec(shape, lambda i, j: (0, 0), pipeline_mode=pl.Buffered(1))

    def layer_slab(rows, cols):
        return pl.BlockSpec((None, rows, cols), lambda i, j: (layer, 0, 0), pipeline_mode=pl.Buffered(1))

    kern = functools.partial(_mix_ffn_kernel, alpha=alpha, ln_in=(layer == 0))
    return pl.pallas_call(
        kern,
        grid=(b // bb, nt),
        in_specs=[pl.BlockSpec((bb, tt, d), lambda i, j: (i, j, 0)),
                  resident((1, d)), resident((1, d)),
                  pl.BlockSpec((bb, 6, d), lambda i, j: (i, 0, 0)),
                  pl.BlockSpec((m, GLA_WIDTH), row),
                  pl.BlockSpec((m, LRU_WIDTH), row),
                  pl.BlockSpec((m, MLA_HEADS * MLA_KV_LORA), row),
                  layer_slab(GLA_WIDTH + LRU_WIDTH, d),
                  layer_slab(MLA_HEADS * MLA_KV_LORA, d),
                  resident((1, d)), resident((1, d)),
                  layer_slab(d, 2 * D_FF),
                  layer_slab(D_FF, d),
                  resident((1, d)), resident((1, d))],
        out_specs=pl.BlockSpec((bb, tt, d), lambda i, j: (i, j, 0)),
        out_shape=jax.ShapeDtypeStruct((b, t, d), F32),
        compiler_params=_cparams(("parallel", "parallel")),
        name="mix_ffn",
    )(x, ln_in_gb[0], ln_in_gb[1], ada, o_gla, o_lru, o_lat, wgl, wfold, ln1[0], ln1[1], wgu, wd,
      ln2[0], ln2[1])


def _rotate_half_cols(w):
    half = MLA_ROPE // 2
    return jnp.concatenate([-w[..., half:], w[..., :half]], axis=-1)


def _prep_stacked_weights(w_in, mla_w_uq, w_qlat):
    depth, d, _ = w_in.shape
    o = np.cumsum([0, 128, 128, 256, 256, 16, 256, 256, 256, 128, 32])
    gq, gk, gv, gg, glr, lx, lgt, dq, dkv, kr = [w_in[:, :, o[i]:o[i + 1]] for i in range(10)]
    tail = jnp.concatenate([kr, _rotate_half_cols(kr), glr,
                            jnp.zeros((depth, d, LANES - 2 * MLA_ROPE - GLA_LOWRANK), F32)], axis=2)
    w_in_p = jnp.concatenate([gq * (GLA_DK ** -0.5), gk, gv, gg, lx, lgt, dq, dkv, tail],
                             axis=2).astype(BF16)
    rope_w = mla_w_uq[:, :, :, MLA_NOPE:] * QUERY_SCALE
    rope_blk = jnp.concatenate(
        [rope_w, _rotate_half_cols(rope_w),
         jnp.zeros((depth, MLA_Q_LORA, MLA_HEADS, LANES - 2 * MLA_ROPE), F32)], axis=-1)
    wq2 = jnp.concatenate(
        [w_qlat, rope_blk.reshape(depth, MLA_Q_LORA, MLA_HEADS * LANES).astype(BF16)], axis=2)
    return w_in_p, wq2


def _prep_layer_weights(l, gla_w_gate, gla_b_gate, gla_norm_g, lru_conv_w, lru_conv_b, lru_w_a,
                        lru_b_a, lru_w_x, lru_b_x, lru_lambda, mla_q_norm_g, mla_kv_norm_g):
    wg_p = jnp.pad(gla_w_gate[l], ((2 * MLA_ROPE, LANES - 2 * MLA_ROPE - GLA_LOWRANK),
                                   (0, 0))).astype(BF16)
    bg = gla_b_gate[l].reshape(1, LANES)
    ng = jnp.tile(gla_norm_g[l], GLA_HEADS).reshape(1, GLA_WIDTH)

    def block_diag(wb):
        on_diag = jnp.eye(LRU_BLOCKS, dtype=bool)[:, None, :, None]
        return jnp.where(on_diag, wb[:, :, None, :], 0.0).reshape(LRU_WIDTH, LRU_WIDTH)

    wax = jnp.concatenate([block_diag(lru_w_a[l]), block_diag(lru_w_x[l])], axis=1).astype(BF16)
    bax = jnp.concatenate([lru_b_a[l], lru_b_x[l]]).reshape(1, 2 * LRU_WIDTH)
    sp = jax.nn.softplus(-lru_lambda[l].astype(F32)).reshape(1, LRU_WIDTH)
    return dict(
        wg_p=wg_p, bg=bg, ng=ng,
        qn=mla_q_norm_g[l].reshape(1, MLA_Q_LORA), kvn=mla_kv_norm_g[l].reshape(1, MLA_KV_LORA),
        cw=lru_conv_w[l], cb=lru_conv_b[l].reshape(1, LRU_WIDTH), wax=wax, bax=bax, sp=sp)


def _rope_table(pos):
    half = MLA_ROPE // 2
    inv_freq = ROPE_THETA ** (-jnp.arange(half, dtype=F32) / half)
    ang = pos.astype(F32)[:, None] * inv_freq[None, :]
    cos, sin = jnp.cos(ang), jnp.sin(ang)
    return jnp.concatenate([cos, cos, sin, sin,
                            jnp.zeros((pos.shape[0], LANES - 2 * MLA_ROPE), F32)], axis=1)


def _group_layer(l, x, ln_in_gb, ada, lw, big, cs, s0, h0, cbuf, tiles, alpha, ln1, ln2, attend):
    b, t, _ = x.shape
    bb, tt, nb, gla_nb, gla_tt = tiles
    proj, qcat, ckv_new, kr_new, kcat = _inproj(x, ln_in_gb, ada, big["w_in_p"], lw["wg_p"], lw["bg"],
                                                lw["qn"], lw["kvn"], big["wq2"], cs, bb, tt, l)
    o_gla, s_new = _gla(proj, s0, lw["ng"], b, t, gla_nb, gla_tt)
    o_lru, h_new, conv_new = _lru(proj, cbuf, h0, lw["cw"], lw["cb"], lw["wax"], lw["bax"], lw["sp"],
                                  b, t, nb)
    o_lat = attend(qcat, kcat, ckv_new, kr_new)
    x2 = _mix_ffn(x, ln_in_gb, ada, o_gla, o_lru, o_lat.reshape(b * t, -1), big["w_out"], big["w_fold"],
                  ln1, big["wgu"], big["wd"], ln2, bb, tt, alpha, l)
    states = (s_new, h_new.reshape(b, LRU_WIDTH), conv_new,
              ckv_new.reshape(b, t, MLA_KV_LORA), kr_new.reshape(b, t, MLA_ROPE))
    return x2, states


def kernel(x_prompt, x_sample, c_prompt, c_sample, state_gla, state_lru, state_conv, cache_ckv, cache_krope, page_table, ln_in_g, ln_in_b, w_ada, b_ada, w_in, gla_w_gate, gla_b_gate, gla_norm_g, lru_conv_w, lru_conv_b, lru_w_a, lru_b_a, lru_w_x, lru_b_x, lru_lambda, mla_q_norm_g, mla_w_uq, mla_kv_norm_g, mla_w_uk, mla_w_uv, w_out, ln1_g, ln1_b, ffn_w_gu, ffn_w_down, ln2_g, ln2_b):
    bp, tp, d = x_prompt.shape
    bs, ts, _ = x_sample.shape
    depth = w_in.shape[0]
    n_pages = page_table.shape[1]
    past_len = n_pages * PAGE_SIZE
    alpha = (2.0 * depth) ** 0.25

    tiles_p = (1, min(512, tp), 2 if bp % 2 == 0 else 1, 8 if bp % 8 == 0 else 1, min(256, tp))
    tiles_s = (bs, ts, 4 if bs % 4 == 0 else 1, 16 if bs % 16 == 0 else 1, ts)
    pages_per_step = PAGES_PER_STEP if n_pages % PAGES_PER_STEP == 0 else n_pages

    ada = _ada_all(jnp.concatenate([c_prompt, c_sample], axis=0), w_ada, b_ada)
    ada = ada.reshape(depth, bp + bs, 6, d)
    xp, xs = x_prompt, x_sample
    ln_in_gb = (ln_in_g.reshape(1, d), ln_in_b.reshape(1, d))
    w_qlat, w_fold = _fold_weights(mla_w_uq, mla_w_uk, mla_w_uv, w_out)

    cache_krope_t = jnp.swapaxes(cache_krope, 2, 3)
    cs_p = _rope_table(jnp.arange(tp, dtype=jnp.int32))
    cs_s = jnp.tile(_rope_table(past_len + jnp.arange(ts, dtype=jnp.int32)), (bs, 1))
    zero_s = jnp.zeros((bp, GLA_HEADS, GLA_DK, GLA_DV), F32)
    zero_h = jnp.zeros((bp, 1, LRU_WIDTH), F32)
    zero_conv = jnp.zeros((bp, CONV_WIDTH - 1, LRU_WIDTH), F32)
    w_in_p, wq2 = _prep_stacked_weights(w_in, mla_w_uq, w_qlat)
    big = dict(w_in_p=w_in_p, wq2=wq2, w_out=w_out.astype(BF16), w_fold=w_fold,
               wgu=ffn_w_gu.astype(BF16), wd=ffn_w_down.astype(BF16))

    st_p, st_s = [], []
    for l in range(depth):
        lw = _prep_layer_weights(l, gla_w_gate, gla_b_gate, gla_norm_g, lru_conv_w, lru_conv_b,
                                 lru_w_a, lru_b_a, lru_w_x, lru_b_x, lru_lambda, mla_q_norm_g,
                                 mla_kv_norm_g)
        ln1 = (ln1_g[l].reshape(1, d), ln1_b[l].reshape(1, d))
        ln2 = (ln2_g[l].reshape(1, d), ln2_b[l].reshape(1, d))

        def attend_p(qcat, kcat, ckv_new, kr_new):
            return _attn_prompt(qcat, kcat, bp, tp)

        def attend_s(qcat, kcat, ckv_new, kr_new, l=l):
            return _attn_paged(qcat, ckv_new, kr_new, cache_ckv, cache_krope_t, page_table, l,
                               pages_per_step)

        xp, sp = _group_layer(l, xp, ln_in_gb, ada[l, :bp], lw, big, cs_p, zero_s, zero_h, zero_conv,
                              tiles_p, alpha, ln1, ln2, attend_p)
        xs, ss = _group_layer(l, xs, ln_in_gb, ada[l, bp:], lw, big, cs_s, state_gla[l],
                              state_lru[l].reshape(bs, 1, LRU_WIDTH), state_conv[l],
                              tiles_s, alpha, ln1, ln2, attend_s)
        st_p.append(sp)
        st_s.append(ss)

    def stk(outs, j):
        return jnp.stack([o[j] for o in outs])

    return (xp, xs, stk(st_p, 0), stk(st_s, 0), stk(st_p, 1), stk(st_s, 1), stk(st_p, 2), stk(st_s, 2),
            stk(st_p, 3), stk(st_s, 3), stk(st_p, 4), stk(st_s, 4))
```

```python
import functools
import math

import numpy as np
import jax
import jax.numpy as jnp
from jax import lax
from jax.experimental import pallas as pl
from jax.experimental.pallas import tpu as pltpu

F32 = jnp.float32
BF16 = jnp.bfloat16
HIGHEST = lax.Precision.HIGHEST

D_MODEL = 1024
PAGE_SIZE = 128
GLA_HEADS = 4
GLA_DK = 32
GLA_DV = 64
GLA_WIDTH = GLA_HEADS * GLA_DV
GLA_LOWRANK = 16
GLA_GATE_TAU = 16.0
GLA_CHUNK = 32
LRU_WIDTH = 256
LRU_BLOCKS = 4
LRU_BLOCK_W = LRU_WIDTH // LRU_BLOCKS
CONV_WIDTH = 4
LRU_C = 8.0
MLA_HEADS = 8
MLA_NOPE = 64
MLA_ROPE = 32
MLA_V = 64
MLA_Q_LORA = 256
MLA_KV_LORA = 128
ROPE_THETA = 10000.0
D_FF = 2816
LN_EPS = 1e-5
RMS_EPS = 1e-6
ATTN_SCALE = (MLA_NOPE + MLA_ROPE) ** -0.5
QUERY_SCALE = ATTN_SCALE * math.log2(math.e)
INPROJ_PARTS = 2
ATTN_BLOCK = 512

LANES = 128
SUBLANES = 8
VMEM_LIMIT_BYTES = 56 * 1024 * 1024

C_Q, C_K, C_V, C_GG, C_LX, C_LG = 0, 128, 256, 512, 768, 1024
C_DQ, C_DKV, C_TAIL = 1280, 1536, 1664
W_IN_COLS = 1792
C_LOGF = 1280
PROJ_COLS = 1408
QK_WIDTH = 256


def _cparams(sem):
    return pltpu.CompilerParams(dimension_semantics=sem, vmem_limit_bytes=VMEM_LIMIT_BYTES)


def _layer_norm_rows(y, g, b):
    mu = jnp.mean(y, axis=-1, keepdims=True)
    yc = y - mu
    var = jnp.mean(yc * yc, axis=-1, keepdims=True)
    return yc * lax.rsqrt(var + LN_EPS) * g + b


def _rms_rows(y, g):
    return y * lax.rsqrt(jnp.mean(y * y, axis=-1, keepdims=True) + RMS_EPS) * g


def _split3(x):
    hi = x.astype(BF16)
    r1 = x - hi.astype(F32)
    mid = r1.astype(BF16)
    lo = (r1 - mid.astype(F32)).astype(BF16)
    return hi, mid, lo


def _dot_exact_lhs(w, x):
    return sum(jnp.dot(w, t, preferred_element_type=F32) for t in _split3(x))


def _dot_exact_rhs(x, w):
    return sum(jnp.dot(t, w, preferred_element_type=F32) for t in _split3(x))


def _ada_kernel(c_ref, w_ref, b_ref, o_ref):
    c = c_ref[...]
    s = (c * jax.nn.sigmoid(c)).astype(BF16)
    o_ref[0] = jnp.dot(s, w_ref[0].astype(BF16), preferred_element_type=F32) + b_ref[0]


def _ada_all(c_all, w_ada, b_ada):
    depth, d, n = w_ada.shape
    bt = c_all.shape[0]
    tn = 1536
    return pl.pallas_call(
        _ada_kernel,
        grid=(depth, n // tn),
        in_specs=[pl.BlockSpec((bt, d), lambda l, j: (0, 0)),
                  pl.BlockSpec((1, d, tn), lambda l, j: (l, 0, j)),
                  pl.BlockSpec((1, 1, tn), lambda l, j: (l, 0, j))],
        out_specs=pl.BlockSpec((1, bt, tn), lambda l, j: (l, 0, j)),
        out_shape=jax.ShapeDtypeStruct((depth, bt, n), F32),
        compiler_params=_cparams(("parallel", "parallel")),
        name="ada_mod",
    )(c_all, w_ada, b_ada.reshape(depth, 1, n))


def _fold_kernel(uq_ref, uk_ref, uv_ref, wo_ref, qlat_ref, fold_ref):
    for hh in range(MLA_HEADS):
        a = uq_ref[0, hh]
        b = uk_ref[0, hh]
        ql = lax.dot_general(a, b, (((1,), (1,)), ((), ())), precision=HIGHEST,
                             preferred_element_type=F32)
        qlat_ref[0, :, hh * MLA_KV_LORA:(hh + 1) * MLA_KV_LORA] = (ql * QUERY_SCALE).astype(BF16)
        fd = jnp.dot(uv_ref[0, hh], wo_ref[0, hh * MLA_V:(hh + 1) * MLA_V, :], precision=HIGHEST,
                     preferred_element_type=F32)
        fold_ref[0, hh * MLA_KV_LORA:(hh + 1) * MLA_KV_LORA, :] = fd.astype(BF16)


def _fold_weights(mla_w_uq, mla_w_uk, mla_w_uv, w_out):
    depth = mla_w_uq.shape[0]
    uq_n = jnp.transpose(mla_w_uq[..., :MLA_NOPE], (0, 2, 1, 3))
    uk_t = jnp.transpose(mla_w_uk, (0, 2, 1, 3))
    uv_t = jnp.transpose(mla_w_uv, (0, 2, 1, 3))
    mla_rows = MLA_HEADS * MLA_V
    mla_blk = (GLA_WIDTH + LRU_WIDTH) // mla_rows
    return pl.pallas_call(
        _fold_kernel,
        grid=(depth,),
        in_specs=[pl.BlockSpec((1, MLA_HEADS, MLA_Q_LORA, MLA_NOPE), lambda l: (l, 0, 0, 0)),
                  pl.BlockSpec((1, MLA_HEADS, MLA_KV_LORA, MLA_NOPE), lambda l: (l, 0, 0, 0)),
                  pl.BlockSpec((1, MLA_HEADS, MLA_KV_LORA, MLA_V), lambda l: (l, 0, 0, 0)),
                  pl.BlockSpec((1, mla_rows, D_MODEL), lambda l: (l, mla_blk, 0))],
        out_specs=[pl.BlockSpec((1, MLA_Q_LORA, MLA_HEADS * MLA_KV_LORA), lambda l: (l, 0, 0)),
                   pl.BlockSpec((1, MLA_HEADS * MLA_KV_LORA, D_MODEL), lambda l: (l, 0, 0))],
        out_shape=[jax.ShapeDtypeStruct((depth, MLA_Q_LORA, MLA_HEADS * MLA_KV_LORA), BF16),
                   jax.ShapeDtypeStruct((depth, MLA_HEADS * MLA_KV_LORA, D_MODEL), BF16)],
        compiler_params=_cparams(("parallel",)),
        name="fold_weights",
    )(uq_n, uk_t, uv_t, w_out)


def _inproj_kernel(x_ref, lng_ref, lnb_ref, ada_ref, w_ref, wg_ref, bg_ref, qn_ref, kvn_ref, wq2_ref,
                   cs_ref, proj_ref, qcat_ref, ckv_ref, kr_ref, kcat_ref, *, ln_in):
    bb, tt, d = x_ref.shape
    m = bb * tt
    parts = INPROJ_PARTS if (bb == 1 and tt % (INPROJ_PARTS * 16) == 0) else 1
    mp = m // parts
    x = x_ref[...]
    if ln_in:
        x = _layer_norm_rows(x, lng_ref[...], lnb_ref[...])
    shift = ada_ref[:, 0:1, :]
    scale = ada_ref[:, 1:2, :]
    h = (x * (1.0 + scale) + shift).reshape(m, d).astype(BF16)
    lane = lax.broadcasted_iota(jnp.int32, (mp, LANES), 1)
    rope_lanes = lane < MLA_ROPE

    def project(r):
        rows = pl.ds(r * mp, mp)
        p = jnp.dot(h[r * mp:(r + 1) * mp], w_ref[...], preferred_element_type=F32)
        proj_ref[rows, 0:C_DQ] = p[:, 0:C_DQ]
        tail = p[:, C_TAIL:C_TAIL + LANES]
        z = jnp.dot(tail.astype(BF16), wg_ref[...], preferred_element_type=F32) + bg_ref[...]
        proj_ref[rows, C_LOGF:C_LOGF + LANES] = jax.nn.log_sigmoid(z) / GLA_GATE_TAU
        return p

    def finish(r, p):
        rows = pl.ds(r * mp, mp)
        cs = cs_ref[rows, :]

        def rope(block):
            rot = block * cs
            rot = rot + pltpu.roll(rot, LANES - MLA_ROPE, 1)
            return jnp.where(rope_lanes, rot, 0.0)

        kr = rope(p[:, C_TAIL:C_TAIL + LANES])
        kr_ref[rows, :] = kr[:, 0:MLA_ROPE]
        ckv = _rms_rows(p[:, C_DKV:C_DKV + MLA_KV_LORA], kvn_ref[...])
        ckv_ref[rows, :] = ckv
        kcat_ref[rows, :] = jnp.concatenate([ckv.astype(BF16), kr.astype(BF16)], axis=1)

        dqn = _rms_rows(p[:, C_DQ:C_DQ + MLA_Q_LORA], qn_ref[...]).astype(BF16)
        q2 = jnp.dot(dqn, wq2_ref[...], preferred_element_type=F32)
        for hh in range(MLA_HEADS):
            lat = q2[:, hh * LANES:(hh + 1) * LANES]
            rp = rope(q2[:, (MLA_HEADS + hh) * LANES:(MLA_HEADS + hh + 1) * LANES])
            qc = jnp.concatenate([lat, rp], axis=1).astype(qcat_ref.dtype)
            if parts == 1:
                qcat_ref[:, hh] = qc.reshape(bb, tt, QK_WIDTH)
            else:
                qcat_ref[0, hh, rows, :] = qc

    projected = [project(r) for r in range(parts)]
    for r in range(parts):
        finish(r, projected[r])


def _inproj(x, ln_in_gb, ada, w_in_p, wg_p, bg, qn, kvn, wq2, cs, bb, tt, layer):
    b, t, d = x.shape
    n = b * t
    m = bb * tt
    grid = (b // bb, t // tt)
    nt = t // tt
    const = lambda i, j: (0, 0)
    row = lambda i, j: (i * nt + j, 0)
    return pl.pallas_call(
        functools.partial(_inproj_kernel, ln_in=(layer == 0)),
        grid=grid,
        in_specs=[pl.BlockSpec((bb, tt, d), lambda i, j: (i, j, 0)),
                  pl.BlockSpec((1, d), const),
                  pl.BlockSpec((1, d), const),
                  pl.BlockSpec((bb, 6, d), lambda i, j: (i, 0, 0)),
                  pl.BlockSpec((None, d, W_IN_COLS), lambda i, j: (layer, 0, 0)),
                  pl.BlockSpec((LANES, LANES), const),
                  pl.BlockSpec((1, LANES), const),
                  pl.BlockSpec((1, MLA_Q_LORA), const),
                  pl.BlockSpec((1, MLA_KV_LORA), const),
                  pl.BlockSpec((None, MLA_Q_LORA, 2 * MLA_HEADS * LANES), lambda i, j: (layer, 0, 0)),
                  pl.BlockSpec((m, LANES), lambda i, j: (j, 0))],
        out_specs=[pl.BlockSpec((m, PROJ_COLS), row),
                   pl.BlockSpec((bb, MLA_HEADS, tt, QK_WIDTH), lambda i, j: (i, 0, j, 0)),
                   pl.BlockSpec((m, MLA_KV_LORA), row),
                   pl.BlockSpec((m, MLA_ROPE), row),
                   pl.BlockSpec((m, QK_WIDTH), row)],
        out_shape=[jax.ShapeDtypeStruct((n, PROJ_COLS), F32),
                   jax.ShapeDtypeStruct((b, MLA_HEADS, t, QK_WIDTH), BF16 if tt % 16 == 0 else F32),
                   jax.ShapeDtypeStruct((n, MLA_KV_LORA), F32),
                   jax.ShapeDtypeStruct((n, MLA_ROPE), F32),
                   jax.ShapeDtypeStruct((n, QK_WIDTH), BF16)],
        compiler_params=_cparams(("parallel", "parallel")),
        name="in_proj",
    )(x, ln_in_gb[0], ln_in_gb[1], ada, w_in_p, wg_p, bg, qn, kvn, wq2, cs)


def _gla_kernel(q_ref, k_ref, v_ref, gg_ref, g_ref, s0_ref, ng_ref, ltri_ref, ind_ref,
                msk_ref, seg_ref, o_ref, sT_ref, st_sc, b_sc, o_sc, *, nb, tt, c):
    tj = pl.program_id(1)
    n_chunks = tt // c
    groups = c // SUBLANES
    ltri = ltri_ref[...]
    ind = ind_ref[...]
    msk = msk_ref[...]
    row_iota = lax.broadcasted_iota(jnp.int32, (c, LANES), 0)

    on_diag = ind.astype(F32) > 0.5

    @pl.when(tj == 0)
    def _():
        for j in range(nb):
            s_cat = jnp.concatenate([s0_ref[j, hh] for hh in range(GLA_HEADS)], axis=1)
            s_bd = jnp.where(on_diag, jnp.concatenate([s_cat] * GLA_HEADS, axis=0), 0.0)
            st_sc[j] = s_bd.T

    def chunk(ci, carry):
        for j in range(nb):
            r0 = pl.multiple_of(ci * c, c)
            g = g_ref[j, pl.ds(r0, c), :]
            b = _dot_exact_lhs(ltri, g)
            b_sc[j] = b
            q = q_ref[j, pl.ds(r0, c), :]
            k = k_ref[j, pl.ds(r0, c), :]
            v = v_ref[j, pl.ds(r0, c), :]
            blast = b_sc[j, pl.ds(c - 1, 1), :]
            qe = q * jnp.exp(b)
            ke = k * jnp.exp(blast - b)
            st = st_sc[j]
            o_inter = lax.dot_general(qe.astype(BF16), st.astype(BF16), (((1,), (1,)), ((), ())),
                                      preferred_element_type=F32)
            ut = lax.dot_general(v.astype(BF16), ke.astype(BF16), (((0,), (0,)), ((), ())),
                                 preferred_element_type=F32)
            st_sc[j] = st * jnp.exp(blast) + ut * msk

            o_blk = [None] * groups
            for g0 in range(groups):
                lo = g0 * SUBLANES
                pieces = []
                for s in range(lo, lo + SUBLANES):
                    bs = b_sc[j, pl.ds(s, 1), :]
                    ks = k_ref[j, pl.ds(r0 + s, 1), :]
                    e = jnp.exp(b[lo:, :] - bs)
                    head = jnp.where(row_iota[lo:lo + SUBLANES, :] >= s, e[0:SUBLANES, :], 0.0)
                    e = head if c - lo == SUBLANES else jnp.concatenate([head, e[SUBLANES:, :]], axis=0)
                    pieces.append(e * q[lo:, :] * ks)
                w = jnp.concatenate(pieces, axis=0).astype(BF16)
                a = jnp.dot(w, ind, preferred_element_type=F32)
                rows = c - lo
                for idx in range(SUBLANES):
                    vs = v_ref[j, pl.ds(r0 + lo + idx, 1), :]
                    for rb in range(g0, groups):
                        piece = a[idx * rows + (rb - g0) * SUBLANES:
                                  idx * rows + (rb - g0 + 1) * SUBLANES, :] * vs
                        o_blk[rb] = piece if o_blk[rb] is None else o_blk[rb] + piece
            o_intra = jnp.concatenate(o_blk, axis=0) if groups > 1 else o_blk[0]
            o_sc[j, pl.ds(r0, c), :] = o_inter + o_intra
        return carry

    lax.fori_loop(0, n_chunks, chunk, 0, unroll=2 if n_chunks % 2 == 0 else 1)

    @pl.when(tj == pl.num_programs(1) - 1)
    def _():
        for j in range(nb):
            s_bd = st_sc[j].T
            for hh in range(GLA_HEADS):
                sT_ref[j, hh] = s_bd[hh * GLA_DK:(hh + 1) * GLA_DK, hh * GLA_DV:(hh + 1) * GLA_DV]

    ng = ng_ref[...]
    seg = seg_ref[...]

    def epi(j, carry):
        o = o_sc[j]
        ms = _dot_exact_rhs(o * o, seg)
        gg = gg_ref[j]
        o_ref[j] = o * lax.rsqrt(ms + RMS_EPS) * ng * (gg * jax.nn.sigmoid(gg))
        return carry

    lax.fori_loop(0, nb, epi, 0)


def _gla(proj, s0T, ng, b, t, nb, tt):
    n = b * t
    c = min(GLA_CHUNK, t)
    proj3 = proj.reshape(b, t, PROJ_COLS)
    ltri = jnp.asarray(np.tril(np.ones((c, c), np.float32))).astype(BF16)
    hk = np.arange(GLA_HEADS * GLA_DK) // GLA_DK
    hv = np.arange(GLA_WIDTH) // GLA_DV
    ind = jnp.asarray((hk[:, None] == hv[None, :]).astype(np.float32)).astype(BF16)
    msk = jnp.asarray((hv[:, None] == hk[None, :]).astype(np.float32))
    seg = jnp.asarray((hv[:, None] == hv[None, :]).astype(np.float32) / GLA_DV).astype(BF16)
    const = lambda i, j: (0, 0)
    kern = functools.partial(_gla_kernel, nb=nb, tt=tt, c=c)

    def cols(width, col0):
        return pl.BlockSpec((nb, tt, width), lambda i, j: (i, j, col0 // width))

    o_gla, sT = pl.pallas_call(
        kern,
        grid=(b // nb, t // tt),
        in_specs=[cols(LANES, C_Q), cols(LANES, C_K), cols(GLA_WIDTH, C_V), cols(GLA_WIDTH, C_GG),
                  cols(LANES, C_LOGF),
                  pl.BlockSpec((nb, GLA_HEADS, GLA_DK, GLA_DV), lambda i, j: (i, 0, 0, 0)),
                  pl.BlockSpec((1, GLA_WIDTH), const),
                  pl.BlockSpec((c, c), const),
                  pl.BlockSpec((LANES, GLA_WIDTH), const),
                  pl.BlockSpec((GLA_WIDTH, LANES), const),
                  pl.BlockSpec((GLA_WIDTH, GLA_WIDTH), const)],
        out_specs=[pl.BlockSpec((nb, tt, GLA_WIDTH), lambda i, j: (i, j, 0)),
                   pl.BlockSpec((nb, GLA_HEADS, GLA_DK, GLA_DV), lambda i, j: (i, 0, 0, 0))],
        out_shape=[jax.ShapeDtypeStruct((b, t, GLA_WIDTH), F32),
                   jax.ShapeDtypeStruct((b, GLA_HEADS, GLA_DK, GLA_DV), F32)],
        scratch_shapes=[pltpu.VMEM((nb, GLA_WIDTH, LANES), F32),
                        pltpu.VMEM((nb, c, LANES), F32),
                        pltpu.VMEM((nb, tt, GLA_WIDTH), F32)],
        compiler_params=_cparams(("parallel", "arbitrary")),
        name="gla",
    )(proj3, proj3, proj3, proj3, proj3, s0T, ng, ltri, ind, msk, seg)
    return o_gla.reshape(n, GLA_WIDTH), sT


def _lru_kernel(lx_ref, lg_ref, cbuf_ref, h0_ref, cw_ref, cb_ref, wax_ref, bax_ref, sp_ref,
                o_ref, hn_ref, cn_ref, xp_sc, a_sc, u_sc, *, nb, t):
    pad = SUBLANES
    hist = CONV_WIDTH - 1
    row8 = lax.broadcasted_iota(jnp.int32, (SUBLANES, LRU_WIDTH), 0)
    rows_per_iter = min(32, t)
    sub = rows_per_iter // SUBLANES

    def scan_block(a, u):
        for dd in (1, 2, 4):
            a_s = jnp.where(row8 >= dd, pltpu.roll(a, dd, 0), 1.0)
            u_s = jnp.where(row8 >= dd, pltpu.roll(u, dd, 0), 0.0)
            u = a * u_s + u
            a = a * a_s
        return a, u

    for j in range(nb):
        base = j * t
        x = lx_ref[pl.ds(base, t), :]
        xp_sc[pl.ds(0, pad), :] = jnp.zeros((pad, LRU_WIDTH), F32)
        xp_sc[pl.ds(pad - hist, hist), :] = cbuf_ref[j]
        xp_sc[pl.ds(pad, pad), :] = x[0:pad, :]
        xc = cb_ref[...]
        for kk in range(CONV_WIDTH):
            xc = xc + xp_sc[pl.ds(pad - hist + kk, pad), :] * cw_ref[pl.ds(kk, 1), :]
        if t > pad:
            body = cb_ref[...] + x * cw_ref[pl.ds(hist, 1), :]
            for kk in range(hist):
                body = body + pltpu.roll(x, hist - kk, 0) * cw_ref[pl.ds(kk, 1), :]
            xc = jnp.concatenate([xc, body[pad:, :]], axis=0)
        cn_ref[j] = lx_ref[pl.ds(base + t - hist, hist), :]

        ax = jnp.dot(xc.astype(BF16), wax_ref[...], preferred_element_type=F32) + bax_ref[...]
        r = jax.nn.sigmoid(ax[:, 0:LRU_WIDTH])
        ig = jax.nn.sigmoid(ax[:, LRU_WIDTH:2 * LRU_WIDTH])
        log_a = -LRU_C * r * sp_ref[...]
        a = jnp.exp(log_a)
        u = jnp.sqrt((a * a + 1.0) * jnp.tanh(-log_a)) * (ig * xc)
        a_sc[...] = a
        u_sc[...] = u
        u_sc[pl.ds(0, 1), :] = u[0:1, :] + a[0:1, :] * h0_ref[j]

        def step(i, hprev):
            r0 = pl.multiple_of(i * rows_per_iter, rows_per_iter)
            scans = []
            for sb in range(sub):
                ab = a_sc[pl.ds(r0 + sb * SUBLANES, SUBLANES), :]
                ub = u_sc[pl.ds(r0 + sb * SUBLANES, SUBLANES), :]
                scans.append(scan_block(ab, ub))
            for sb in range(sub):
                ac, uc = scans[sb]
                hb = ac * hprev + uc
                u_sc[pl.ds(r0 + sb * SUBLANES, SUBLANES), :] = hb
                hprev = hb[SUBLANES - 1:SUBLANES, :]
            return hprev

        hlast = lax.fori_loop(0, t // rows_per_iter, step, jnp.zeros((1, LRU_WIDTH), F32))
        hn_ref[j] = hlast
        lg = lg_ref[pl.ds(base, t), :]
        o_ref[pl.ds(base, t), :] = jax.nn.gelu(lg, approximate=True) * u_sc[...]


def _lru(proj, cbuf, h0, cw, cb, wax, bax, sp, b, t, nb):
    n = b * t
    rows = nb * t
    const = lambda i: (0, 0)
    kern = functools.partial(_lru_kernel, nb=nb, t=t)
    return pl.pallas_call(
        kern,
        grid=(b // nb,),
        in_specs=[pl.BlockSpec((rows, LRU_WIDTH), lambda i: (i, C_LX // LRU_WIDTH)),
                  pl.BlockSpec((rows, LRU_WIDTH), lambda i: (i, C_LG // LRU_WIDTH)),
                  pl.BlockSpec((nb, CONV_WIDTH - 1, LRU_WIDTH), lambda i: (i, 0, 0)),
                  pl.BlockSpec((nb, 1, LRU_WIDTH), lambda i: (i, 0, 0)),
                  pl.BlockSpec((CONV_WIDTH, LRU_WIDTH), const),
                  pl.BlockSpec((1, LRU_WIDTH), const),
                  pl.BlockSpec((LRU_WIDTH, 2 * LRU_WIDTH), const),
                  pl.BlockSpec((1, 2 * LRU_WIDTH), const),
                  pl.BlockSpec((1, LRU_WIDTH), const)],
        out_specs=[pl.BlockSpec((rows, LRU_WIDTH), lambda i: (i, 0)),
                   pl.BlockSpec((nb, 1, LRU_WIDTH), lambda i: (i, 0, 0)),
                   pl.BlockSpec((nb, CONV_WIDTH - 1, LRU_WIDTH), lambda i: (i, 0, 0))],
        out_shape=[jax.ShapeDtypeStruct((n, LRU_WIDTH), F32),
                   jax.ShapeDtypeStruct((b, 1, LRU_WIDTH), F32),
                   jax.ShapeDtypeStruct((b, CONV_WIDTH - 1, LRU_WIDTH), F32)],
        scratch_shapes=[pltpu.VMEM((2 * SUBLANES, LRU_WIDTH), F32),
                        pltpu.VMEM((t, LRU_WIDTH), F32),
                        pltpu.VMEM((t, LRU_WIDTH), F32)],
        compiler_params=_cparams(("parallel",)),
        name="rg_lru",
    )(proj, proj, cbuf, h0, cw, cb, wax, bax, sp)


def _softmax_step(s, m_ref, l_ref, acc_ref, v, first=False):
    shape3 = m_ref.shape
    rows = shape3[0] * shape3[1]
    reps = s.shape[1] // LANES
    m_cur = jnp.max(s, axis=1, keepdims=True)
    if first:
        m_next = jnp.broadcast_to(m_cur, (rows, LANES))
    else:
        m_prev = m_ref[...].reshape(rows, LANES)
        m_next = jnp.maximum(m_prev, m_cur)
        alpha = jnp.exp2(m_prev - m_next)
    m_wide = m_next if reps == 1 else jnp.concatenate([m_next] * reps, axis=1)
    p = jnp.exp2(s - m_wide)
    l_next = jnp.broadcast_to(jnp.sum(p, axis=1, keepdims=True), (rows, LANES))
    acc_next = jnp.dot(p.astype(BF16), v, preferred_element_type=F32)
    if not first:
        l_next = alpha * l_ref[...].reshape(rows, LANES) + l_next
        acc_next = alpha * acc_ref[...].reshape(rows, LANES) + acc_next
    l_ref[...] = l_next.reshape(shape3)
    acc_ref[...] = acc_next.reshape(shape3)
    m_ref[...] = m_next.reshape(shape3)


def _causal_mask(s, n):
    tpos = lax.broadcasted_iota(jnp.int32, s.shape, 0) & (n - 1)
    kpos = lax.broadcasted_iota(jnp.int32, s.shape, 1)
    return jnp.where(kpos <= tpos, s, -jnp.inf)


def _attn_kernel(q_ref, k_ref, o_ref, m_sc, l_sc, acc_sc, *, tq):
    i = pl.program_id(1)
    half = tq // 2
    nt = (((1,), (1,)), ((), ()))

    k0 = pl.multiple_of(i * tq, tq)
    kb_a = k_ref[0, pl.ds(k0, half), :]
    q = q_ref[0].reshape(MLA_HEADS * tq, QK_WIDTH)
    s_a = lax.dot_general(q, kb_a, nt, preferred_element_type=F32)
    tpos = lax.broadcasted_iota(jnp.int32, s_a.shape, 0) & (tq - 1)
    kpos = lax.broadcasted_iota(jnp.int32, s_a.shape, 1)
    s_a = jnp.where(kpos <= tpos, s_a, -jnp.inf)
    _softmax_step(s_a, m_sc, l_sc, acc_sc, kb_a[:, 0:MLA_KV_LORA], first=True)

    kb_b = k_ref[0, pl.ds(k0 + half, half), :]
    late = pl.ds(half, half)
    q_b = q_ref[0, :, late, :].reshape(MLA_HEADS * half, QK_WIDTH)
    s_b = _causal_mask(lax.dot_general(q_b, kb_b, nt, preferred_element_type=F32), half)
    _softmax_step(s_b, m_sc.at[:, late, :], l_sc.at[:, late, :], acc_sc.at[:, late, :],
                  kb_b[:, 0:MLA_KV_LORA])

    def body(j, carry):
        kb = k_ref[0, pl.ds(pl.multiple_of(j * tq, tq), tq), :]
        qf = q_ref[0].reshape(MLA_HEADS * tq, QK_WIDTH)
        s = lax.dot_general(qf, kb, nt, preferred_element_type=F32)
        _softmax_step(s, m_sc, l_sc, acc_sc, kb[:, 0:MLA_KV_LORA])
        return carry

    lax.fori_loop(0, i, body, 0)

    for hh in range(MLA_HEADS):
        o_ref[0, :, hh * MLA_KV_LORA:(hh + 1) * MLA_KV_LORA] = (acc_sc[hh] / l_sc[hh]).astype(BF16)


def _attn_prompt(qcat, kcat, b, t):
    tq = min(ATTN_BLOCK, t)
    kern = functools.partial(_attn_kernel, tq=tq)
    stat = pltpu.VMEM((MLA_HEADS, tq, LANES), F32)
    return pl.pallas_call(
        kern,
        grid=(b, t // tq),
        in_specs=[pl.BlockSpec((1, MLA_HEADS, tq, QK_WIDTH), lambda bi, i: (bi, 0, i, 0)),
                  pl.BlockSpec((1, t, QK_WIDTH), lambda bi, i: (bi, 0, 0))],
        out_specs=pl.BlockSpec((1, tq, MLA_HEADS * MLA_KV_LORA), lambda bi, i: (bi, i, 0)),
        out_shape=jax.ShapeDtypeStruct((b, t, MLA_HEADS * MLA_KV_LORA), BF16),
        scratch_shapes=[stat, stat, stat],
        compiler_params=_cparams(("parallel", "parallel")),
        name="attn_prompt",
    )(qcat, kcat.reshape(b, t, QK_WIDTH))


PAGE_SLOTS = 3
PAGES_PER_STEP = 128
PAGED_SUB_KEYS = 2048
PAGED_LAG = 2


def _attn_paged_kernel(pt_ref, q_ref, cnew_ref, rnew_ref, ckv_hbm, krt_hbm, o_ref,
                       kc_buf, kr_buf, sem, m_sc, l_sc, acc_sc, *, pages, ts, layer, steps, n_steps):
    j = pl.program_id(1)
    step = pl.program_id(0) * steps + j
    r = MLA_HEADS * ts

    def page_copy(src_page, slot, i):
        return (pltpu.make_async_copy(ckv_hbm.at[layer, src_page],
                                      kc_buf.at[slot, pl.ds(i * PAGE_SIZE, PAGE_SIZE), :],
                                      sem.at[slot, 0]),
                pltpu.make_async_copy(krt_hbm.at[layer, src_page],
                                      kr_buf.at[slot, :, pl.ds(i * PAGE_SIZE, PAGE_SIZE)],
                                      sem.at[slot, 1]))

    def start_step(step_id, slot):
        seq = step_id // steps
        first = (step_id - seq * steps) * pages
        for i in range(pages):
            for cp in page_copy(pt_ref[seq, first + i], slot, i):
                cp.start(priority=i % 2)

    @pl.when(step == 0)
    def _():
        for ahead in range(min(PAGE_SLOTS - 1, n_steps)):
            start_step(ahead, ahead)

    @pl.when(j == 0)
    def _():
        m_sc[...] = jnp.full(m_sc.shape, -jnp.inf, F32)
        l_sc[...] = jnp.zeros(l_sc.shape, F32)
        acc_sc[...] = jnp.zeros(acc_sc.shape, F32)

    slot = lax.rem(step, PAGE_SLOTS)
    for i in range(pages):
        for cp in page_copy(0, slot, i):
            cp.wait()

    q = q_ref[0].reshape(r, QK_WIDTH).astype(BF16)
    q_lat = q[:, 0:MLA_KV_LORA]
    q_rope = q[:, MLA_KV_LORA:MLA_KV_LORA + MLA_ROPE]
    sub = min(PAGED_SUB_KEYS, pages * PAGE_SIZE)
    n_sub = (pages * PAGE_SIZE) // sub

    def scores(ci):
        kc = kc_buf[slot, pl.ds(ci * sub, sub), :].astype(BF16)
        krt = kr_buf[slot, :, pl.ds(ci * sub, sub)].astype(BF16)
        s = (lax.dot_general(q_lat, kc, (((1,), (1,)), ((), ())), preferred_element_type=F32)
             + jnp.dot(q_rope, krt, preferred_element_type=F32))
        m_c = jnp.broadcast_to(jnp.max(s, axis=1, keepdims=True), (r, LANES))
        p = jnp.exp2(s - jnp.concatenate([m_c] * (sub // LANES), axis=1))
        l_c = jnp.broadcast_to(jnp.sum(p, axis=1, keepdims=True), (r, LANES))
        return m_c, l_c, p.astype(BF16), kc

    staged, parts = [], []
    for ci in range(n_sub + PAGED_LAG):
        if ci < n_sub:
            staged.append(scores(ci))
        if ci >= PAGED_LAG:
            m_c, l_c, p_c, kc_c = staged[ci - PAGED_LAG]
            parts.append((m_c, l_c, jnp.dot(p_c, kc_c, preferred_element_type=F32)))
    stat3 = m_sc.shape
    m_prev = m_sc[...].reshape(r, LANES)
    m_next = m_prev
    for m_c, _, _ in parts:
        m_next = jnp.maximum(m_next, m_c)
    alpha = jnp.exp2(m_prev - m_next)
    l_next = alpha * l_sc[...].reshape(r, LANES)
    acc_next = alpha * acc_sc[...].reshape(r, LANES)
    for m_c, l_c, acc_c in parts:
        w_c = jnp.exp2(m_c - m_next)
        l_next = l_next + w_c * l_c
        acc_next = acc_next + w_c * acc_c
    m_sc[...] = m_next.reshape(stat3)
    l_sc[...] = l_next.reshape(stat3)
    acc_sc[...] = acc_next.reshape(stat3)

    nxt = step + (PAGE_SLOTS - 1)

    @pl.when(nxt < n_steps)
    def _():
        start_step(nxt, lax.rem(nxt, PAGE_SLOTS))

    @pl.when(j == steps - 1)
    def _():
        pad = jnp.zeros((LANES - ts, MLA_KV_LORA), F32)
        kc_new = jnp.concatenate([cnew_ref[...], pad], axis=0).astype(BF16)
        kr_new = jnp.concatenate([rnew_ref[...], pad[:, 0:MLA_ROPE]], axis=0).astype(BF16)
        s_new = (lax.dot_general(q_lat, kc_new, (((1,), (1,)), ((), ())), preferred_element_type=F32)
                 + lax.dot_general(q_rope, kr_new, (((1,), (1,)), ((), ())),
                                   preferred_element_type=F32))
        _softmax_step(_causal_mask(s_new, ts), m_sc, l_sc, acc_sc, kc_new)
        for hh in range(MLA_HEADS):
            o_ref[0, :, hh * MLA_KV_LORA:(hh + 1) * MLA_KV_LORA] = acc_sc[hh] / l_sc[hh]


def _attn_paged(qcat, ckv_new, kr_new, cache_ckv, cache_krope_t, page_table, layer, pages):
    b, _, ts, _ = qcat.shape
    n_pages = page_table.shape[1]
    steps = n_pages // pages
    kern = functools.partial(_attn_paged_kernel, pages=pages, ts=ts, layer=layer, steps=steps,
                             n_steps=b * steps)
    stat = pltpu.VMEM((MLA_HEADS, ts, LANES), F32)
    grid_spec = pltpu.PrefetchScalarGridSpec(
        num_scalar_prefetch=1,
        grid=(b, steps),
        in_specs=[pl.BlockSpec((1, MLA_HEADS, ts, QK_WIDTH), lambda bi, j, pt: (bi, 0, 0, 0)),
                  pl.BlockSpec((ts, MLA_KV_LORA), lambda bi, j, pt: (bi, 0)),
                  pl.BlockSpec((ts, MLA_ROPE), lambda bi, j, pt: (bi, 0)),
                  pl.BlockSpec(memory_space=pl.ANY),
                  pl.BlockSpec(memory_space=pl.ANY)],
        out_specs=pl.BlockSpec((1, ts, MLA_HEADS * MLA_KV_LORA), lambda bi, j, pt: (bi, 0, 0)),
        scratch_shapes=[pltpu.VMEM((PAGE_SLOTS, pages * PAGE_SIZE, MLA_KV_LORA), F32),
                        pltpu.VMEM((PAGE_SLOTS, MLA_ROPE, pages * PAGE_SIZE), F32),
                        pltpu.SemaphoreType.DMA((PAGE_SLOTS, 2)),
                        stat, stat, stat])
    return pl.pallas_call(
        kern,
        grid_spec=grid_spec,
        out_shape=jax.ShapeDtypeStruct((b, ts, MLA_HEADS * MLA_KV_LORA), F32),
        compiler_params=_cparams(("arbitrary", "arbitrary")),
        name="attn_paged",
    )(page_table, qcat, ckv_new, kr_new, cache_ckv, cache_krope_t)


FFN_CHUNK = 256


def _mix_ffn_kernel(x_ref, lng_ref, lnb_ref, ada_ref, og_ref, ol_ref, oa_ref, wgl_ref, wf_ref, g1_ref,
                    b1_ref, wgu_ref, wd_ref, g2_ref, b2_ref, o_ref, *, alpha, ln_in):
    bb, tt, d = x_ref.shape
    m = bb * tt
    x = x_ref[...]
    if ln_in:
        x = _layer_norm_rows(x, lng_ref[...], lnb_ref[...])
    gate1 = ada_ref[:, 2:3, :]
    shift2 = ada_ref[:, 3:4, :]
    scale2 = ada_ref[:, 4:5, :]
    gate2 = ada_ref[:, 5:6, :]
    ogl = jnp.concatenate([og_ref[...], ol_ref[...]], axis=1).astype(BF16)
    mix = (jnp.dot(ogl, wgl_ref[...], preferred_element_type=F32)
           + jnp.dot(oa_ref[...].astype(BF16), wf_ref[...], preferred_element_type=F32))
    x1 = _layer_norm_rows(alpha * x + gate1 * mix.reshape(bb, tt, d), g1_ref[...], b1_ref[...])

    h2 = (x1 * (1.0 + scale2) + shift2).reshape(m, d).astype(BF16)
    acc = jnp.zeros((m, d), F32)
    for lo in range(0, D_FF, FFN_CHUNK):
        hi = min(lo + FFN_CHUNK, D_FF)
        gf = jnp.dot(h2, wgu_ref[:, lo:hi], preferred_element_type=F32)
        uf = jnp.dot(h2, wgu_ref[:, D_FF + lo:D_FF + hi], preferred_element_type=F32)
        act = (gf * jax.nn.sigmoid(gf) * uf).astype(BF16)
        acc = acc + jnp.dot(act, wd_ref[lo:hi, :], preferred_element_type=F32)
    y = alpha * x1 + gate2 * acc.reshape(bb, tt, d)
    o_ref[...] = _layer_norm_rows(y, g2_ref[...], b2_ref[...])


def _mix_ffn(x, ln_in_gb, ada, o_gla, o_lru, o_lat, wgl, wfold, ln1, wgu, wd, ln2, bb, tt, alpha, layer):
    b, t, d = x.shape
    nt = t // tt
    m = bb * tt
    row = lambda i, j: (i * nt + j, 0)

    def resident(shape):
        return pl.BlockSpec(shape, lambda i, j: (0, 0), pipeline_mode=pl.Buffered(1))

    def layer_slab(rows, cols):
        return pl.BlockSpec((None, rows, cols), lambda i, j: (layer, 0, 0), pipeline_mode=pl.Buffered(1))

    kern = functools.partial(_mix_ffn_kernel, alpha=alpha, ln_in=(layer == 0))
    return pl.pallas_call(
        kern,
        grid=(b // bb, nt),
        in_specs=[pl.BlockSpec((bb, tt, d), lambda i, j: (i, j, 0)),
                  resident((1, d)), resident((1, d)),
                  pl.BlockSpec((bb, 6, d), lambda i, j: (i, 0, 0)),
                  pl.BlockSpec((m, GLA_WIDTH), row),
                  pl.BlockSpec((m, LRU_WIDTH), row),
                  pl.BlockSpec((m, MLA_HEADS * MLA_KV_LORA), row),
                  layer_slab(GLA_WIDTH + LRU_WIDTH, d),
                  layer_slab(MLA_HEADS * MLA_KV_LORA, d),
                  resident((1, d)), resident((1, d)),
                  layer_slab(d, 2 * D_FF),
                  layer_slab(D_FF, d),
                  resident((1, d)), resident((1, d))],
        out_specs=pl.BlockSpec((bb, tt, d), lambda i, j: (i, j, 0)),
        out_shape=jax.ShapeDtypeStruct((b, t, d), F32),
        compiler_params=_cparams(("parallel", "parallel")),
        name="mix_ffn",
    )(x, ln_in_gb[0], ln_in_gb[1], ada, o_gla, o_lru, o_lat, wgl, wfold, ln1[0], ln1[1], wgu, wd,
      ln2[0], ln2[1])


def _rotate_half_cols(w):
    half = MLA_ROPE // 2
    return jnp.concatenate([-w[..., half:], w[..., :half]], axis=-1)


def _prep_stacked_weights(w_in, mla_w_uq, w_qlat):
    depth, d, _ = w_in.shape
    o = np.cumsum([0, 128, 128, 256, 256, 16, 256, 256, 256, 128, 32])
    gq, gk, gv, gg, glr, lx, lgt, dq, dkv, kr = [w_in[:, :, o[i]:o[i + 1]] for i in range(10)]
    tail = jnp.concatenate([kr, _rotate_half_cols(kr), glr,
                            jnp.zeros((depth, d, LANES - 2 * MLA_ROPE - GLA_LOWRANK), F32)], axis=2)
    w_in_p = jnp.concatenate([gq * (GLA_DK ** -0.5), gk, gv, gg, lx, lgt, dq, dkv, tail],
                             axis=2).astype(BF16)
    rope_w = mla_w_uq[:, :, :, MLA_NOPE:] * QUERY_SCALE
    rope_blk = jnp.concatenate(
        [rope_w, _rotate_half_cols(rope_w),
         jnp.zeros((depth, MLA_Q_LORA, MLA_HEADS, LANES - 2 * MLA_ROPE), F32)], axis=-1)
    wq2 = jnp.concatenate(
        [w_qlat, rope_blk.reshape(depth, MLA_Q_LORA, MLA_HEADS * LANES).astype(BF16)], axis=2)
    return w_in_p, wq2


def _prep_layer_weights(l, gla_w_gate, gla_b_gate, gla_norm_g, lru_conv_w, lru_conv_b, lru_w_a,
                        lru_b_a, lru_w_x, lru_b_x, lru_lambda, mla_q_norm_g, mla_kv_norm_g):
    wg_p = jnp.pad(gla_w_gate[l], ((2 * MLA_ROPE, LANES - 2 * MLA_ROPE - GLA_LOWRANK),
                                   (0, 0))).astype(BF16)
    bg = gla_b_gate[l].reshape(1, LANES)
    ng = jnp.tile(gla_norm_g[l], GLA_HEADS).reshape(1, GLA_WIDTH)

    def block_diag(wb):
        on_diag = jnp.eye(LRU_BLOCKS, dtype=bool)[:, None, :, None]
        return jnp.where(on_diag, wb[:, :, None, :], 0.0).reshape(LRU_WIDTH, LRU_WIDTH)

    wax = jnp.concatenate([block_diag(lru_w_a[l]), block_diag(lru_w_x[l])], axis=1).astype(BF16)
    bax = jnp.concatenate([lru_b_a[l], lru_b_x[l]]).reshape(1, 2 * LRU_WIDTH)
    sp = jax.nn.softplus(-lru_lambda[l].astype(F32)).reshape(1, LRU_WIDTH)
    return dict(
        wg_p=wg_p, bg=bg, ng=ng,
        qn=mla_q_norm_g[l].reshape(1, MLA_Q_LORA), kvn=mla_kv_norm_g[l].reshape(1, MLA_KV_LORA),
        cw=lru_conv_w[l], cb=lru_conv_b[l].reshape(1, LRU_WIDTH), wax=wax, bax=bax, sp=sp)


def _rope_table(pos):
    half = MLA_ROPE // 2
    inv_freq = ROPE_THETA ** (-jnp.arange(half, dtype=F32) / half)
    ang = pos.astype(F32)[:, None] * inv_freq[None, :]
    cos, sin = jnp.cos(ang), jnp.sin(ang)
    return jnp.concatenate([cos, cos, sin, sin,
                            jnp.zeros((pos.shape[0], LANES - 2 * MLA_ROPE), F32)], axis=1)


def _group_layer(l, x, ln_in_gb, ada, lw, big, cs, s0, h0, cbuf, tiles, alpha, ln1, ln2, attend):
    b, t, _ = x.shape
    bb, tt, nb, gla_nb, gla_tt = tiles
    proj, qcat, ckv_new, kr_new, kcat = _inproj(x, ln_in_gb, ada, big["w_in_p"], lw["wg_p"], lw["bg"],
                                                lw["qn"], lw["kvn"], big["wq2"], cs, bb, tt, l)
    o_gla, s_new = _gla(proj, s0, lw["ng"], b, t, gla_nb, gla_tt)
    o_lru, h_new, conv_new = _lru(proj, cbuf, h0, lw["cw"], lw["cb"], lw["wax"], lw["bax"], lw["sp"],
                                  b, t, nb)
    o_lat = attend(qcat, kcat, ckv_new, kr_new)
    x2 = _mix_ffn(x, ln_in_gb, ada, o_gla, o_lru, o_lat.reshape(b * t, -1), big["w_out"], big["w_fold"],
                  ln1, big["wgu"], big["wd"], ln2, bb, tt, alpha, l)
    states = (s_new, h_new.reshape(b, LRU_WIDTH), conv_new,
              ckv_new.reshape(b, t, MLA_KV_LORA), kr_new.reshape(b, t, MLA_ROPE))
    return x2, states


def kernel(x_prompt, x_sample, c_prompt, c_sample, state_gla, state_lru, state_conv, cache_ckv, cache_krope, page_table, ln_in_g, ln_in_b, w_ada, b_ada, w_in, gla_w_gate, gla_b_gate, gla_norm_g, lru_conv_w, lru_conv_b, lru_w_a, lru_b_a, lru_w_x, lru_b_x, lru_lambda, mla_q_norm_g, mla_w_uq, mla_kv_norm_g, mla_w_uk, mla_w_uv, w_out, ln1_g, ln1_b, ffn_w_gu, ffn_w_down, ln2_g, ln2_b):
    bp, tp, d = x_prompt.shape
    bs, ts, _ = x_sample.shape
    depth = w_in.shape[0]
    n_pages = page_table.shape[1]
    past_len = n_pages * PAGE_SIZE
    alpha = (2.0 * depth) ** 0.25

    tiles_p = (1, min(512, tp), 2 if bp % 2 == 0 else 1, 8 if bp % 8 == 0 else 1, min(512, tp))
    tiles_s = (bs, ts, 4 if bs % 4 == 0 else 1, 16 if bs % 16 == 0 else 1, ts)
    pages_per_step = PAGES_PER_STEP if n_pages % PAGES_PER_STEP == 0 else n_pages

    ada = _ada_all(jnp.concatenate([c_prompt, c_sample], axis=0), w_ada, b_ada)
    ada = ada.reshape(depth, bp + bs, 6, d)
    xp, xs = x_prompt, x_sample
    ln_in_gb = (ln_in_g.reshape(1, d), ln_in_b.reshape(1, d))
    w_qlat, w_fold = _fold_weights(mla_w_uq, mla_w_uk, mla_w_uv, w_out)

    cache_krope_t = jnp.swapaxes(cache_krope, 2, 3)
    cs_p = _rope_table(jnp.arange(tp, dtype=jnp.int32))
    cs_s = jnp.tile(_rope_table(past_len + jnp.arange(ts, dtype=jnp.int32)), (bs, 1))
    zero_s = jnp.zeros((bp, GLA_HEADS, GLA_DK, GLA_DV), F32)
    zero_h = jnp.zeros((bp, 1, LRU_WIDTH), F32)
    zero_conv = jnp.zeros((bp, CONV_WIDTH - 1, LRU_WIDTH), F32)
    w_in_p, wq2 = _prep_stacked_weights(w_in, mla_w_uq, w_qlat)
    big = dict(w_in_p=w_in_p, wq2=wq2, w_out=w_out.astype(BF16), w_fold=w_fold,
               wgu=ffn_w_gu.astype(BF16), wd=ffn_w_down.astype(BF16))

    st_p, st_s = [], []
    for l in range(depth):
        lw = _prep_layer_weights(l, gla_w_gate, gla_b_gate, gla_norm_g, lru_conv_w, lru_conv_b,
                                 lru_w_a, lru_b_a, lru_w_x, lru_b_x, lru_lambda, mla_q_norm_g,
                                 mla_kv_norm_g)
        ln1 = (ln1_g[l].reshape(1, d), ln1_b[l].reshape(1, d))
        ln2 = (ln2_g[l].reshape(1, d), ln2_b[l].reshape(1, d))

        def attend_p(qcat, kcat, ckv_new, kr_new):
            return _attn_prompt(qcat, kcat, bp, tp)

        def attend_s(qcat, kcat, ckv_new, kr_new, l=l):
            return _attn_paged(qcat, ckv_new, kr_new, cache_ckv, cache_krope_t, page_table, l,
                               pages_per_step)

        xp, sp = _group_layer(l, xp, ln_in_gb, ada[l, :bp], lw, big, cs_p, zero_s, zero_h, zero_conv,
                              tiles_p, alpha, ln1, ln2, attend_p)
        xs, ss = _group_layer(l, xs, ln_in_gb, ada[l, bp:], lw, big, cs_s, state_gla[l],
                              state_lru[l].reshape(bs, 1, LRU_WIDTH), state_conv[l],
                              tiles_s, alpha, ln1, ln2, attend_s)
        st_p.append(sp)
        st_s.append(ss)

    def stk(outs, j):
        return jnp.stack([o[j] for o in outs])

    return (xp, xs, stk(st_p, 0), stk(st_s, 0), stk(st_p, 1), stk(st_s, 1), stk(st_p, 2), stk(st_s, 2),
            stk(st_p, 3), stk(st_s, 3), stk(st_p, 4), stk(st_s, 4))
```
